```python
import jax, jax.numpy as jnp
from jax import lax
import numpy as np

D_MODEL = 1024
BATCH = 16
SEQ = 2048
DEPTH = 1

MEM_LEN = 256
EPS = 1e-6

NSA_HEADS = 8
NSA_KV_GROUPS = 2
NSA_HEAD_DIM = 128
CMP_BLOCK = 32
CMP_STRIDE = 16
CMP_HIDDEN = 256
SEL_BLOCK = 64
N_SELECT = 16
WINDOW = 512
Q_BLOCK = 128
ROPE_THETA = 500000.0
ROT_DIM = NSA_HEAD_DIM // 4

RET_HEADS = 4
RET_QK_DIM = 64
RET_V_DIM = 128
RET_CHUNK = 128
RET_THETA = 10000.0

MEM_HEADS = 4
MEM_HEAD_DIM = 128

NSA_WIDTH = NSA_HEADS * NSA_HEAD_DIM
RET_WIDTH = RET_HEADS * RET_V_DIM
MEM_WIDTH = MEM_HEADS * MEM_HEAD_DIM
MIX_WIDTH = NSA_WIDTH + RET_WIDTH + MEM_WIDTH
NSA_KV_WIDTH = NSA_KV_GROUPS * NSA_HEAD_DIM
IN_SPLITS = (NSA_WIDTH, 3 * 2 * NSA_KV_WIDTH, 3 * NSA_HEADS, RET_HEADS * RET_QK_DIM,
             RET_HEADS * RET_QK_DIM, RET_WIDTH, MEM_WIDTH, MIX_WIDTH)
IN_WIDTH = 6168

kernel_name = "hymba_nsa_retention_memory_layer"


def rms_norm(x, g):
    xf = x.astype(jnp.float32)
    y = xf * lax.rsqrt(jnp.mean(xf * xf, axis=-1, keepdims=True) + EPS)
    return (y * g.astype(jnp.float32)).astype(x.dtype)


def rotary(x, pos, rot_dim, theta):
    half = rot_dim // 2
    inv_freq = theta ** (-jnp.arange(half, dtype=jnp.float32) / half)
    ang = pos.astype(jnp.float32)[..., None] * inv_freq
    cos, sin = jnp.cos(ang), jnp.sin(ang)
    x1 = x[..., :half].astype(jnp.float32)
    x2 = x[..., half:rot_dim].astype(jnp.float32)
    out = jnp.concatenate([x1 * cos - x2 * sin, x1 * sin + x2 * cos,
                           x[..., rot_dim:].astype(jnp.float32)], axis=-1)
    return out.astype(x.dtype)


def masked_softmax(s, mask):
    s = jnp.where(mask, s.astype(jnp.float32), -jnp.inf)
    m = jnp.max(s, axis=-1, keepdims=True)
    m = jnp.where(jnp.isfinite(m), m, 0.0)
    p = jnp.exp(s - m)
    return p / jnp.maximum(jnp.sum(p, axis=-1, keepdims=True), 1e-30)


def compress_blocks(k, pos_emb, w1, w2):
    B, G, T, dh = k.shape
    n_cmp = (T - CMP_BLOCK) // CMP_STRIDE + 1
    idx = np.arange(n_cmp)[:, None] * CMP_STRIDE + np.arange(CMP_BLOCK)[None, :]
    blocks = k[:, :, idx] + pos_emb.astype(k.dtype)
    h = jax.nn.gelu(blocks.reshape(B, G, n_cmp, CMP_BLOCK * dh) @ w1)
    return h @ w2


def nsa_heads(q, k_cmp, v_cmp, k_slc, v_slc, k_win, v_win, gates,
              cmp_pos_k, cmp_w1_k, cmp_w2_k, cmp_pos_v, cmp_w1_v, cmp_w2_v):
    B, G, R, T, dh = q.shape
    scale = dh ** -0.5
    t = np.arange(T)

    n_cmp = (T - CMP_BLOCK) // CMP_STRIDE + 1
    cmp_start = np.arange(n_cmp) * CMP_STRIDE
    kc = compress_blocks(k_cmp, cmp_pos_k, cmp_w1_k, cmp_w2_k)
    vc = compress_blocks(v_cmp, cmp_pos_v, cmp_w1_v, cmp_w2_v)
    s = jnp.einsum('bgrtd,bgnd->bgrtn', q, kc) * scale
    p_cmp = masked_softmax(s, (cmp_start + CMP_BLOCK - 1)[None, :] <= t[:, None])
    o_cmp = jnp.einsum('bgrtn,bgnd->bgrtd', p_cmp.astype(vc.dtype), vc)

    n_sb = T // SEL_BLOCK
    sel_start = np.arange(n_sb) * SEL_BLOCK
    overlap = np.clip(np.minimum(cmp_start[:, None] + CMP_BLOCK, sel_start[None, :] + SEL_BLOCK)
                      - np.maximum(cmp_start[:, None], sel_start[None, :]), 0, None)
    overlap = jnp.asarray(overlap.astype(np.float32) / CMP_BLOCK)
    imp = jnp.einsum('bgrtn,nj->bgtj', p_cmp, overlap)
    blk = np.arange(n_sb)[None, :]
    cur = (t // SEL_BLOCK)[:, None]
    forced = (blk == 0) | (blk == cur) | (blk == cur - 1)
    valid = blk * SEL_BLOCK <= t[:, None]
    imp = jnp.where(forced, jnp.inf, jnp.where(valid, imp, -jnp.inf))
    n_sel = min(N_SELECT, n_sb)
    _, sel_idx = lax.top_k(imp, n_sel)

    kb_slc = k_slc.reshape(B, G, n_sb, SEL_BLOCK, dh)
    vb_slc = v_slc.reshape(B, G, n_sb, SEL_BLOCK, dh)
    k_win_pad = jnp.pad(k_win, ((0, 0), (0, 0), (WINDOW, 0), (0, 0)))
    v_win_pad = jnp.pad(v_win, ((0, 0), (0, 0), (WINDOW, 0), (0, 0)))
    nqb = T // Q_BLOCK
    gather = jax.vmap(lambda kg, ig: kg[ig])

    def one_block(item):
        b = item // nqb
        t0 = (item % nqb) * Q_BLOCK
        qb = lax.dynamic_slice_in_dim(q[b], t0, Q_BLOCK, axis=2)
        tq = t0 + jnp.arange(Q_BLOCK)
        idx = lax.dynamic_slice_in_dim(sel_idx[b], t0, Q_BLOCK, axis=1)
        ks = gather(kb_slc[b], idx).reshape(G, Q_BLOCK, n_sel * SEL_BLOCK, dh)
        vs = gather(vb_slc[b], idx).reshape(G, Q_BLOCK, n_sel * SEL_BLOCK, dh)
        kpos = (idx[..., None] * SEL_BLOCK + jnp.arange(SEL_BLOCK)).reshape(G, Q_BLOCK, -1)
        s_sl = jnp.einsum('grqd,gqkd->grqk', qb, ks) * scale
        p_sl = masked_softmax(s_sl, (kpos <= tq[None, :, None])[:, None])
        o_sl = jnp.einsum('grqk,gqkd->grqd', p_sl.astype(vs.dtype), vs)
        kw = lax.dynamic_slice_in_dim(k_win_pad[b], t0, Q_BLOCK + WINDOW, axis=1)
        vw = lax.dynamic_slice_in_dim(v_win_pad[b], t0, Q_BLOCK + WINDOW, axis=1)
        kwpos = t0 - WINDOW + jnp.arange(Q_BLOCK + WINDOW)
        rel = tq[:, None] - kwpos[None, :]
        mask_w = (rel >= 0) & (rel < WINDOW) & (kwpos[None, :] >= 0)
        s_w = jnp.einsum('grqd,gkd->grqk', qb, kw) * scale
        p_w = masked_softmax(s_w, mask_w)
        o_w = jnp.einsum('grqk,gkd->grqd', p_w.astype(vw.dtype), vw)
        return o_sl, o_w

    o_slc, o_win = lax.map(one_block, jnp.arange(B * nqb))

    def unblock(o):
        return o.reshape(B, nqb, G, R, Q_BLOCK, dh).transpose(0, 2, 3, 1, 4, 5).reshape(B, G, R, T, dh)

    o_slc, o_win = unblock(o_slc), unblock(o_win)
    g = gates.reshape(B, T, 3, G, R).transpose(2, 0, 3, 4, 1)[..., None]
    o = g[0] * o_cmp + g[1] * o_slc + g[2] * o_win
    return o.transpose(0, 3, 1, 2, 4).reshape(B, T, G * R * dh)


def retention_heads(q, k, v, pos, gn_gain):
    B, H, T, dk = q.shape
    dv = v.shape[-1]
    q = rotary(q, pos, dk, RET_THETA)
    k = rotary(k, pos, dk, RET_THETA) * (dk ** -0.5)
    log_gamma = jnp.log1p(-(2.0 ** (-5.0 - jnp.arange(H, dtype=jnp.float32))))
    C = RET_CHUNK
    n = jnp.arange(C, dtype=jnp.float32)
    diff = n[:, None] - n[None, :]
    decay = jnp.where(diff >= 0, jnp.exp(log_gamma[:, None, None] * jnp.maximum(diff, 0.0)), 0.0)
    xi = jnp.exp(log_gamma[:, None] * (n + 1.0))[..., None]
    zeta = jnp.exp(log_gamma[:, None] * (C - 1.0 - n))[..., None]
    gamma_c = jnp.exp(log_gamma * C)[:, None, None]
    nc = T // C

    def to_chunks(a):
        return a.reshape(B, H, nc, C, a.shape[-1]).transpose(2, 0, 1, 3, 4)

    def step(state, inp):
        qc, kc, vc = inp
        inner = jnp.einsum('bhnd,bhmd->bhnm', qc, kc) * decay
        out = (jnp.einsum('bhnm,bhme->bhne', inner, vc)
               + jnp.einsum('bhnd,bhde->bhne', qc, state) * xi)
        state = state * gamma_c + jnp.einsum('bhmd,bhme->bhde', kc * zeta, vc)
        return state, out

    state0 = jnp.zeros((B, H, dk, dv), jnp.float32)
    _, o = lax.scan(step, state0, (to_chunks(q), to_chunks(k), to_chunks(v)))
    o = o.transpose(1, 0, 3, 2, 4).reshape(B, T, H, dv).astype(jnp.float32)
    mu = jnp.mean(o, axis=-1, keepdims=True)
    var = jnp.mean(jnp.square(o - mu), axis=-1, keepdims=True)
    o = ((o - mu) * lax.rsqrt(var + EPS)).reshape(B, T, H * dv) * gn_gain.astype(jnp.float32)
    return o.astype(v.dtype)


def memory_heads(qm, mem_n, w_mem_kv):
    B, M, _ = mem_n.shape
    kv = (mem_n @ w_mem_kv).reshape(B, M, 2, MEM_HEADS, MEM_HEAD_DIM)
    km, vm = kv[:, :, 0], kv[:, :, 1]
    s = jnp.einsum('bthd,bmhd->bhtm', qm, km) * (MEM_HEAD_DIM ** -0.5)
    p = jax.nn.softmax(s.astype(jnp.float32), axis=-1)
    o = jnp.einsum('bhtm,bmhd->bthd', p.astype(vm.dtype), vm)
    return o.reshape(B, qm.shape[1], MEM_WIDTH)


def hybrid_layer(x, mem, positions, norm_pre, w_in, cmp_pos_k, cmp_w1_k, cmp_w2_k,
                 cmp_pos_v, cmp_w1_v, cmp_w2_v, ret_gn, mem_norm, w_mem_kv, w_out, norm_post):
    B, T, _ = x.shape
    h = rms_norm(x, norm_pre)
    z = h @ w_in
    q_nsa, kv_nsa, g_nsa, q_ret, k_ret, v_ret, q_mem, gate = jnp.split(
        z, list(np.cumsum(IN_SPLITS)[:-1]), axis=-1)

    R = NSA_HEADS // NSA_KV_GROUPS
    q_nsa = q_nsa.reshape(B, T, NSA_KV_GROUPS, R, NSA_HEAD_DIM).transpose(0, 2, 3, 1, 4)
    q_nsa = rotary(q_nsa, positions[:, None, None, :], ROT_DIM, ROPE_THETA)
    kvn = kv_nsa.reshape(B, T, 3, 2, NSA_KV_GROUPS, NSA_HEAD_DIM).transpose(2, 3, 0, 4, 1, 5)
    kpos = positions[:, None, :]
    k_cmp = rotary(kvn[0, 0], kpos, ROT_DIM, ROPE_THETA)
    k_slc = rotary(kvn[1, 0], kpos, ROT_DIM, ROPE_THETA)
    k_win = rotary(kvn[2, 0], kpos, ROT_DIM, ROPE_THETA)
    gates = jax.nn.sigmoid(g_nsa).reshape(B, T, 3, NSA_HEADS)
    o_nsa = nsa_heads(q_nsa, k_cmp, kvn[0, 1], k_slc, kvn[1, 1], k_win, kvn[2, 1], gates,
                      cmp_pos_k, cmp_w1_k, cmp_w2_k, cmp_pos_v, cmp_w1_v, cmp_w2_v)

    q_ret = q_ret.reshape(B, T, RET_HEADS, RET_QK_DIM).transpose(0, 2, 1, 3)
    k_ret = k_ret.reshape(B, T, RET_HEADS, RET_QK_DIM).transpose(0, 2, 1, 3)
    v_ret = v_ret.reshape(B, T, RET_HEADS, RET_V_DIM).transpose(0, 2, 1, 3)
    o_ret = retention_heads(q_ret, k_ret, v_ret, positions[:, None, :], ret_gn)

    o_mem = memory_heads(q_mem.reshape(B, T, MEM_HEADS, MEM_HEAD_DIM), rms_norm(mem, mem_norm), w_mem_kv)

    mixed = jnp.concatenate([o_nsa, o_ret, o_mem], axis=-1) * jax.nn.silu(gate)
    y = mixed @ w_out
    return x + rms_norm(y, norm_post)


def setup_inputs(seed: int = 0) -> dict:
    key = jax.random.key(seed)
    ks = jax.random.split(key, 20)
    f32 = jnp.float32

    def w(k, shape, fan_in):
        return jax.random.normal(k, shape, f32) * (fan_in ** -0.5)

    def gain(k, shape):
        return 1.0 + 0.05 * jax.random.normal(k, shape, f32)

    x = jax.random.normal(ks[0], (BATCH, SEQ, D_MODEL), f32)
    mem = jax.random.normal(ks[1], (BATCH, MEM_LEN, D_MODEL), f32)
    offsets = jax.random.randint(ks[2], (BATCH, 1), 0, 1024, dtype=jnp.int32)
    positions = (offsets + jnp.arange(SEQ, dtype=jnp.int32)[None, :]).astype(jnp.int32)
    L = DEPTH
    return {
        "x": x,
        "mem": mem,
        "positions": positions,
        "norm_pre": gain(ks[3], (L, D_MODEL)),
        "w_in": w(ks[4], (L, D_MODEL, IN_WIDTH), D_MODEL),
        "cmp_pos_k": 0.1 * jax.random.normal(ks[5], (L, CMP_BLOCK, NSA_HEAD_DIM), f32),
        "cmp_w1_k": w(ks[6], (L, CMP_BLOCK * NSA_HEAD_DIM, CMP_HIDDEN), CMP_BLOCK * NSA_HEAD_DIM),
        "cmp_w2_k": w(ks[7], (L, CMP_HIDDEN, NSA_HEAD_DIM), CMP_HIDDEN),
        "cmp_pos_v": 0.1 * jax.random.normal(ks[8], (L, CMP_BLOCK, NSA_HEAD_DIM), f32),
        "cmp_w1_v": w(ks[9], (L, CMP_BLOCK * NSA_HEAD_DIM, CMP_HIDDEN), CMP_BLOCK * NSA_HEAD_DIM),
        "cmp_w2_v": w(ks[10], (L, CMP_HIDDEN, NSA_HEAD_DIM), CMP_HIDDEN),
        "ret_gn": gain(ks[11], (L, RET_WIDTH)),
        "mem_norm": gain(ks[12], (L, D_MODEL)),
        "w_mem_kv": w(ks[13], (L, D_MODEL, 2 * MEM_WIDTH), D_MODEL),
        "w_out": w(ks[14], (L, MIX_WIDTH, D_MODEL), MIX_WIDTH),
        "norm_post": gain(ks[15], (L, D_MODEL)),
    }


def reference(x, mem, positions, norm_pre, w_in, cmp_pos_k, cmp_w1_k, cmp_w2_k,
              cmp_pos_v, cmp_w1_v, cmp_w2_v, ret_gn, mem_norm, w_mem_kv, w_out, norm_post):
    for layer in range(DEPTH):
        x = hybrid_layer(x, mem, positions, norm_pre[layer], w_in[layer],
                         cmp_pos_k[layer], cmp_w1_k[layer], cmp_w2_k[layer],
                         cmp_pos_v[layer], cmp_w1_v[layer], cmp_w2_v[layer],
                         ret_gn[layer], mem_norm[layer], w_mem_kv[layer],
                         w_out[layer], norm_post[layer])
    return x
```

```python
import functools
import math

import jax
import jax.numpy as jnp
import numpy as np
from jax import lax
from jax.experimental import pallas as pl
from jax.experimental.pallas import tpu as pltpu

D_MODEL = 1024
MEM_LEN = 256
EPS = 1e-6

NSA_HEADS = 8
NSA_KV_GROUPS = 2
NSA_REP = NSA_HEADS // NSA_KV_GROUPS
NSA_HEAD_DIM = 128
CMP_BLOCK = 32
CMP_STRIDE = 16
CMP_HIDDEN = 256
SEL_BLOCK = 64
N_SELECT = 16
WINDOW = 512
Q_BLOCK = 128
ROPE_THETA = 500000.0
ROT_DIM = NSA_HEAD_DIM // 4

RET_HEADS = 4
RET_QK_DIM = 64
RET_V_DIM = 128
RET_CHUNK = 128
RET_THETA = 10000.0

MEM_HEADS = 4
MEM_HEAD_DIM = 128

NSA_WIDTH = NSA_HEADS * NSA_HEAD_DIM
NSA_KV_ALL = 3 * 2 * NSA_KV_GROUPS * NSA_HEAD_DIM
RET_QK_WIDTH = RET_HEADS * RET_QK_DIM
RET_WIDTH = RET_HEADS * RET_V_DIM
MEM_WIDTH = MEM_HEADS * MEM_HEAD_DIM
MIX_WIDTH = NSA_WIDTH + RET_WIDTH + MEM_WIDTH
GATE_PAD = 128

LANES = 128
NEG = -1e30
VMEM_LIMIT = 56 * 1024 * 1024

ROW_TILE = 256
SLC_CHUNK = 512

_NT = (((1,), (1,)), ((), ()))
_TN = (((0,), (0,)), ((), ()))

bf16 = jnp.bfloat16
f32 = jnp.float32


def _dot(a, b):
    return jnp.dot(a, b, preferred_element_type=f32)


def _dot_nt(a, b):
    return lax.dot_general(a, b, _NT, preferred_element_type=f32)


def _dot_tn(a, b):
    return lax.dot_general(a, b, _TN, preferred_element_type=f32)


def _rms(x, g):
    return x * lax.rsqrt(jnp.mean(x * x, axis=-1, keepdims=True) + EPS) * g


def _rotate_heads(acc, c, s1, s2, shift):
    outs = []
    for j in range(acc.shape[1] // LANES):
        a = acc[:, j * LANES:(j + 1) * LANES]
        outs.append(a * c + pltpu.roll(a, shift, 1) * s1 + pltpu.roll(a, LANES - shift, 1) * s2)
    return outs


def _in_proj_kernel(x_ref, pos_ref, g_ref, w_ref, invn_ref, invr_ref,
                    q_ref, kv_ref, rq_ref, rk_ref, rv_ref, mq_ref, gs_ref, gg_ref):
    x = x_ref[...]
    hb = _rms(x, g_ref[...]).astype(bf16)
    pos = pos_ref[...].astype(f32)
    lane = lax.broadcasted_iota(jnp.int32, (x.shape[0], LANES), 1)

    half = ROT_DIM // 2
    ang = pos * invn_ref[...]
    cn, sn = jnp.cos(ang), jnp.sin(ang)
    s1n = jnp.where((lane >= half) & (lane < ROT_DIM), sn, 0.0)
    s2n = jnp.where(lane < half, -sn, 0.0)
    halfr = RET_QK_DIM // 2
    angr = pos * invr_ref[...]
    cr, sr = jnp.cos(angr), jnp.sin(angr)
    lr = lane & (RET_QK_DIM - 1)
    s1r = jnp.where(lr >= halfr, sr, 0.0)
    s2r = jnp.where(lr < halfr, -sr, 0.0)

    col = 0
    for j in range(NSA_WIDTH // 512):
        acc = _dot(hb, w_ref[:, col:col + 512])
        for i, o in enumerate(_rotate_heads(acc, cn, s1n, s2n, half)):
            q_ref[:, j * 512 + i * LANES: j * 512 + (i + 1) * LANES] = o.astype(bf16)
        col += 512
    for br in range(3):
        acc = _dot(hb, w_ref[:, col:col + 512])
        for i, o in enumerate(_rotate_heads(acc[:, :256], cn, s1n, s2n, half)):
            kv_ref[:, br * 512 + i * LANES: br * 512 + (i + 1) * LANES] = o.astype(bf16)
        kv_ref[:, br * 512 + 256: br * 512 + 512] = acc[:, 256:].astype(bf16)
        col += 512
    acc = _dot(hb, w_ref[:, col:col + 512])
    rot = _rotate_heads(acc, cr, s1r, s2r, halfr)
    for i in range(2):
        rq_ref[:, i * LANES:(i + 1) * LANES] = rot[i].astype(bf16)
        rk_ref[:, i * LANES:(i + 1) * LANES] = (rot[2 + i] * (RET_QK_DIM ** -0.5)).astype(bf16)
    col += 512
    rv_ref[...] = _dot(hb, w_ref[:, col:col + 512]).astype(bf16)
    col += 512
    mq_ref[...] = _dot(hb, w_ref[:, col:col + 512]).astype(bf16)
    col += 512
    for j in range(MIX_WIDTH // 512):
        acc = _dot(hb, w_ref[:, col:col + 512])
        gs_ref[:, j * 512:(j + 1) * 512] = (acc * (1.0 / (1.0 + jnp.exp(-acc)))).astype(bf16)
        col += 512
    acc = _dot(hb, w_ref[:, col:col + 2 * GATE_PAD])
    gg_ref[...] = 1.0 / (1.0 + jnp.exp(-acc))


def _in_proj(x2, pos2, norm_pre, w_r, invn, invr):
    bt = x2.shape[0]
    tm = ROW_TILE
    win = w_r.shape[1]
    row = lambda i: (i, 0)
    const = lambda i: (0, 0)
    widths = (NSA_WIDTH, NSA_KV_ALL, RET_QK_WIDTH, RET_QK_WIDTH, RET_WIDTH, MEM_WIDTH, MIX_WIDTH)
    out_shape = [jax.ShapeDtypeStruct((bt, w), bf16) for w in widths]
    out_shape.append(jax.ShapeDtypeStruct((bt, 2 * GATE_PAD), f32))
    out_specs = [pl.BlockSpec((tm, w), row) for w in widths] + [pl.BlockSpec((tm, 2 * GATE_PAD), row)]
    return pl.pallas_call(
        _in_proj_kernel,
        grid=(bt // tm,),
        in_specs=[
            pl.BlockSpec((tm, D_MODEL), row),
            pl.BlockSpec((tm, 1), row),
            pl.BlockSpec((1, D_MODEL), const),
            pl.BlockSpec((D_MODEL, win), const),
            pl.BlockSpec((1, LANES), const),
            pl.BlockSpec((1, LANES), const),
        ],
        out_specs=out_specs,
        out_shape=out_shape,
        compiler_params=pltpu.CompilerParams(
            dimension_semantics=("arbitrary",), vmem_limit_bytes=VMEM_LIMIT),
        name="in_proj",
    )(x2, pos2, norm_pre, w_r, invn, invr)


def _mem_kv_kernel(m_ref, g_ref, w_ref, o_ref):
    hb = _rms(m_ref[...], g_ref[...]).astype(bf16)
    o_ref[...] = _dot(hb, w_ref[...]).astype(bf16)


def _mem_kv(mem2, mem_norm, w_kv):
    n = mem2.shape[0]
    tm = MEM_LEN
    return pl.pallas_call(
        _mem_kv_kernel,
        grid=(n // tm,),
        in_specs=[
            pl.BlockSpec((tm, D_MODEL), lambda i: (i, 0)),
            pl.BlockSpec((1, D_MODEL), lambda i: (0, 0)),
            pl.BlockSpec((D_MODEL, 2 * MEM_WIDTH), lambda i: (0, 0)),
        ],
        out_specs=pl.BlockSpec((tm, 2 * MEM_WIDTH), lambda i: (i, 0)),
        out_shape=jax.ShapeDtypeStruct((n, 2 * MEM_WIDTH), bf16),
        compiler_params=pltpu.CompilerParams(
            dimension_semantics=("arbitrary",), vmem_limit_bytes=VMEM_LIMIT),
        name="mem_kv",
    )(mem2, mem_norm, w_kv)


def _compress_kernel(kv_ref, pos_ref, w1_ref, w2_ref, o_ref, xf_ref):
    seq = kv_ref.shape[0]
    nslot = seq // CMP_STRIDE
    xf_ref[...] = kv_ref[...].astype(f32)
    pos = pos_ref[0]
    ha = jnp.zeros((nslot, CMP_HIDDEN), f32)
    hb = jnp.zeros((nslot, CMP_HIDDEN), f32)
    for p in range(CMP_STRIDE):
        a = xf_ref[pl.ds(p, nslot, stride=CMP_STRIDE), :]
        la = (a + pos[p:p + 1, :]).astype(bf16)
        lb = (a + pos[CMP_STRIDE + p:CMP_STRIDE + p + 1, :]).astype(bf16)
        ha = ha + _dot(la, w1_ref[0, p * NSA_HEAD_DIM:(p + 1) * NSA_HEAD_DIM, :])
        hb = hb + _dot(lb, w1_ref[0, (CMP_STRIDE + p) * NSA_HEAD_DIM:(CMP_STRIDE + p + 1) * NSA_HEAD_DIM, :])
    h = ha + pltpu.roll(hb, nslot - 1, 0)
    h = jax.nn.gelu(h)
    out = _dot(h.astype(bf16), w2_ref[0])
    rowi = lax.broadcasted_iota(jnp.int32, out.shape, 0)
    o_ref[0, 0] = jnp.where(rowi < nslot - 1, out, 0.0).astype(bf16)


def _compress(kv, cmp_pos, cmp_w1, cmp_w2, batch, seq):
    nslot = seq // CMP_STRIDE
    return pl.pallas_call(
        _compress_kernel,
        grid=(batch, 2 * NSA_KV_GROUPS),
        in_specs=[
            pl.BlockSpec((seq, NSA_HEAD_DIM), lambda b, j: (b, j)),
            pl.BlockSpec((1, CMP_BLOCK, NSA_HEAD_DIM), lambda b, j: (j // NSA_KV_GROUPS, 0, 0)),
            pl.BlockSpec((1, CMP_BLOCK * NSA_HEAD_DIM, CMP_HIDDEN), lambda b, j: (j // NSA_KV_GROUPS, 0, 0)),
            pl.BlockSpec((1, CMP_HIDDEN, NSA_HEAD_DIM), lambda b, j: (j // NSA_KV_GROUPS, 0, 0)),
        ],
        out_specs=pl.BlockSpec((1, 1, nslot, NSA_HEAD_DIM), lambda b, j: (b, j, 0, 0)),
        out_shape=jax.ShapeDtypeStruct((batch, 2 * NSA_KV_GROUPS, nslot, NSA_HEAD_DIM), bf16),
        scratch_shapes=[pltpu.VMEM((seq, NSA_HEAD_DIM), f32)],
        compiler_params=pltpu.CompilerParams(
            dimension_semantics=("arbitrary", "arbitrary"), vmem_limit_bytes=VMEM_LIMIT),
        name="compress",
    )(kv, cmp_pos, cmp_w1, cmp_w2)


def _nsa_kernel(q_ref, kc_ref, vc_ref, ks_ref, vs_ref, kw_ref, vw_ref, g_ref, ovl_ref, o_ref):
    qi = pl.program_id(2)
    t0 = qi * Q_BLOCK
    scale = NSA_HEAD_DIM ** -0.5
    rows = NSA_REP * Q_BLOCK
    nslot = kc_ref.shape[2]
    n_cmp = nslot - 1

    q = q_ref[...]
    q4 = jnp.concatenate([q[:, r * NSA_HEAD_DIM:(r + 1) * NSA_HEAD_DIM] for r in range(NSA_REP)], axis=0)
    trow = t0 + (lax.broadcasted_iota(jnp.int32, (rows, 1), 0) & (Q_BLOCK - 1))

    s = _dot_nt(q4, kc_ref[0, 0]) * scale
    ncol = lax.broadcasted_iota(jnp.int32, (1, nslot), 1)
    cmask = (ncol * CMP_STRIDE + (CMP_BLOCK - 1) <= trow) & (ncol < n_cmp)
    s = jnp.where(cmask, s, NEG)
    m = jnp.max(s, axis=1, keepdims=True)
    p = jnp.where(cmask, jnp.exp(s - m), 0.0)
    p = p / jnp.maximum(jnp.sum(p, axis=1, keepdims=True), 1e-30)
    o_cmp = _dot(p.astype(bf16), vc_ref[0, 0])

    psum = p[0:Q_BLOCK]
    for r in range(1, NSA_REP):
        psum = psum + p[r * Q_BLOCK:(r + 1) * Q_BLOCK]
    imp_t = lax.dot_general(ovl_ref[...], psum, _NT, precision=lax.Precision.HIGHEST,
                            preferred_element_type=f32)
    n_sb = imp_t.shape[0]
    jblk = lax.broadcasted_iota(jnp.int32, (n_sb, Q_BLOCK), 0)
    tl = t0 + lax.broadcasted_iota(jnp.int32, (n_sb, Q_BLOCK), 1)
    cur = tl >> 6
    forced = (jblk == 0) | (jblk == cur) | (jblk == cur - 1)
    valid = jblk * SEL_BLOCK <= tl
    v = jnp.where(forced, jnp.inf, jnp.where(valid, imp_t, -jnp.inf))
    cnt = jnp.zeros((n_sb, Q_BLOCK), f32)
    for i in range(n_sb):
        vi = v[i:i + 1, :]
        ahead = (vi > v) | ((vi == v) & (jblk > i))
        cnt = cnt + jnp.where(ahead, 1.0, 0.0)
    sel_t = jnp.where(cnt < float(N_SELECT), 1.0, 0.0).astype(bf16)

    jrow = lax.broadcasted_iota(jnp.int32, (n_sb, SLC_CHUNK), 0)
    kcol = lax.broadcasted_iota(jnp.int32, (n_sb, SLC_CHUNK), 1)
    kpos0 = lax.broadcasted_iota(jnp.int32, (1, SLC_CHUNK), 1)

    def slc_body(c, carry):
        m_i, l_i, acc = carry
        base = pl.multiple_of(c * SLC_CHUNK, SLC_CHUNK)
        ks = ks_ref[pl.ds(base, SLC_CHUNK), :]
        vs = vs_ref[pl.ds(base, SLC_CHUNK), :]
        sc = _dot_nt(q4, ks) * scale
        expand = jnp.where(((base + kcol) >> 6) == jrow, 1.0, 0.0).astype(bf16)
        mk = _dot_tn(sel_t, expand)
        mk4 = jnp.concatenate([mk] * NSA_REP, axis=0)
        msk = (mk4 > 0.5) & (base + kpos0 <= trow)
        sc = jnp.where(msk, sc, NEG)
        m_new = jnp.maximum(m_i, jnp.max(sc, axis=1, keepdims=True))
        alpha = jnp.exp(m_i - m_new)
        pc = jnp.where(msk, jnp.exp(sc - m_new), 0.0)
        l_new = alpha * l_i + jnp.sum(pc, axis=1, keepdims=True)
        acc = alpha * acc + _dot(pc.astype(bf16), vs)
        return m_new, l_new, acc

    n_chunks = (t0 + Q_BLOCK + SLC_CHUNK - 1) // SLC_CHUNK
    m0 = jnp.full((rows, 1), NEG, f32)
    l0 = jnp.zeros((rows, 1), f32)
    a0 = jnp.zeros((rows, NSA_HEAD_DIM), f32)
    _, l_s, acc_s = lax.fori_loop(0, n_chunks, slc_body, (m0, l0, a0))
    o_slc = acc_s / jnp.maximum(l_s, 1e-30)

    wlen = WINDOW + Q_BLOCK
    start = pl.multiple_of(jnp.maximum(t0 - WINDOW, 0), Q_BLOCK)
    kw = kw_ref[pl.ds(start, wlen), :]
    vw = vw_ref[pl.ds(start, wlen), :]
    sw = _dot_nt(q4, kw) * scale
    rel = trow - (start + lax.broadcasted_iota(jnp.int32, (1, wlen), 1))
    wmask = (rel >= 0) & (rel < WINDOW)
    sw = jnp.where(wmask, sw, NEG)
    mw = jnp.max(sw, axis=1, keepdims=True)
    pw = jnp.where(wmask, jnp.exp(sw - mw), 0.0)
    lw = jnp.maximum(jnp.sum(pw, axis=1, keepdims=True), 1e-30)
    o_win = _dot(pw.astype(bf16), vw) / lw

    gs = g_ref[...]
    for r in range(NSA_REP):
        sl = slice(r * Q_BLOCK, (r + 1) * Q_BLOCK)
        o = (gs[:, r:r + 1] * o_cmp[sl]
             + gs[:, NSA_REP + r:NSA_REP + r + 1] * o_slc[sl]
             + gs[:, 2 * NSA_REP + r:2 * NSA_REP + r + 1] * o_win[sl])
        o_ref[:, r * NSA_HEAD_DIM:(r + 1) * NSA_HEAD_DIM] = o.astype(bf16)


def _nsa(q, cmp, kv, gsig, ovl_t, batch, seq):
    nq = seq // Q_BLOCK
    gw = NSA_REP * NSA_HEAD_DIM
    nslot = cmp.shape[2]
    G = NSA_KV_GROUPS
    kvspec = lambda off: pl.BlockSpec((seq, NSA_HEAD_DIM), lambda b, g, i, off=off: (b, off + g))
    return pl.pallas_call(
        _nsa_kernel,
        grid=(batch, G, nq),
        in_specs=[
            pl.BlockSpec((Q_BLOCK, gw), lambda b, g, i: (b * nq + i, g)),
            pl.BlockSpec((1, 1, nslot, NSA_HEAD_DIM), lambda b, g, i: (b, g, 0, 0)),
            pl.BlockSpec((1, 1, nslot, NSA_HEAD_DIM), lambda b, g, i: (b, G + g, 0, 0)),
            kvspec(2 * G), kvspec(3 * G), kvspec(4 * G), kvspec(5 * G),
            pl.BlockSpec((Q_BLOCK, GATE_PAD), lambda b, g, i: (b * nq + i, g)),
            pl.BlockSpec(ovl_t.shape, lambda b, g, i: (0, 0)),
        ],
        out_specs=pl.BlockSpec((Q_BLOCK, gw), lambda b, g, i: (b * nq + i, g)),
        out_shape=jax.ShapeDtypeStruct((batch * seq, NSA_WIDTH), bf16),
        compiler_params=pltpu.CompilerParams(
            dimension_semantics=("arbitrary", "arbitrary", "arbitrary"), vmem_limit_bytes=VMEM_LIMIT),
        name="nsa",
    )(q, cmp, cmp, kv, kv, kv, kv, gsig, ovl_t)


def _retention_kernel(q_ref, k_ref, v_ref, gn_ref, o_ref):
    seq = q_ref.shape[0]
    C = RET_CHUNK
    n = lax.broadcasted_iota(jnp.int32, (C, C), 0)
    mcol = lax.broadcasted_iota(jnp.int32, (C, C), 1)
    diff = (n - mcol).astype(f32)
    nvec = lax.broadcasted_iota(jnp.int32, (C, 1), 0).astype(f32)
    decay, xi, zeta, gamma_c = [], [], [], []
    for h in range(RET_HEADS):
        lg = math.log1p(-(2.0 ** (-5.0 - h)))
        decay.append(jnp.where(diff >= 0, jnp.exp(lg * jnp.maximum(diff, 0.0)), 0.0))
        xi.append(jnp.exp(lg * (nvec + 1.0)))
        zeta.append(jnp.exp(lg * (C - 1.0 - nvec)))
        gamma_c.append(math.exp(lg * C))
    gn = gn_ref[...]

    def body(c, states):
        base = pl.multiple_of(c * C, C)
        new_states = []
        for h in range(RET_HEADS):
            qc = q_ref[pl.ds(base, C), h * RET_QK_DIM:(h + 1) * RET_QK_DIM]
            kc = k_ref[pl.ds(base, C), h * RET_QK_DIM:(h + 1) * RET_QK_DIM]
            vc = v_ref[pl.ds(base, C), h * RET_V_DIM:(h + 1) * RET_V_DIM]
            inner = _dot_nt(qc, kc) * decay[h]
            out = _dot(inner.astype(bf16), vc) + _dot(qc, states[h].astype(bf16)) * xi[h]
            kz = (kc.astype(f32) * zeta[h]).astype(bf16)
            new_states.append(states[h] * gamma_c[h] + _dot_tn(kz, vc))
            mu = jnp.mean(out, axis=-1, keepdims=True)
            d = out - mu
            var = jnp.mean(d * d, axis=-1, keepdims=True)
            o = d * lax.rsqrt(var + EPS) * gn[:, h * RET_V_DIM:(h + 1) * RET_V_DIM]
            o_ref[pl.ds(base, C), h * RET_V_DIM:(h + 1) * RET_V_DIM] = o.astype(bf16)
        return tuple(new_states)

    s0 = tuple(jnp.zeros((RET_QK_DIM, RET_V_DIM), f32) for _ in range(RET_HEADS))
    lax.fori_loop(0, seq // C, body, s0)


def _retention(rq, rk, rv, ret_gn, batch, seq):
    return pl.pallas_call(
        _retention_kernel,
        grid=(batch,),
        in_specs=[
            pl.BlockSpec((seq, RET_QK_WIDTH), lambda b: (b, 0)),
            pl.BlockSpec((seq, RET_QK_WIDTH), lambda b: (b, 0)),
            pl.BlockSpec((seq, RET_WIDTH), lambda b: (b, 0)),
            pl.BlockSpec((1, RET_WIDTH), lambda b: (0, 0)),
        ],
        out_specs=pl.BlockSpec((seq, RET_WIDTH), lambda b: (b, 0)),
        out_shape=jax.ShapeDtypeStruct((batch * seq, RET_WIDTH), bf16),
        compiler_params=pltpu.CompilerParams(
            dimension_semantics=("arbitrary",), vmem_limit_bytes=VMEM_LIMIT),
        name="retention",
    )(rq, rk, rv, ret_gn)


def _out_proj_kernel(x_ref, on_ref, or_ref, mq_ref, kvm_ref, gs_ref, w_ref, g_ref, o_ref):
    scale = MEM_HEAD_DIM ** -0.5
    y = _dot((on_ref[...].astype(f32) * gs_ref[:, 0:NSA_WIDTH].astype(f32)).astype(bf16),
             w_ref[0:NSA_WIDTH, :])
    y = y + _dot((or_ref[...].astype(f32)
                  * gs_ref[:, NSA_WIDTH:NSA_WIDTH + RET_WIDTH].astype(f32)).astype(bf16),
                 w_ref[NSA_WIDTH:NSA_WIDTH + RET_WIDTH, :])
    off = NSA_WIDTH + RET_WIDTH
    for h in range(MEM_HEADS):
        sl = slice(h * MEM_HEAD_DIM, (h + 1) * MEM_HEAD_DIM)
        s = _dot_nt(mq_ref[:, sl], kvm_ref[:, sl]) * scale
        m = jnp.max(s, axis=1, keepdims=True)
        p = jnp.exp(s - m)
        p = p / jnp.sum(p, axis=1, keepdims=True)
        om = _dot(p.astype(bf16), kvm_ref[:, MEM_WIDTH + h * MEM_HEAD_DIM:MEM_WIDTH + (h + 1) * MEM_HEAD_DIM])
        gate = gs_ref[:, off + h * MEM_HEAD_DIM:off + (h + 1) * MEM_HEAD_DIM].astype(f32)
        y = y + _dot((om * gate).astype(bf16),
                     w_ref[off + h * MEM_HEAD_DIM:off + (h + 1) * MEM_HEAD_DIM, :])
    o_ref[...] = x_ref[...] + _rms(y, g_ref[...])


def _out_proj(x2, o_nsa, o_ret, mq, kvm, gsilu, w_out, norm_post, seq):
    bt = x2.shape[0]
    tm = ROW_TILE
    per_b = seq // tm
    row = lambda i: (i, 0)
    const = lambda i: (0, 0)
    return pl.pallas_call(
        _out_proj_kernel,
        grid=(bt // tm,),
        in_specs=[
            pl.BlockSpec((tm, D_MODEL), row),
            pl.BlockSpec((tm, NSA_WIDTH), row),
            pl.BlockSpec((tm, RET_WIDTH), row),
            pl.BlockSpec((tm, MEM_WIDTH), row),
            pl.BlockSpec((MEM_LEN, 2 * MEM_WIDTH), lambda i: (i // per_b, 0)),
            pl.BlockSpec((tm, MIX_WIDTH), row),
            pl.BlockSpec((MIX_WIDTH, D_MODEL), const),
            pl.BlockSpec((1, D_MODEL), const),
        ],
        out_specs=pl.BlockSpec((tm, D_MODEL), row),
        out_shape=jax.ShapeDtypeStruct((bt, D_MODEL), f32),
        compiler_params=pltpu.CompilerParams(
            dimension_semantics=("arbitrary",), vmem_limit_bytes=VMEM_LIMIT),
        name="out_proj",
    )(x2, o_nsa, o_ret, mq, kvm, gsilu, w_out, norm_post)


def _relayout_w_in(w):
    o = 0
    q_nsa = w[:, o:o + NSA_WIDTH]; o += NSA_WIDTH
    kv = w[:, o:o + NSA_KV_ALL]; o += NSA_KV_ALL
    gates = w[:, o:o + 3 * NSA_HEADS]; o += 3 * NSA_HEADS
    rest = w[:, o:]
    gates = gates.reshape(D_MODEL, 3, NSA_KV_GROUPS, NSA_REP).transpose(0, 2, 1, 3)
    gates = gates.reshape(D_MODEL, NSA_KV_GROUPS, 3 * NSA_REP)
    gates = jnp.pad(gates, ((0, 0), (0, 0), (0, GATE_PAD - 3 * NSA_REP)))
    gates = gates.reshape(D_MODEL, NSA_KV_GROUPS * GATE_PAD)
    return jnp.concatenate([q_nsa, kv, rest, gates], axis=1).astype(bf16)


def _overlap_t(seq):
    n_slot = seq // CMP_STRIDE
    n_sb = seq // SEL_BLOCK
    cmp_start = np.arange(n_slot) * CMP_STRIDE
    sel_start = np.arange(n_sb) * SEL_BLOCK
    ov = np.clip(np.minimum(cmp_start[None, :] + CMP_BLOCK, sel_start[:, None] + SEL_BLOCK)
                 - np.maximum(cmp_start[None, :], sel_start[:, None]), 0, None)
    return jnp.asarray(ov.astype(np.float32) / CMP_BLOCK)


def kernel(x, mem, positions, norm_pre, w_in, cmp_pos_k, cmp_w1_k, cmp_w2_k, cmp_pos_v, cmp_w1_v,
           cmp_w2_v, ret_gn, mem_norm, w_mem_kv, w_out, norm_post):
    depth = norm_pre.shape[0]
    batch, seq, _ = x.shape
    assert x.shape[2] == D_MODEL and mem.shape[1:] == (MEM_LEN, D_MODEL)
    assert seq % SLC_CHUNK == 0 and seq >= WINDOW + Q_BLOCK

    half = ROT_DIM // 2
    inv_n = ROPE_THETA ** (-jnp.arange(half, dtype=f32) / half)
    invn = jnp.concatenate([inv_n, inv_n, jnp.zeros((LANES - ROT_DIM,), f32)])[None, :]
    halfr = RET_QK_DIM // 2
    inv_r = RET_THETA ** (-jnp.arange(halfr, dtype=f32) / halfr)
    invr = jnp.tile(inv_r, LANES // halfr)[None, :]
    ovl_t = _overlap_t(seq)
    pos2 = positions.reshape(batch * seq, 1)
    mem2 = mem.reshape(batch * MEM_LEN, D_MODEL)

    x2 = x.reshape(batch * seq, D_MODEL)
    for layer in range(depth):
        w_r = _relayout_w_in(w_in[layer])
        q, kv, rq, rk, rv, mq, gsilu, gsig = _in_proj(x2, pos2, norm_pre[layer][None, :], w_r, invn, invr)
        kvm = _mem_kv(mem2, mem_norm[layer][None, :], w_mem_kv[layer].astype(bf16))
        cmp = _compress(
            kv,
            jnp.stack([cmp_pos_k[layer], cmp_pos_v[layer]]),
            jnp.stack([cmp_w1_k[layer], cmp_w1_v[layer]]).astype(bf16),
            jnp.stack([cmp_w2_k[layer], cmp_w2_v[layer]]).astype(bf16),
            batch, seq)
        o_nsa = _nsa(q, cmp, kv, gsig, ovl_t, batch, seq)
        o_ret = _retention(rq, rk, rv, ret_gn[layer][None, :], batch, seq)
        x2 = _out_proj(x2, o_nsa, o_ret, mq, kvm, gsilu, w_out[layer].astype(bf16),
                       norm_post[layer][None, :], seq)
    return x2.reshape(batch, seq, D_MODEL)
```

```python
import functools
import math

import jax
import jax.numpy as jnp
import numpy as np
from jax import lax
from jax.experimental import pallas as pl
from jax.experimental.pallas import tpu as pltpu

D_MODEL = 1024
MEM_LEN = 256
EPS = 1e-6

NSA_HEADS = 8
NSA_KV_GROUPS = 2
NSA_REP = NSA_HEADS // NSA_KV_GROUPS
NSA_HEAD_DIM = 128
CMP_BLOCK = 32
CMP_STRIDE = 16
CMP_HIDDEN = 256
SEL_BLOCK = 64
N_SELECT = 16
WINDOW = 512
Q_BLOCK = 128
ROPE_THETA = 500000.0
ROT_DIM = NSA_HEAD_DIM // 4

RET_HEADS = 4
RET_QK_DIM = 64
RET_V_DIM = 128
RET_CHUNK = 128
RET_THETA = 10000.0

MEM_HEADS = 4
MEM_HEAD_DIM = 128

NSA_WIDTH = NSA_HEADS * NSA_HEAD_DIM
NSA_KV_ALL = 3 * 2 * NSA_KV_GROUPS * NSA_HEAD_DIM
RET_QK_WIDTH = RET_HEADS * RET_QK_DIM
RET_WIDTH = RET_HEADS * RET_V_DIM
MEM_WIDTH = MEM_HEADS * MEM_HEAD_DIM
MIX_WIDTH = NSA_WIDTH + RET_WIDTH + MEM_WIDTH
GATE_PAD = 128

LANES = 128
SUBLANES = 8
NEG = -1e30
VMEM_LIMIT = 56 * 1024 * 1024

ROW_TILE = 256
SLC_CHUNK = 512

_NT = (((1,), (1,)), ((), ()))
_TN = (((0,), (0,)), ((), ()))

bf16 = jnp.bfloat16
f32 = jnp.float32


def _dot(a, b):
    return jnp.dot(a, b, preferred_element_type=f32)


def _dot_nt(a, b):
    return lax.dot_general(a, b, _NT, preferred_element_type=f32)


def _dot_tn(a, b):
    return lax.dot_general(a, b, _TN, preferred_element_type=f32)


def _col_reduce(op, x):
    slabs = [x[i:i + SUBLANES] for i in range(0, x.shape[0], SUBLANES)]
    while len(slabs) > 1:
        nxt = [op(slabs[i], slabs[i + 1]) for i in range(0, len(slabs) - 1, 2)]
        if len(slabs) % 2:
            nxt.append(slabs[-1])
        slabs = nxt
    red = jnp.max if op is jnp.maximum else jnp.sum
    return red(slabs[0], axis=0, keepdims=True)


def _rms(x, g):
    return x * lax.rsqrt(jnp.mean(x * x, axis=-1, keepdims=True) + EPS) * g


def _rotate_heads(acc, c, s1, s2, shift):
    outs = []
    for j in range(acc.shape[1] // LANES):
        a = acc[:, j * LANES:(j + 1) * LANES]
        outs.append(a * c + pltpu.roll(a, shift, 1) * s1 + pltpu.roll(a, LANES - shift, 1) * s2)
    return outs


def _in_proj_kernel(x_ref, pos_ref, g_ref, w_ref, invn_ref, invr_ref,
                    q_ref, kv_ref, rq_ref, rk_ref, rv_ref, mq_ref, gs_ref, gg_ref):
    x = x_ref[...]
    hb = _rms(x, g_ref[...]).astype(bf16)
    pos = pos_ref[...].astype(f32)
    lane = lax.broadcasted_iota(jnp.int32, (x.shape[0], LANES), 1)

    half = ROT_DIM // 2
    ang = pos * invn_ref[...]
    cn, sn = jnp.cos(ang), jnp.sin(ang)
    s1n = jnp.where((lane >= half) & (lane < ROT_DIM), sn, 0.0)
    s2n = jnp.where(lane < half, -sn, 0.0)
    halfr = RET_QK_DIM // 2
    angr = pos * invr_ref[...]
    cr, sr = jnp.cos(angr), jnp.sin(angr)
    lr = lane & (RET_QK_DIM - 1)
    s1r = jnp.where(lr >= halfr, sr, 0.0)
    s2r = jnp.where(lr < halfr, -sr, 0.0)

    col = 0
    for j in range(NSA_WIDTH // 512):
        acc = _dot(hb, w_ref[:, col:col + 512])
        for i, o in enumerate(_rotate_heads(acc, cn, s1n, s2n, half)):
            q_ref[:, j * 512 + i * LANES: j * 512 + (i + 1) * LANES] = o.astype(bf16)
        col += 512
    for br in range(3):
        acc = _dot(hb, w_ref[:, col:col + 512])
        for i, o in enumerate(_rotate_heads(acc[:, :256], cn, s1n, s2n, half)):
            kv_ref[:, br * 512 + i * LANES: br * 512 + (i + 1) * LANES] = o.astype(bf16)
        kv_ref[:, br * 512 + 256: br * 512 + 512] = acc[:, 256:].astype(bf16)
        col += 512
    acc = _dot(hb, w_ref[:, col:col + 512])
    rot = _rotate_heads(acc, cr, s1r, s2r, halfr)
    for i in range(2):
        rq_ref[:, i * LANES:(i + 1) * LANES] = rot[i].astype(bf16)
        rk_ref[:, i * LANES:(i + 1) * LANES] = (rot[2 + i] * (RET_QK_DIM ** -0.5)).astype(bf16)
    col += 512
    rv_ref[...] = _dot(hb, w_ref[:, col:col + 512]).astype(bf16)
    col += 512
    mq_ref[...] = _dot(hb, w_ref[:, col:col + 512]).astype(bf16)
    col += 512
    for j in range(MIX_WIDTH // 512):
        acc = _dot(hb, w_ref[:, col:col + 512])
        gs_ref[:, j * 512:(j + 1) * 512] = (acc * (1.0 / (1.0 + jnp.exp(-acc)))).astype(bf16)
        col += 512
    acc = _dot(hb, w_ref[:, col:col + 2 * GATE_PAD])
    gg_ref[...] = 1.0 / (1.0 + jnp.exp(-acc))


def _in_proj(x2, pos2, norm_pre, w_r, invn, invr):
    bt = x2.shape[0]
    tm = ROW_TILE
    win = w_r.shape[1]
    row = lambda i: (i, 0)
    const = lambda i: (0, 0)
    widths = (NSA_WIDTH, NSA_KV_ALL, RET_QK_WIDTH, RET_QK_WIDTH, RET_WIDTH, MEM_WIDTH, MIX_WIDTH)
    out_shape = [jax.ShapeDtypeStruct((bt, w), bf16) for w in widths]
    out_shape.append(jax.ShapeDtypeStruct((bt, 2 * GATE_PAD), f32))
    out_specs = [pl.BlockSpec((tm, w), row) for w in widths] + [pl.BlockSpec((tm, 2 * GATE_PAD), row)]
    return pl.pallas_call(
        _in_proj_kernel,
        grid=(bt // tm,),
        in_specs=[
            pl.BlockSpec((tm, D_MODEL), row),
            pl.BlockSpec((tm, 1), row),
            pl.BlockSpec((1, D_MODEL), const),
            pl.BlockSpec((D_MODEL, win), const),
            pl.BlockSpec((1, LANES), const),
            pl.BlockSpec((1, LANES), const),
        ],
        out_specs=out_specs,
        out_shape=out_shape,
        compiler_params=pltpu.CompilerParams(
            dimension_semantics=("arbitrary",), vmem_limit_bytes=VMEM_LIMIT),
        name="in_proj",
    )(x2, pos2, norm_pre, w_r, invn, invr)


def _mem_kv_kernel(m_ref, g_ref, w_ref, o_ref):
    hb = _rms(m_ref[...], g_ref[...]).astype(bf16)
    o_ref[...] = _dot(hb, w_ref[...]).astype(bf16)


def _mem_kv(mem2, mem_norm, w_kv):
    n = mem2.shape[0]
    tm = MEM_LEN
    return pl.pallas_call(
        _mem_kv_kernel,
        grid=(n // tm,),
        in_specs=[
            pl.BlockSpec((tm, D_MODEL), lambda i: (i, 0)),
            pl.BlockSpec((1, D_MODEL), lambda i: (0, 0)),
            pl.BlockSpec((D_MODEL, 2 * MEM_WIDTH), lambda i: (0, 0)),
        ],
        out_specs=pl.BlockSpec((tm, 2 * MEM_WIDTH), lambda i: (i, 0)),
        out_shape=jax.ShapeDtypeStruct((n, 2 * MEM_WIDTH), bf16),
        compiler_params=pltpu.CompilerParams(
            dimension_semantics=("arbitrary",), vmem_limit_bytes=VMEM_LIMIT),
        name="mem_kv",
    )(mem2, mem_norm, w_kv)


def _compress_kernel(kv_ref, pos_ref, w1_ref, w2_ref, o_ref, xf_ref):
    seq = kv_ref.shape[0]
    nslot = seq // CMP_STRIDE
    xf_ref[...] = kv_ref[...].astype(f32)
    pos = pos_ref[0]
    ha = jnp.zeros((nslot, CMP_HIDDEN), f32)
    hb = jnp.zeros((nslot, CMP_HIDDEN), f32)
    for p in range(CMP_STRIDE):
        a = xf_ref[pl.ds(p, nslot, stride=CMP_STRIDE), :]
        la = (a + pos[p:p + 1, :]).astype(bf16)
        lb = (a + pos[CMP_STRIDE + p:CMP_STRIDE + p + 1, :]).astype(bf16)
        ha = ha + _dot(la, w1_ref[0, p * NSA_HEAD_DIM:(p + 1) * NSA_HEAD_DIM, :])
        hb = hb + _dot(lb, w1_ref[0, (CMP_STRIDE + p) * NSA_HEAD_DIM:(CMP_STRIDE + p + 1) * NSA_HEAD_DIM, :])
    h = ha + pltpu.roll(hb, nslot - 1, 0)
    h = jax.nn.gelu(h)
    out = _dot(h.astype(bf16), w2_ref[0])
    rowi = lax.broadcasted_iota(jnp.int32, out.shape, 0)
    o_ref[0, 0] = jnp.where(rowi < nslot - 1, out, 0.0).astype(bf16)


def _compress(kv, cmp_pos, cmp_w1, cmp_w2, batch, seq):
    nslot = seq // CMP_STRIDE
    return pl.pallas_call(
        _compress_kernel,
        grid=(batch, 2 * NSA_KV_GROUPS),
        in_specs=[
            pl.BlockSpec((seq, NSA_HEAD_DIM), lambda b, j: (b, j)),
            pl.BlockSpec((1, CMP_BLOCK, NSA_HEAD_DIM), lambda b, j: (j // NSA_KV_GROUPS, 0, 0)),
            pl.BlockSpec((1, CMP_BLOCK * NSA_HEAD_DIM, CMP_HIDDEN), lambda b, j: (j // NSA_KV_GROUPS, 0, 0)),
            pl.BlockSpec((1, CMP_HIDDEN, NSA_HEAD_DIM), lambda b, j: (j // NSA_KV_GROUPS, 0, 0)),
        ],
        out_specs=pl.BlockSpec((1, 1, nslot, NSA_HEAD_DIM), lambda b, j: (b, j, 0, 0)),
        out_shape=jax.ShapeDtypeStruct((batch, 2 * NSA_KV_GROUPS, nslot, NSA_HEAD_DIM), bf16),
        scratch_shapes=[pltpu.VMEM((seq, NSA_HEAD_DIM), f32)],
        compiler_params=pltpu.CompilerParams(
            dimension_semantics=("arbitrary", "arbitrary"), vmem_limit_bytes=VMEM_LIMIT),
        name="compress",
    )(kv, cmp_pos, cmp_w1, cmp_w2)


def _nsa_kernel(q_ref, kc_ref, vc_ref, ks_ref, vs_ref, kw_ref, vw_ref, g_ref, ovl_ref, exp_ref, o_ref):
    qi = pl.program_id(2)
    t0 = qi * Q_BLOCK
    c2 = (NSA_HEAD_DIM ** -0.5) * math.log2(math.e)
    nslot = kc_ref.shape[2]
    n_cmp = nslot - 1
    heads = range(NSA_REP)
    hcols = lambda a, r: a[:, r * Q_BLOCK:(r + 1) * Q_BLOCK]

    q = q_ref[...]
    q4 = jnp.concatenate([q[:, r * NSA_HEAD_DIM:(r + 1) * NSA_HEAD_DIM] for r in heads], axis=0)
    tq = t0 + lax.broadcasted_iota(jnp.int32, (1, Q_BLOCK), 1)

    def softmax_cols(s_r):
        m = _col_reduce(jnp.maximum, s_r)
        e = jnp.exp2((s_r - m) * c2)
        return e, _col_reduce(jnp.add, e)

    s = _dot_nt(kc_ref[0, 0], q4)
    wlen = WINDOW + Q_BLOCK
    start = pl.multiple_of(jnp.maximum(t0 - WINDOW, 0), Q_BLOCK)
    sw = _dot_nt(kw_ref[pl.ds(start, wlen), :], q4)
    nrow = lax.broadcasted_iota(jnp.int32, (nslot, 1), 0)
    bias_c = jnp.where((nrow * CMP_STRIDE + (CMP_BLOCK - 1) <= tq) & (nrow < n_cmp), 0.0, NEG)
    row_ok = tq >= CMP_BLOCK - 1
    o_cmp, psum = [], None
    for r in heads:
        e, l = softmax_cols(hcols(s, r) + bias_c)
        p = e * jnp.where(row_ok, 1.0 / l, 0.0)
        o_cmp.append(_dot_tn(vc_ref[0, 0], p.astype(bf16)))
        psum = p if psum is None else psum + p

    imp_t = jnp.dot(ovl_ref[...], psum, precision=lax.Precision.HIGHEST,
                    preferred_element_type=f32)
    n_sb = imp_t.shape[0]
    jblk = lax.broadcasted_iota(jnp.int32, (n_sb, Q_BLOCK), 0)
    tl = t0 + lax.broadcasted_iota(jnp.int32, (n_sb, Q_BLOCK), 1)
    cur = tl >> 6
    forced = (jblk == 0) | (jblk == cur) | (jblk == cur - 1)
    valid = jblk * SEL_BLOCK <= tl
    v = jnp.where(forced, jnp.inf, jnp.where(valid, imp_t, -jnp.inf))
    cnt = jnp.zeros((n_sb, Q_BLOCK), f32)
    for i in range(n_sb):
        vi = v[i:i + 1, :]
        ahead = (vi > v) | ((vi == v) & (jblk > i))
        cnt = cnt + jnp.where(ahead, 1.0, 0.0)
    selbias_t = jnp.where(cnt < float(N_SELECT), 0.0, NEG)
    selbias = jnp.concatenate([selbias_t, jnp.zeros((LANES - n_sb, Q_BLOCK), f32)], axis=0).T
    q4_aug = jnp.concatenate([q4, jnp.concatenate([selbias.astype(bf16)] * NSA_REP, axis=0)], axis=1)

    vw = vw_ref[pl.ds(start, wlen), :]
    rel = tq - (start + lax.broadcasted_iota(jnp.int32, (wlen, 1), 0))
    bias_w = jnp.where((rel >= 0) & (rel < WINDOW), 0.0, NEG)
    gs = g_ref[...].T
    o_part = []
    for r in heads:
        e, l = softmax_cols(hcols(sw, r) + bias_w)
        o_win = _dot_tn(vw, e.astype(bf16)) * (1.0 / l)
        o_part.append(gs[r:r + 1, :] * o_cmp[r] + gs[2 * NSA_REP + r:2 * NSA_REP + r + 1, :] * o_win)

    krow = lax.broadcasted_iota(jnp.int32, (SLC_CHUNK, 1), 0)
    seq = ks_ref.shape[0]

    def slc_branch(n):
        use_sel = n * SLC_CHUNK > N_SELECT * SEL_BLOCK

        def slc_scores(c):
            ks = ks_ref[c * SLC_CHUNK:(c + 1) * SLC_CHUNK, :]
            if not use_sel:
                return _dot_nt(ks, q4)
            return _dot_nt(jnp.concatenate([ks, exp_ref[c * SLC_CHUNK:(c + 1) * SLC_CHUNK, :]], axis=1), q4_aug)

        state = [None] * NSA_REP
        sc = slc_scores(0)
        for c in range(n):
            sc_next = slc_scores(c + 1) if c + 1 < n else None
            vs = vs_ref[c * SLC_CHUNK:(c + 1) * SLC_CHUNK, :]
            bias = None
            if c == n - 1:
                bias = jnp.where(c * SLC_CHUNK + krow <= tq, 0.0, NEG)
            for r in heads:
                s_r = hcols(sc, r) if bias is None else hcols(sc, r) + bias
                m_c = _col_reduce(jnp.maximum, s_r)
                if state[r] is None:
                    e = jnp.exp2((s_r - m_c) * c2)
                    state[r] = (m_c, _col_reduce(jnp.add, e), _dot_tn(vs, e.astype(bf16)))
                else:
                    m_i, l_i, acc = state[r]
                    m_new = jnp.maximum(m_i, m_c)
                    alpha = jnp.exp2((m_i - m_new) * c2)
                    e = jnp.exp2((s_r - m_new) * c2)
                    state[r] = (m_new, alpha * l_i + _col_reduce(jnp.add, e),
                                alpha * acc + _dot_tn(vs, e.astype(bf16)))
            sc = sc_next
        for r in heads:
            o_slc = state[r][2] * (1.0 / state[r][1])
            o = o_part[r] + gs[NSA_REP + r:NSA_REP + r + 1, :] * o_slc
            o_ref[:, r * NSA_HEAD_DIM:(r + 1) * NSA_HEAD_DIM] = o.T.astype(bf16)

    n_chunks = (t0 + Q_BLOCK + SLC_CHUNK - 1) // SLC_CHUNK
    for n in range(1, seq // SLC_CHUNK + 1):
        pl.when(n_chunks == n)(functools.partial(slc_branch, n))


def _nsa(q, cmp, kv, gsig, ovl_t, expand, batch, seq):
    nq = seq // Q_BLOCK
    gw = NSA_REP * NSA_HEAD_DIM
    nslot = cmp.shape[2]
    G = NSA_KV_GROUPS
    kvspec = lambda off: pl.BlockSpec((seq, NSA_HEAD_DIM), lambda b, g, i, off=off: (b, off + g))
    return pl.pallas_call(
        _nsa_kernel,
        grid=(batch, G, nq),
        in_specs=[
            pl.BlockSpec((Q_BLOCK, gw), lambda b, g, i: (b * nq + i, g)),
            pl.BlockSpec((1, 1, nslot, NSA_HEAD_DIM), lambda b, g, i: (b, g, 0, 0)),
            pl.BlockSpec((1, 1, nslot, NSA_HEAD_DIM), lambda b, g, i: (b, G + g, 0, 0)),
            kvspec(2 * G), kvspec(3 * G), kvspec(4 * G), kvspec(5 * G),
            pl.BlockSpec((Q_BLOCK, GATE_PAD), lambda b, g, i: (b * nq + i, g)),
            pl.BlockSpec(ovl_t.shape, lambda b, g, i: (0, 0)),
            pl.BlockSpec(expand.shape, lambda b, g, i: (0, 0)),
        ],
        out_specs=pl.BlockSpec((Q_BLOCK, gw), lambda b, g, i: (b * nq + i, g)),
        out_shape=jax.ShapeDtypeStruct((batch * seq, NSA_WIDTH), bf16),
        compiler_params=pltpu.CompilerParams(
            dimension_semantics=("arbitrary", "arbitrary", "arbitrary"), vmem_limit_bytes=VMEM_LIMIT),
        name="nsa",
    )(q, cmp, cmp, kv, kv, kv, kv, gsig, ovl_t, expand)


def _retention_kernel(q_ref, k_ref, v_ref, gn_ref, o_ref):
    seq = q_ref.shape[0]
    C = RET_CHUNK
    n = lax.broadcasted_iota(jnp.int32, (C, C), 0)
    mcol = lax.broadcasted_iota(jnp.int32, (C, C), 1)
    diff = (n - mcol).astype(f32)
    nvec = lax.broadcasted_iota(jnp.int32, (C, 1), 0).astype(f32)
    decay, xi, zeta, gamma_c = [], [], [], []
    for h in range(RET_HEADS):
        lg = math.log1p(-(2.0 ** (-5.0 - h)))
        decay.append(jnp.where(diff >= 0, jnp.exp(lg * jnp.maximum(diff, 0.0)), 0.0))
        xi.append(jnp.exp(lg * (nvec + 1.0)))
        zeta.append(jnp.exp(lg * (C - 1.0 - nvec)))
        gamma_c.append(math.exp(lg * C))
    gn = gn_ref[...]

    def body(c, states):
        base = pl.multiple_of(c * C, C)
        new_states = []
        for h in range(RET_HEADS):
            qc = q_ref[pl.ds(base, C), h * RET_QK_DIM:(h + 1) * RET_QK_DIM]
            kc = k_ref[pl.ds(base, C), h * RET_QK_DIM:(h + 1) * RET_QK_DIM]
            vc = v_ref[pl.ds(base, C), h * RET_V_DIM:(h + 1) * RET_V_DIM]
            inner = _dot_nt(qc, kc) * decay[h]
            out = _dot(inner.astype(bf16), vc) + _dot(qc, states[h].astype(bf16)) * xi[h]
            kz = (kc.astype(f32) * zeta[h]).astype(bf16)
            new_states.append(states[h] * gamma_c[h] + _dot_tn(kz, vc))
            mu = jnp.mean(out, axis=-1, keepdims=True)
            d = out - mu
            var = jnp.mean(d * d, axis=-1, keepdims=True)
            o = d * lax.rsqrt(var + EPS) * gn[:, h * RET_V_DIM:(h + 1) * RET_V_DIM]
            o_ref[pl.ds(base, C), h * RET_V_DIM:(h + 1) * RET_V_DIM] = o.astype(bf16)
        return tuple(new_states)

    s0 = tuple(jnp.zeros((RET_QK_DIM, RET_V_DIM), f32) for _ in range(RET_HEADS))
    lax.fori_loop(0, seq // C, body, s0)


def _retention(rq, rk, rv, ret_gn, batch, seq):
    return pl.pallas_call(
        _retention_kernel,
        grid=(batch,),
        in_specs=[
            pl.BlockSpec((seq, RET_QK_WIDTH), lambda b: (b, 0)),
            pl.BlockSpec((seq, RET_QK_WIDTH), lambda b: (b, 0)),
            pl.BlockSpec((seq, RET_WIDTH), lambda b: (b, 0)),
            pl.BlockSpec((1, RET_WIDTH), lambda b: (0, 0)),
        ],
        out_specs=pl.BlockSpec((seq, RET_WIDTH), lambda b: (b, 0)),
        out_shape=jax.ShapeDtypeStruct((batch * seq, RET_WIDTH), bf16),
        compiler_params=pltpu.CompilerParams(
            dimension_semantics=("arbitrary",), vmem_limit_bytes=VMEM_LIMIT),
        name="retention",
    )(rq, rk, rv, ret_gn)


def _out_proj_kernel(x_ref, on_ref, or_ref, mq_ref, kvm_ref, gs_ref, w_ref, g_ref, o_ref):
    scale = MEM_HEAD_DIM ** -0.5
    y = _dot((on_ref[...].astype(f32) * gs_ref[:, 0:NSA_WIDTH].astype(f32)).astype(bf16),
             w_ref[0:NSA_WIDTH, :])
    y = y + _dot((or_ref[...].astype(f32)
                  * gs_ref[:, NSA_WIDTH:NSA_WIDTH + RET_WIDTH].astype(f32)).astype(bf16),
                 w_ref[NSA_WIDTH:NSA_WIDTH + RET_WIDTH, :])
    off = NSA_WIDTH + RET_WIDTH
    for h in range(MEM_HEADS):
        sl = slice(h * MEM_HEAD_DIM, (h + 1) * MEM_HEAD_DIM)
        s = _dot_nt(mq_ref[:, sl], kvm_ref[:, sl]) * scale
        m = jnp.max(s, axis=1, keepdims=True)
        p = jnp.exp(s - m)
        p = p / jnp.sum(p, axis=1, keepdims=True)
        om = _dot(p.astype(bf16), kvm_ref[:, MEM_WIDTH + h * MEM_HEAD_DIM:MEM_WIDTH + (h + 1) * MEM_HEAD_DIM])
        gate = gs_ref[:, off + h * MEM_HEAD_DIM:off + (h + 1) * MEM_HEAD_DIM].astype(f32)
        y = y + _dot((om * gate).astype(bf16),
                     w_ref[off + h * MEM_HEAD_DIM:off + (h + 1) * MEM_HEAD_DIM, :])
    o_ref[...] = x_ref[...] + _rms(y, g_ref[...])


def _out_proj(x2, o_nsa, o_ret, mq, kvm, gsilu, w_out, norm_post, seq):
    bt = x2.shape[0]
    tm = ROW_TILE
    per_b = seq // tm
    row = lambda i: (i, 0)
    const = lambda i: (0, 0)
    return pl.pallas_call(
        _out_proj_kernel,
        grid=(bt // tm,),
        in_specs=[
            pl.BlockSpec((tm, D_MODEL), row),
            pl.BlockSpec((tm, NSA_WIDTH), row),
            pl.BlockSpec((tm, RET_WIDTH), row),
            pl.BlockSpec((tm, MEM_WIDTH), row),
            pl.BlockSpec((MEM_LEN, 2 * MEM_WIDTH), lambda i: (i // per_b, 0)),
            pl.BlockSpec((tm, MIX_WIDTH), row),
            pl.BlockSpec((MIX_WIDTH, D_MODEL), const),
            pl.BlockSpec((1, D_MODEL), const),
        ],
        out_specs=pl.BlockSpec((tm, D_MODEL), row),
        out_shape=jax.ShapeDtypeStruct((bt, D_MODEL), f32),
        compiler_params=pltpu.CompilerParams(
            dimension_semantics=("arbitrary",), vmem_limit_bytes=VMEM_LIMIT),
        name="out_proj",
    )(x2, o_nsa, o_ret, mq, kvm, gsilu, w_out, norm_post)


def _relayout_w_in(w):
    o = 0
    q_nsa = w[:, o:o + NSA_WIDTH]; o += NSA_WIDTH
    kv = w[:, o:o + NSA_KV_ALL]; o += NSA_KV_ALL
    gates = w[:, o:o + 3 * NSA_HEADS]; o += 3 * NSA_HEADS
    rest = w[:, o:]
    gates = gates.reshape(D_MODEL, 3, NSA_KV_GROUPS, NSA_REP).transpose(0, 2, 1, 3)
    gates = gates.reshape(D_MODEL, NSA_KV_GROUPS, 3 * NSA_REP)
    gates = jnp.pad(gates, ((0, 0), (0, 0), (0, GATE_PAD - 3 * NSA_REP)))
    gates = gates.reshape(D_MODEL, NSA_KV_GROUPS * GATE_PAD)
    return jnp.concatenate([q_nsa, kv, rest, gates], axis=1).astype(bf16)


def _overlap_t(seq):
    n_slot = seq // CMP_STRIDE
    n_sb = seq // SEL_BLOCK
    cmp_start = np.arange(n_slot) * CMP_STRIDE
    sel_start = np.arange(n_sb) * SEL_BLOCK
    ov = np.clip(np.minimum(cmp_start[None, :] + CMP_BLOCK, sel_start[:, None] + SEL_BLOCK)
                 - np.maximum(cmp_start[None, :], sel_start[:, None]), 0, None)
    return jnp.asarray(ov.astype(np.float32) / CMP_BLOCK)


def kernel(x, mem, positions, norm_pre, w_in, cmp_pos_k, cmp_w1_k, cmp_w2_k, cmp_pos_v, cmp_w1_v,
           cmp_w2_v, ret_gn, mem_norm, w_mem_kv, w_out, norm_post):
    depth = norm_pre.shape[0]
    batch, seq, _ = x.shape
    assert x.shape[2] == D_MODEL and mem.shape[1:] == (MEM_LEN, D_MODEL)
    assert seq % SLC_CHUNK == 0 and seq >= WINDOW + Q_BLOCK

    half = ROT_DIM // 2
    inv_n = ROPE_THETA ** (-jnp.arange(half, dtype=f32) / half)
    invn = jnp.concatenate([inv_n, inv_n, jnp.zeros((LANES - ROT_DIM,), f32)])[None, :]
    halfr = RET_QK_DIM // 2
    inv_r = RET_THETA ** (-jnp.arange(halfr, dtype=f32) / halfr)
    invr = jnp.tile(inv_r, LANES // halfr)[None, :]
    ovl_t = _overlap_t(seq)
    expand = jnp.asarray(np.arange(seq)[:, None] // SEL_BLOCK == np.arange(LANES)[None, :], bf16)
    pos2 = positions.reshape(batch * seq, 1)
    mem2 = mem.reshape(batch * MEM_LEN, D_MODEL)

    x2 = x.reshape(batch * seq, D_MODEL)
    for layer in range(depth):
        w_r = _relayout_w_in(w_in[layer])
        q, kv, rq, rk, rv, mq, gsilu, gsig = _in_proj(x2, pos2, norm_pre[layer][None, :], w_r, invn, invr)
        kvm = _mem_kv(mem2, mem_norm[layer][None, :], w_mem_kv[layer].astype(bf16))
        cmp = _compress(
            kv,
            jnp.stack([cmp_pos_k[layer], cmp_pos_v[layer]]),
            jnp.stack([cmp_w1_k[layer], cmp_w1_v[layer]]).astype(bf16),
            jnp.stack([cmp_w2_k[layer], cmp_w2_v[layer]]).astype(bf16),
            batch, seq)
        o_nsa = _nsa(q, cmp, kv, gsig, ovl_t, expand, batch, seq)
        o_ret = _retention(rq, rk, rv, ret_gn[layer][None, :], batch, seq)
        x2 = _out_proj(x2, o_nsa, o_ret, mq, kvm, gsilu, w_out[layer].astype(bf16),
                       norm_post[layer][None, :], seq)
    return x2.reshape(batch, seq, D_MODEL)
```

```python
import functools
import math

import jax
import jax.numpy as jnp
import numpy as np
from jax import lax
from jax.experimental import pallas as pl
from jax.experimental.pallas import tpu as pltpu

D_MODEL = 1024
MEM_LEN = 256
EPS = 1e-6

NSA_HEADS = 8
NSA_KV_GROUPS = 2
NSA_REP = NSA_HEADS // NSA_KV_GROUPS
NSA_HEAD_DIM = 128
CMP_BLOCK = 32
CMP_STRIDE = 16
CMP_HIDDEN = 256
SEL_BLOCK = 64
N_SELECT = 16
WINDOW = 512
Q_BLOCK = 128
ROPE_THETA = 500000.0
ROT_DIM = NSA_HEAD_DIM // 4

RET_HEADS = 4
RET_QK_DIM = 64
RET_V_DIM = 128
RET_CHUNK = 128
RET_THETA = 10000.0

MEM_HEADS = 4
MEM_HEAD_DIM = 128

NSA_WIDTH = NSA_HEADS * NSA_HEAD_DIM
NSA_KV_ALL = 3 * 2 * NSA_KV_GROUPS * NSA_HEAD_DIM
RET_QK_WIDTH = RET_HEADS * RET_QK_DIM
RET_WIDTH = RET_HEADS * RET_V_DIM
MEM_WIDTH = MEM_HEADS * MEM_HEAD_DIM
MIX_WIDTH = NSA_WIDTH + RET_WIDTH + MEM_WIDTH
GATE_PAD = 128

LANES = 128
SUBLANES = 8
NEG = -1e30
VMEM_LIMIT = 56 * 1024 * 1024

ROW_TILE = 256
OUT_ROW_TILE = 512
SLC_CHUNK = 512

_NT = (((1,), (1,)), ((), ()))
_TN = (((0,), (0,)), ((), ()))

bf16 = jnp.bfloat16
f32 = jnp.float32


def _dot(a, b):
    return jnp.dot(a, b, preferred_element_type=f32)


def _dot_nt(a, b):
    return lax.dot_general(a, b, _NT, preferred_element_type=f32)


def _dot_tn(a, b):
    return lax.dot_general(a, b, _TN, preferred_element_type=f32)


def _col_reduce(op, x):
    slabs = [x[i:i + SUBLANES] for i in range(0, x.shape[0], SUBLANES)]
    while len(slabs) > 1:
        nxt = [op(slabs[i], slabs[i + 1]) for i in range(0, len(slabs) - 1, 2)]
        if len(slabs) % 2:
            nxt.append(slabs[-1])
        slabs = nxt
    red = jnp.max if op is jnp.maximum else jnp.sum
    return red(slabs[0], axis=0, keepdims=True)


def _rms(x, g):
    return x * lax.rsqrt(jnp.mean(x * x, axis=-1, keepdims=True) + EPS) * g


def _rotate_heads(acc, c, s1, s2, shift):
    outs = []
    for j in range(acc.shape[1] // LANES):
        a = acc[:, j * LANES:(j + 1) * LANES]
        outs.append(a * c + pltpu.roll(a, shift, 1) * s1 + pltpu.roll(a, LANES - shift, 1) * s2)
    return outs


def _in_proj_kernel(x_ref, pos_ref, g_ref, w_ref, invn_ref, invr_ref,
                    q_ref, kv_ref, rq_ref, rk_ref, rv_ref, mq_ref, gs_ref, gg_ref):
    x = x_ref[...]
    hb = _rms(x, g_ref[...]).astype(bf16)
    pos = pos_ref[...].astype(f32)
    lane = lax.broadcasted_iota(jnp.int32, (x.shape[0], LANES), 1)

    half = ROT_DIM // 2
    ang = pos * invn_ref[...]
    cn, sn = jnp.cos(ang), jnp.sin(ang)
    s1n = jnp.where((lane >= half) & (lane < ROT_DIM), sn, 0.0)
    s2n = jnp.where(lane < half, -sn, 0.0)
    halfr = RET_QK_DIM // 2
    angr = pos * invr_ref[...]
    cr, sr = jnp.cos(angr), jnp.sin(angr)
    lr = lane & (RET_QK_DIM - 1)
    s1r = jnp.where(lr >= halfr, sr, 0.0)
    s2r = jnp.where(lr < halfr, -sr, 0.0)

    col = 0
    for j in range(NSA_WIDTH // 512):
        acc = _dot(hb, w_ref[:, col:col + 512])
        for i, o in enumerate(_rotate_heads(acc, cn, s1n, s2n, half)):
            q_ref[:, j * 512 + i * LANES: j * 512 + (i + 1) * LANES] = o.astype(bf16)
        col += 512
    for br in range(3):
        acc = _dot(hb, w_ref[:, col:col + 512])
        for i, o in enumerate(_rotate_heads(acc[:, :256], cn, s1n, s2n, half)):
            kv_ref[:, br * 512 + i * LANES: br * 512 + (i + 1) * LANES] = o.astype(bf16)
        kv_ref[:, br * 512 + 256: br * 512 + 512] = acc[:, 256:].astype(bf16)
        col += 512
    acc = _dot(hb, w_ref[:, col:col + 512])
    rot = _rotate_heads(acc, cr, s1r, s2r, halfr)
    for i in range(2):
        rq_ref[:, i * LANES:(i + 1) * LANES] = rot[i].astype(bf16)
        rk_ref[:, i * LANES:(i + 1) * LANES] = (rot[2 + i] * (RET_QK_DIM ** -0.5)).astype(bf16)
    col += 512
    rv_ref[...] = _dot(hb, w_ref[:, col:col + 512]).astype(bf16)
    col += 512
    mq_ref[...] = _dot(hb, w_ref[:, col:col + 512]).astype(bf16)
    col += 512
    for j in range(MIX_WIDTH // 512):
        acc = _dot(hb, w_ref[:, col:col + 512])
        gs_ref[:, j * 512:(j + 1) * 512] = (acc * (1.0 / (1.0 + jnp.exp(-acc)))).astype(bf16)
        col += 512
    acc = _dot(hb, w_ref[:, col:col + 2 * GATE_PAD])
    gg_ref[...] = 1.0 / (1.0 + jnp.exp(-acc))


def _in_proj(x2, pos2, norm_pre, w_r, invn, invr):
    bt = x2.shape[0]
    tm = ROW_TILE
    win = w_r.shape[1]
    row = lambda i: (i, 0)
    const = lambda i: (0, 0)
    widths = (NSA_WIDTH, NSA_KV_ALL, RET_QK_WIDTH, RET_QK_WIDTH, RET_WIDTH, MEM_WIDTH, MIX_WIDTH)
    out_shape = [jax.ShapeDtypeStruct((bt, w), bf16) for w in widths]
    out_shape.append(jax.ShapeDtypeStruct((bt, 2 * GATE_PAD), f32))
    out_specs = [pl.BlockSpec((tm, w), row) for w in widths] + [pl.BlockSpec((tm, 2 * GATE_PAD), row)]
    return pl.pallas_call(
        _in_proj_kernel,
        grid=(bt // tm,),
        in_specs=[
            pl.BlockSpec((tm, D_MODEL), row),
            pl.BlockSpec((tm, 1), row),
            pl.BlockSpec((1, D_MODEL), const),
            pl.BlockSpec((D_MODEL, win), const),
            pl.BlockSpec((1, LANES), const),
            pl.BlockSpec((1, LANES), const),
        ],
        out_specs=out_specs,
        out_shape=out_shape,
        compiler_params=pltpu.CompilerParams(
            dimension_semantics=("arbitrary",), vmem_limit_bytes=VMEM_LIMIT),
        name="in_proj",
    )(x2, pos2, norm_pre, w_r, invn, invr)


def _mem_kv_kernel(m_ref, g_ref, w_ref, o_ref):
    hb = _rms(m_ref[...], g_ref[...]).astype(bf16)
    o_ref[...] = _dot(hb, w_ref[...]).astype(bf16)


def _mem_kv(mem2, mem_norm, w_kv):
    n = mem2.shape[0]
    tm = MEM_LEN
    return pl.pallas_call(
        _mem_kv_kernel,
        grid=(n // tm,),
        in_specs=[
            pl.BlockSpec((tm, D_MODEL), lambda i: (i, 0)),
            pl.BlockSpec((1, D_MODEL), lambda i: (0, 0)),
            pl.BlockSpec((D_MODEL, 2 * MEM_WIDTH), lambda i: (0, 0)),
        ],
        out_specs=pl.BlockSpec((tm, 2 * MEM_WIDTH), lambda i: (i, 0)),
        out_shape=jax.ShapeDtypeStruct((n, 2 * MEM_WIDTH), bf16),
        compiler_params=pltpu.CompilerParams(
            dimension_semantics=("arbitrary",), vmem_limit_bytes=VMEM_LIMIT),
        name="mem_kv",
    )(mem2, mem_norm, w_kv)


def _compress_kernel(kv_ref, pos_ref, w1_ref, w2_ref, o_ref, xf_ref):
    seq = kv_ref.shape[0]
    nslot = seq // CMP_STRIDE
    xf_ref[...] = kv_ref[...].astype(f32)
    pos = pos_ref[0]
    ha = jnp.zeros((nslot, CMP_HIDDEN), f32)
    hb = jnp.zeros((nslot, CMP_HIDDEN), f32)
    for p in range(CMP_STRIDE):
        a = xf_ref[pl.ds(p, nslot, stride=CMP_STRIDE), :]
        la = (a + pos[p:p + 1, :]).astype(bf16)
        lb = (a + pos[CMP_STRIDE + p:CMP_STRIDE + p + 1, :]).astype(bf16)
        ha = ha + _dot(la, w1_ref[0, p * NSA_HEAD_DIM:(p + 1) * NSA_HEAD_DIM, :])
        hb = hb + _dot(lb, w1_ref[0, (CMP_STRIDE + p) * NSA_HEAD_DIM:(CMP_STRIDE + p + 1) * NSA_HEAD_DIM, :])
    h = ha + pltpu.roll(hb, nslot - 1, 0)
    h = jax.nn.gelu(h)
    out = _dot(h.astype(bf16), w2_ref[0])
    rowi = lax.broadcasted_iota(jnp.int32, out.shape, 0)
    o_ref[0, 0] = jnp.where(rowi < nslot - 1, out, 0.0).astype(bf16)


def _compress(kv, cmp_pos, cmp_w1, cmp_w2, batch, seq):
    nslot = seq // CMP_STRIDE
    return pl.pallas_call(
        _compress_kernel,
        grid=(batch, 2 * NSA_KV_GROUPS),
        in_specs=[
            pl.BlockSpec((seq, NSA_HEAD_DIM), lambda b, j: (b, j)),
            pl.BlockSpec((1, CMP_BLOCK, NSA_HEAD_DIM), lambda b, j: (j // NSA_KV_GROUPS, 0, 0)),
            pl.BlockSpec((1, CMP_BLOCK * NSA_HEAD_DIM, CMP_HIDDEN), lambda b, j: (j // NSA_KV_GROUPS, 0, 0)),
            pl.BlockSpec((1, CMP_HIDDEN, NSA_HEAD_DIM), lambda b, j: (j // NSA_KV_GROUPS, 0, 0)),
        ],
        out_specs=pl.BlockSpec((1, 1, nslot, NSA_HEAD_DIM), lambda b, j: (b, j, 0, 0)),
        out_shape=jax.ShapeDtypeStruct((batch, 2 * NSA_KV_GROUPS, nslot, NSA_HEAD_DIM), bf16),
        scratch_shapes=[pltpu.VMEM((seq, NSA_HEAD_DIM), f32)],
        compiler_params=pltpu.CompilerParams(
            dimension_semantics=("arbitrary", "arbitrary"), vmem_limit_bytes=VMEM_LIMIT),
        name="compress",
    )(kv, cmp_pos, cmp_w1, cmp_w2)


def _nsa_kernel(q_ref, kc_ref, vc_ref, ks_ref, vs_ref, kw_ref, vw_ref, g_ref, ovl_ref, exp_ref, o_ref):
    qi = pl.program_id(2)
    t0 = qi * Q_BLOCK
    c2 = (NSA_HEAD_DIM ** -0.5) * math.log2(math.e)
    nslot = kc_ref.shape[2]
    n_cmp = nslot - 1
    heads = range(NSA_REP)
    hcols = lambda a, r: a[:, r * Q_BLOCK:(r + 1) * Q_BLOCK]

    q = q_ref[...]
    q4 = jnp.concatenate([q[:, r * NSA_HEAD_DIM:(r + 1) * NSA_HEAD_DIM] for r in heads], axis=0)
    tq = t0 + lax.broadcasted_iota(jnp.int32, (1, Q_BLOCK), 1)

    def softmax_cols(s_r):
        m = _col_reduce(jnp.maximum, s_r)
        e = jnp.exp2((s_r - m) * c2)
        return e, _col_reduce(jnp.add, e)

    s = _dot_nt(kc_ref[0, 0], q4)
    wlen = WINDOW + Q_BLOCK
    start = pl.multiple_of(jnp.maximum(t0 - WINDOW, 0), Q_BLOCK)
    sw = _dot_nt(kw_ref[pl.ds(start, wlen), :], q4)
    nrow = lax.broadcasted_iota(jnp.int32, (nslot, 1), 0)
    bias_c = jnp.where((nrow * CMP_STRIDE + (CMP_BLOCK - 1) <= tq) & (nrow < n_cmp), 0.0, NEG)
    row_ok = tq >= CMP_BLOCK - 1
    o_cmp, psum = [], None
    for r in heads:
        e, l = softmax_cols(hcols(s, r) + bias_c)
        p = e * jnp.where(row_ok, 1.0 / l, 0.0)
        o_cmp.append(_dot_tn(vc_ref[0, 0], p.astype(bf16)))
        psum = p if psum is None else psum + p

    imp_t = jnp.dot(ovl_ref[...], psum, precision=lax.Precision.HIGHEST,
                    preferred_element_type=f32)
    n_sb = imp_t.shape[0]
    jblk = lax.broadcasted_iota(jnp.int32, (n_sb, Q_BLOCK), 0)
    tl = t0 + lax.broadcasted_iota(jnp.int32, (n_sb, Q_BLOCK), 1)
    cur = tl >> 6
    forced = (jblk == 0) | (jblk == cur) | (jblk == cur - 1)
    valid = jblk * SEL_BLOCK <= tl
    v = jnp.where(forced, jnp.inf, jnp.where(valid, imp_t, -jnp.inf))
    cnt = jnp.zeros((n_sb, Q_BLOCK), f32)
    for i in range(n_sb):
        vi = v[i:i + 1, :]
        ahead = (vi > v) | ((vi == v) & (jblk > i))
        cnt = cnt + jnp.where(ahead, 1.0, 0.0)
    selbias_t = jnp.where(cnt < float(N_SELECT), 0.0, NEG)
    selbias = jnp.concatenate([selbias_t, jnp.zeros((LANES - n_sb, Q_BLOCK), f32)], axis=0).T
    q4_aug = jnp.concatenate([q4, jnp.concatenate([selbias.astype(bf16)] * NSA_REP, axis=0)], axis=1)

    vw = vw_ref[pl.ds(start, wlen), :]
    rel = tq - (start + lax.broadcasted_iota(jnp.int32, (wlen, 1), 0))
    bias_w = jnp.where((rel >= 0) & (rel < WINDOW), 0.0, NEG)
    gs = g_ref[...].T
    o_part = []
    for r in heads:
        e, l = softmax_cols(hcols(sw, r) + bias_w)
        o_win = _dot_tn(vw, e.astype(bf16)) * (1.0 / l)
        o_part.append(gs[r:r + 1, :] * o_cmp[r] + gs[2 * NSA_REP + r:2 * NSA_REP + r + 1, :] * o_win)

    krow = lax.broadcasted_iota(jnp.int32, (SLC_CHUNK, 1), 0)
    seq = ks_ref.shape[0]

    def slc_branch(n):
        use_sel = n * SLC_CHUNK > N_SELECT * SEL_BLOCK

        def slc_scores(c):
            ks = ks_ref[c * SLC_CHUNK:(c + 1) * SLC_CHUNK, :]
            if not use_sel:
                return _dot_nt(ks, q4)
            return _dot_nt(jnp.concatenate([ks, exp_ref[c * SLC_CHUNK:(c + 1) * SLC_CHUNK, :]], axis=1), q4_aug)

        state = [None] * NSA_REP
        sc = slc_scores(0)
        for c in range(n):
            sc_next = slc_scores(c + 1) if c + 1 < n else None
            vs = vs_ref[c * SLC_CHUNK:(c + 1) * SLC_CHUNK, :]
            bias = None
            if c == n - 1:
                bias = jnp.where(c * SLC_CHUNK + krow <= tq, 0.0, NEG)
            for r in heads:
                s_r = hcols(sc, r) if bias is None else hcols(sc, r) + bias
                m_c = _col_reduce(jnp.maximum, s_r)
                if state[r] is None:
                    e = jnp.exp2((s_r - m_c) * c2)
                    state[r] = (m_c, _col_reduce(jnp.add, e), _dot_tn(vs, e.astype(bf16)))
                else:
                    m_i, l_i, acc = state[r]
                    m_new = jnp.maximum(m_i, m_c)
                    alpha = jnp.exp2((m_i - m_new) * c2)
                    e = jnp.exp2((s_r - m_new) * c2)
                    state[r] = (m_new, alpha * l_i + _col_reduce(jnp.add, e),
                                alpha * acc + _dot_tn(vs, e.astype(bf16)))
            sc = sc_next
        for r in heads:
            o_slc = state[r][2] * (1.0 / state[r][1])
            o = o_part[r] + gs[NSA_REP + r:NSA_REP + r + 1, :] * o_slc
            o_ref[:, r * NSA_HEAD_DIM:(r + 1) * NSA_HEAD_DIM] = o.T.astype(bf16)

    n_chunks = (t0 + Q_BLOCK + SLC_CHUNK - 1) // SLC_CHUNK
    for n in range(1, seq // SLC_CHUNK + 1):
        pl.when(n_chunks == n)(functools.partial(slc_branch, n))


def _nsa(q, cmp, kv, gsig, ovl_t, expand, batch, seq):
    nq = seq // Q_BLOCK
    gw = NSA_REP * NSA_HEAD_DIM
    nslot = cmp.shape[2]
    G = NSA_KV_GROUPS
    kvspec = lambda off: pl.BlockSpec((seq, NSA_HEAD_DIM), lambda b, g, i, off=off: (b, off + g))
    return pl.pallas_call(
        _nsa_kernel,
        grid=(batch, G, nq),
        in_specs=[
            pl.BlockSpec((Q_BLOCK, gw), lambda b, g, i: (b * nq + i, g)),
            pl.BlockSpec((1, 1, nslot, NSA_HEAD_DIM), lambda b, g, i: (b, g, 0, 0)),
            pl.BlockSpec((1, 1, nslot, NSA_HEAD_DIM), lambda b, g, i: (b, G + g, 0, 0)),
            kvspec(2 * G), kvspec(3 * G), kvspec(4 * G), kvspec(5 * G),
            pl.BlockSpec((Q_BLOCK, GATE_PAD), lambda b, g, i: (b * nq + i, g)),
            pl.BlockSpec(ovl_t.shape, lambda b, g, i: (0, 0)),
            pl.BlockSpec(expand.shape, lambda b, g, i: (0, 0)),
        ],
        out_specs=pl.BlockSpec((Q_BLOCK, gw), lambda b, g, i: (b * nq + i, g)),
        out_shape=jax.ShapeDtypeStruct((batch * seq, NSA_WIDTH), bf16),
        compiler_params=pltpu.CompilerParams(
            dimension_semantics=("arbitrary", "arbitrary", "arbitrary"), vmem_limit_bytes=VMEM_LIMIT),
        name="nsa",
    )(q, cmp, cmp, kv, kv, kv, kv, gsig, ovl_t, expand)


def _retention_kernel(q_ref, k_ref, v_ref, gn_ref, o_ref):
    seq = q_ref.shape[0]
    C = RET_CHUNK
    n = lax.broadcasted_iota(jnp.int32, (C, C), 0)
    mcol = lax.broadcasted_iota(jnp.int32, (C, C), 1)
    diff = (n - mcol).astype(f32)
    nvec = lax.broadcasted_iota(jnp.int32, (C, 1), 0).astype(f32)
    decay, xi, zeta, gamma_c = [], [], [], []
    for h in range(RET_HEADS):
        lg = math.log1p(-(2.0 ** (-5.0 - h)))
        decay.append(jnp.where(diff >= 0, jnp.exp(lg * jnp.maximum(diff, 0.0)), 0.0))
        xi.append(jnp.exp(lg * (nvec + 1.0)))
        zeta.append(jnp.exp(lg * (C - 1.0 - nvec)))
        gamma_c.append(math.exp(lg * C))
    gn = gn_ref[...]

    def body(c, states):
        base = pl.multiple_of(c * C, C)
        hs = range(RET_HEADS)
        qc = [q_ref[pl.ds(base, C), h * RET_QK_DIM:(h + 1) * RET_QK_DIM] for h in hs]
        kc = [k_ref[pl.ds(base, C), h * RET_QK_DIM:(h + 1) * RET_QK_DIM] for h in hs]
        vc = [v_ref[pl.ds(base, C), h * RET_V_DIM:(h + 1) * RET_V_DIM] for h in hs]
        qk = [_dot_nt(qc[h], kc[h]) for h in hs]
        cross = [_dot(qc[h], states[h].astype(bf16)) for h in hs]
        kv = [_dot_tn((kc[h].astype(f32) * zeta[h]).astype(bf16), vc[h]) for h in hs]
        intra = [_dot((qk[h] * decay[h]).astype(bf16), vc[h]) for h in hs]
        for h in hs:
            out = intra[h] + cross[h] * xi[h]
            mu = jnp.mean(out, axis=-1, keepdims=True)
            d = out - mu
            var = jnp.mean(d * d, axis=-1, keepdims=True)
            o = d * lax.rsqrt(var + EPS) * gn[:, h * RET_V_DIM:(h + 1) * RET_V_DIM]
            o_ref[pl.ds(base, C), h * RET_V_DIM:(h + 1) * RET_V_DIM] = o.astype(bf16)
        return tuple(states[h] * gamma_c[h] + kv[h] for h in hs)

    s0 = tuple(jnp.zeros((RET_QK_DIM, RET_V_DIM), f32) for _ in range(RET_HEADS))
    lax.fori_loop(0, seq // C, body, s0, unroll=2)


def _retention(rq, rk, rv, ret_gn, batch, seq):
    return pl.pallas_call(
        _retention_kernel,
        grid=(batch,),
        in_specs=[
            pl.BlockSpec((seq, RET_QK_WIDTH), lambda b: (b, 0)),
            pl.BlockSpec((seq, RET_QK_WIDTH), lambda b: (b, 0)),
            pl.BlockSpec((seq, RET_WIDTH), lambda b: (b, 0)),
            pl.BlockSpec((1, RET_WIDTH), lambda b: (0, 0)),
        ],
        out_specs=pl.BlockSpec((seq, RET_WIDTH), lambda b: (b, 0)),
        out_shape=jax.ShapeDtypeStruct((batch * seq, RET_WIDTH), bf16),
        compiler_params=pltpu.CompilerParams(
            dimension_semantics=("arbitrary",), vmem_limit_bytes=VMEM_LIMIT),
        name="retention",
    )(rq, rk, rv, ret_gn)


def _out_proj_kernel(x_ref, on_ref, or_ref, mq_ref, kvm_ref, gs_ref, w_ref, g_ref, o_ref):
    c2 = (MEM_HEAD_DIM ** -0.5) * math.log2(math.e)
    off = NSA_WIDTH + RET_WIDTH
    heads = range(MEM_HEADS)
    hsl = lambda h: slice(h * MEM_HEAD_DIM, (h + 1) * MEM_HEAD_DIM)
    s = [_dot_nt(mq_ref[:, hsl(h)], kvm_ref[:, hsl(h)]) for h in heads]
    mix_a = jnp.concatenate([on_ref[...] * gs_ref[:, 0:NSA_WIDTH],
                             or_ref[...] * gs_ref[:, NSA_WIDTH:off]], axis=1)
    y = _dot(mix_a, w_ref[0:off, :])
    mix_m = []
    for h in heads:
        m = jnp.max(s[h], axis=1, keepdims=True)
        e = jnp.exp2((s[h] - m) * c2)
        l = jnp.sum(e, axis=1, keepdims=True)
        om = _dot(e.astype(bf16), kvm_ref[:, MEM_WIDTH + h * MEM_HEAD_DIM:MEM_WIDTH + (h + 1) * MEM_HEAD_DIM])
        mix_m.append((om * (1.0 / l)).astype(bf16) * gs_ref[:, off + h * MEM_HEAD_DIM:off + (h + 1) * MEM_HEAD_DIM])
    y = y + _dot(jnp.concatenate(mix_m, axis=1), w_ref[off:, :])
    o_ref[...] = x_ref[...] + _rms(y, g_ref[...])


def _out_proj(x2, o_nsa, o_ret, mq, kvm, gsilu, w_out, norm_post, seq):
    bt = x2.shape[0]
    tm = OUT_ROW_TILE
    per_b = seq // tm
    row = lambda i: (i, 0)
    const = lambda i: (0, 0)
    return pl.pallas_call(
        _out_proj_kernel,
        grid=(bt // tm,),
        in_specs=[
            pl.BlockSpec((tm, D_MODEL), row),
            pl.BlockSpec((tm, NSA_WIDTH), row),
            pl.BlockSpec((tm, RET_WIDTH), row),
            pl.BlockSpec((tm, MEM_WIDTH), row),
            pl.BlockSpec((MEM_LEN, 2 * MEM_WIDTH), lambda i: (i // per_b, 0)),
            pl.BlockSpec((tm, MIX_WIDTH), row),
            pl.BlockSpec((MIX_WIDTH, D_MODEL), const),
            pl.BlockSpec((1, D_MODEL), const),
        ],
        out_specs=pl.BlockSpec((tm, D_MODEL), row),
        out_shape=jax.ShapeDtypeStruct((bt, D_MODEL), f32),
        compiler_params=pltpu.CompilerParams(
            dimension_semantics=("arbitrary",), vmem_limit_bytes=VMEM_LIMIT),
        name="out_proj",
    )(x2, o_nsa, o_ret, mq, kvm, gsilu, w_out, norm_post)


def _relayout_w_in(w):
    o = 0
    q_nsa = w[:, o:o + NSA_WIDTH]; o += NSA_WIDTH
    kv = w[:, o:o + NSA_KV_ALL]; o += NSA_KV_ALL
    gates = w[:, o:o + 3 * NSA_HEADS]; o += 3 * NSA_HEADS
    rest = w[:, o:]
    gates = gates.reshape(D_MODEL, 3, NSA_KV_GROUPS, NSA_REP).transpose(0, 2, 1, 3)
    gates = gates.reshape(D_MODEL, NSA_KV_GROUPS, 3 * NSA_REP)
    gates = jnp.pad(gates, ((0, 0), (0, 0), (0, GATE_PAD - 3 * NSA_REP)))
    gates = gates.reshape(D_MODEL, NSA_KV_GROUPS * GATE_PAD)
    return jnp.concatenate([q_nsa, kv, rest, gates], axis=1).astype(bf16)


def _overlap_t(seq):
    n_slot = seq // CMP_STRIDE
    n_sb = seq // SEL_BLOCK
    cmp_start = np.arange(n_slot) * CMP_STRIDE
    sel_start = np.arange(n_sb) * SEL_BLOCK
    ov = np.clip(np.minimum(cmp_start[None, :] + CMP_BLOCK, sel_start[:, None] + SEL_BLOCK)
                 - np.maximum(cmp_start[None, :], sel_start[:, None]), 0, None)
    return jnp.asarray(ov.astype(np.float32) / CMP_BLOCK)


def kernel(x, mem, positions, norm_pre, w_in, cmp_pos_k, cmp_w1_k, cmp_w2_k, cmp_pos_v, cmp_w1_v,
           cmp_w2_v, ret_gn, mem_norm, w_mem_kv, w_out, norm_post):
    depth = norm_pre.shape[0]
    batch, seq, _ = x.shape
    assert x.shape[2] == D_MODEL and mem.shape[1:] == (MEM_LEN, D_MODEL)
    assert seq % SLC_CHUNK == 0 and seq >= WINDOW + Q_BLOCK

    half = ROT_DIM // 2
    inv_n = ROPE_THETA ** (-jnp.arange(half, dtype=f32) / half)
    invn = jnp.concatenate([inv_n, inv_n, jnp.zeros((LANES - ROT_DIM,), f32)])[None, :]
    halfr = RET_QK_DIM // 2
    inv_r = RET_THETA ** (-jnp.arange(halfr, dtype=f32) / halfr)
    invr = jnp.tile(inv_r, LANES // halfr)[None, :]
    ovl_t = _overlap_t(seq)
    expand = jnp.asarray(np.arange(seq)[:, None] // SEL_BLOCK == np.arange(LANES)[None, :], bf16)
    pos2 = positions.reshape(batch * seq, 1)
    mem2 = mem.reshape(batch * MEM_LEN, D_MODEL)

    x2 = x.reshape(batch * seq, D_MODEL)
    for layer in range(depth):
        w_r = _relayout_w_in(w_in[layer])
        q, kv, rq, rk, rv, mq, gsilu, gsig = _in_proj(x2, pos2, norm_pre[layer][None, :], w_r, invn, invr)
        kvm = _mem_kv(mem2, mem_norm[layer][None, :], w_mem_kv[layer].astype(bf16))
        cmp = _compress(
            kv,
            jnp.stack([cmp_pos_k[layer], cmp_pos_v[layer]]),
            jnp.stack([cmp_w1_k[layer], cmp_w1_v[layer]]).astype(bf16),
            jnp.stack([cmp_w2_k[layer], cmp_w2_v[layer]]).astype(bf16),
            batch, seq)
        o_nsa = _nsa(q, cmp, kv, gsig, ovl_t, expand, batch, seq)
        o_ret = _retention(rq, rk, rv, ret_gn[layer][None, :], batch, seq)
        x2 = _out_proj(x2, o_nsa, o_ret, mq, kvm, gsilu, w_out[layer].astype(bf16),
                       norm_post[layer][None, :], seq)
    return x2.reshape(batch, seq, D_MODEL)
```

```python
import functools
import math

import jax
import jax.numpy as jnp
import numpy as np
from jax import lax
from jax.experimental import pallas as pl
from jax.experimental.pallas import tpu as pltpu

D_MODEL = 1024
MEM_LEN = 256
EPS = 1e-6

NSA_HEADS = 8
NSA_KV_GROUPS = 2
NSA_REP = NSA_HEADS // NSA_KV_GROUPS
NSA_HEAD_DIM = 128
CMP_BLOCK = 32
CMP_STRIDE = 16
CMP_HIDDEN = 256
SEL_BLOCK = 64
N_SELECT = 16
WINDOW = 512
Q_BLOCK = 128
ROPE_THETA = 500000.0
ROT_DIM = NSA_HEAD_DIM // 4

RET_HEADS = 4
RET_QK_DIM = 64
RET_V_DIM = 128
RET_CHUNK = 128
RET_THETA = 10000.0

MEM_HEADS = 4
MEM_HEAD_DIM = 128

NSA_WIDTH = NSA_HEADS * NSA_HEAD_DIM
NSA_KV_ALL = 3 * 2 * NSA_KV_GROUPS * NSA_HEAD_DIM
RET_QK_WIDTH = RET_HEADS * RET_QK_DIM
RET_WIDTH = RET_HEADS * RET_V_DIM
MEM_WIDTH = MEM_HEADS * MEM_HEAD_DIM
MIX_WIDTH = NSA_WIDTH + RET_WIDTH + MEM_WIDTH
GATE_PAD = 128

LANES = 128
SUBLANES = 8
NEG = -1e30
LOG2E = math.log2(math.e)
NSA_QSCALE = NSA_HEAD_DIM ** -0.5 * LOG2E
MEM_QSCALE = MEM_HEAD_DIM ** -0.5 * LOG2E
VMEM_LIMIT = 56 * 1024 * 1024

ROW_TILE = 256
OUT_ROW_TILE = 512
SLC_CHUNK = 512

_NT = (((1,), (1,)), ((), ()))
_TN = (((0,), (0,)), ((), ()))

bf16 = jnp.bfloat16
f32 = jnp.float32


def _dot(a, b):
    return jnp.dot(a, b, preferred_element_type=f32)


def _dot_nt(a, b):
    return lax.dot_general(a, b, _NT, preferred_element_type=f32)


def _dot_tn(a, b):
    return lax.dot_general(a, b, _TN, preferred_element_type=f32)


def _col_reduce(op, x):
    slabs = [x[i:i + SUBLANES] for i in range(0, x.shape[0], SUBLANES)]
    while len(slabs) > 1:
        nxt = [op(slabs[i], slabs[i + 1]) for i in range(0, len(slabs) - 1, 2)]
        if len(slabs) % 2:
            nxt.append(slabs[-1])
        slabs = nxt
    red = jnp.max if op is jnp.maximum else jnp.sum
    return red(slabs[0], axis=0, keepdims=True)


def _rms(x, g):
    return x * lax.rsqrt(jnp.mean(x * x, axis=-1, keepdims=True) + EPS) * g


def _rotate_heads(acc, c, s1, s2, shift):
    outs = []
    for j in range(acc.shape[1] // LANES):
        a = acc[:, j * LANES:(j + 1) * LANES]
        outs.append(a * c + pltpu.roll(a, shift, 1) * s1 + pltpu.roll(a, LANES - shift, 1) * s2)
    return outs


def _in_proj_kernel(x_ref, pos_ref, g_ref, w_ref, invn_ref, invr_ref,
                    q_ref, kv_ref, rq_ref, rk_ref, rv_ref, mq_ref, gs_ref, gg_ref):
    x = x_ref[...]
    hb = _rms(x, g_ref[...]).astype(bf16)
    pos = pos_ref[...].astype(f32)
    lane = lax.broadcasted_iota(jnp.int32, (x.shape[0], LANES), 1)

    half = ROT_DIM // 2
    ang = pos * invn_ref[...]
    cn, sn = jnp.cos(ang), jnp.sin(ang)
    s1n = jnp.where((lane >= half) & (lane < ROT_DIM), sn, 0.0)
    s2n = jnp.where(lane < half, -sn, 0.0)
    halfr = RET_QK_DIM // 2
    angr = pos * invr_ref[...]
    cr, sr = jnp.cos(angr), jnp.sin(angr)
    lr = lane & (RET_QK_DIM - 1)
    s1r = jnp.where(lr >= halfr, sr, 0.0)
    s2r = jnp.where(lr < halfr, -sr, 0.0)

    col = 0
    for j in range(NSA_WIDTH // 512):
        acc = _dot(hb, w_ref[:, col:col + 512])
        for i, o in enumerate(_rotate_heads(acc, cn * NSA_QSCALE, s1n * NSA_QSCALE, s2n * NSA_QSCALE, half)):
            q_ref[:, j * 512 + i * LANES: j * 512 + (i + 1) * LANES] = o.astype(bf16)
        col += 512
    for br in range(3):
        acc = _dot(hb, w_ref[:, col:col + 512])
        for i, o in enumerate(_rotate_heads(acc[:, :256], cn, s1n, s2n, half)):
            kv_ref[:, br * 512 + i * LANES: br * 512 + (i + 1) * LANES] = o.astype(bf16)
        kv_ref[:, br * 512 + 256: br * 512 + 512] = acc[:, 256:].astype(bf16)
        col += 512
    acc = _dot(hb, w_ref[:, col:col + 512])
    rot = _rotate_heads(acc, cr, s1r, s2r, halfr)
    for i in range(2):
        rq_ref[:, i * LANES:(i + 1) * LANES] = rot[i].astype(bf16)
        rk_ref[:, i * LANES:(i + 1) * LANES] = (rot[2 + i] * (RET_QK_DIM ** -0.5)).astype(bf16)
    col += 512
    rv_ref[...] = _dot(hb, w_ref[:, col:col + 512]).astype(bf16)
    col += 512
    mq_ref[...] = (_dot(hb, w_ref[:, col:col + 512]) * MEM_QSCALE).astype(bf16)
    col += 512
    for j in range(MIX_WIDTH // 512):
        acc = _dot(hb, w_ref[:, col:col + 512])
        gs_ref[:, j * 512:(j + 1) * 512] = (acc * (1.0 / (1.0 + jnp.exp(-acc)))).astype(bf16)
        col += 512
    acc = _dot(hb, w_ref[:, col:col + 2 * GATE_PAD])
    gg_ref[...] = 1.0 / (1.0 + jnp.exp(-acc))


def _in_proj(x2, pos2, norm_pre, w_r, invn, invr):
    bt = x2.shape[0]
    tm = ROW_TILE
    win = w_r.shape[1]
    row = lambda i: (i, 0)
    const = lambda i: (0, 0)
    widths = (NSA_WIDTH, NSA_KV_ALL, RET_QK_WIDTH, RET_QK_WIDTH, RET_WIDTH, MEM_WIDTH, MIX_WIDTH)
    out_shape = [jax.ShapeDtypeStruct((bt, w), bf16) for w in widths]
    out_shape.append(jax.ShapeDtypeStruct((bt, 2 * GATE_PAD), f32))
    out_specs = [pl.BlockSpec((tm, w), row) for w in widths] + [pl.BlockSpec((tm, 2 * GATE_PAD), row)]
    return pl.pallas_call(
        _in_proj_kernel,
        grid=(bt // tm,),
        in_specs=[
            pl.BlockSpec((tm, D_MODEL), row),
            pl.BlockSpec((tm, 1), row),
            pl.BlockSpec((1, D_MODEL), const),
            pl.BlockSpec((D_MODEL, win), const),
            pl.BlockSpec((1, LANES), const),
            pl.BlockSpec((1, LANES), const),
        ],
        out_specs=out_specs,
        out_shape=out_shape,
        compiler_params=pltpu.CompilerParams(
            dimension_semantics=("arbitrary",), vmem_limit_bytes=VMEM_LIMIT),
        name="in_proj",
    )(x2, pos2, norm_pre, w_r, invn, invr)


def _mem_kv_kernel(m_ref, g_ref, w_ref, o_ref):
    hb = _rms(m_ref[...], g_ref[...]).astype(bf16)
    o_ref[...] = _dot(hb, w_ref[...]).astype(bf16)


def _mem_kv(mem2, mem_norm, w_kv):
    n = mem2.shape[0]
    tm = MEM_LEN
    return pl.pallas_call(
        _mem_kv_kernel,
        grid=(n // tm,),
        in_specs=[
            pl.BlockSpec((tm, D_MODEL), lambda i: (i, 0)),
            pl.BlockSpec((1, D_MODEL), lambda i: (0, 0)),
            pl.BlockSpec((D_MODEL, 2 * MEM_WIDTH), lambda i: (0, 0)),
        ],
        out_specs=pl.BlockSpec((tm, 2 * MEM_WIDTH), lambda i: (i, 0)),
        out_shape=jax.ShapeDtypeStruct((n, 2 * MEM_WIDTH), bf16),
        compiler_params=pltpu.CompilerParams(
            dimension_semantics=("arbitrary",), vmem_limit_bytes=VMEM_LIMIT),
        name="mem_kv",
    )(mem2, mem_norm, w_kv)


def _compress_kernel(kv_ref, pos_ref, w1_ref, w2_ref, o_ref, xf_ref):
    seq = kv_ref.shape[0]
    nslot = seq // CMP_STRIDE
    xf_ref[...] = kv_ref[...].astype(f32)
    pos = pos_ref[0]
    ha = jnp.zeros((nslot, CMP_HIDDEN), f32)
    hb = jnp.zeros((nslot, CMP_HIDDEN), f32)
    for p in range(CMP_STRIDE):
        a = xf_ref[pl.ds(p, nslot, stride=CMP_STRIDE), :]
        la = (a + pos[p:p + 1, :]).astype(bf16)
        lb = (a + pos[CMP_STRIDE + p:CMP_STRIDE + p + 1, :]).astype(bf16)
        ha = ha + _dot(la, w1_ref[0, p * NSA_HEAD_DIM:(p + 1) * NSA_HEAD_DIM, :])
        hb = hb + _dot(lb, w1_ref[0, (CMP_STRIDE + p) * NSA_HEAD_DIM:(CMP_STRIDE + p + 1) * NSA_HEAD_DIM, :])
    h = ha + pltpu.roll(hb, nslot - 1, 0)
    h = jax.nn.gelu(h)
    out = _dot(h.astype(bf16), w2_ref[0])
    rowi = lax.broadcasted_iota(jnp.int32, out.shape, 0)
    o_ref[0, 0] = jnp.where(rowi < nslot - 1, out, 0.0).astype(bf16)


def _compress(kv, cmp_pos, cmp_w1, cmp_w2, batch, seq):
    nslot = seq // CMP_STRIDE
    return pl.pallas_call(
        _compress_kernel,
        grid=(batch, 2 * NSA_KV_GROUPS),
        in_specs=[
            pl.BlockSpec((seq, NSA_HEAD_DIM), lambda b, j: (b, j)),
            pl.BlockSpec((1, CMP_BLOCK, NSA_HEAD_DIM), lambda b, j: (j // NSA_KV_GROUPS, 0, 0)),
            pl.BlockSpec((1, CMP_BLOCK * NSA_HEAD_DIM, CMP_HIDDEN), lambda b, j: (j // NSA_KV_GROUPS, 0, 0)),
            pl.BlockSpec((1, CMP_HIDDEN, NSA_HEAD_DIM), lambda b, j: (j // NSA_KV_GROUPS, 0, 0)),
        ],
        out_specs=pl.BlockSpec((1, 1, nslot, NSA_HEAD_DIM), lambda b, j: (b, j, 0, 0)),
        out_shape=jax.ShapeDtypeStruct((batch, 2 * NSA_KV_GROUPS, nslot, NSA_HEAD_DIM), bf16),
        scratch_shapes=[pltpu.VMEM((seq, NSA_HEAD_DIM), f32)],
        compiler_params=pltpu.CompilerParams(
            dimension_semantics=("arbitrary", "arbitrary"), vmem_limit_bytes=VMEM_LIMIT),
        name="compress",
    )(kv, cmp_pos, cmp_w1, cmp_w2)


def _nsa_tile(n_static, q_ref, kc_ref, vc_ref, ks_ref, vs_ref, kw_ref, vw_ref, g_ref, ovl_ref, exp_ref,
              o_ref):
    qi = pl.program_id(2)
    t0 = qi * Q_BLOCK
    nslot = kc_ref.shape[2]
    n_cmp = nslot - 1
    heads = range(NSA_REP)
    hcols = lambda a, r: a[:, r * Q_BLOCK:(r + 1) * Q_BLOCK]

    q = q_ref[...]
    q4 = jnp.concatenate([q[:, r * NSA_HEAD_DIM:(r + 1) * NSA_HEAD_DIM] for r in heads], axis=0)
    tq = t0 + lax.broadcasted_iota(jnp.int32, (1, Q_BLOCK), 1)

    def softmax_cols(s_r):
        m = _col_reduce(jnp.maximum, s_r)
        e = jnp.exp2(s_r - m)
        return e, _col_reduce(jnp.add, e)

    s = _dot_nt(kc_ref[0, 0], q4)
    wlen = WINDOW + Q_BLOCK
    start = pl.multiple_of(jnp.maximum(t0 - WINDOW, 0), Q_BLOCK)
    sw = _dot_nt(kw_ref[pl.ds(start, wlen), :], q4)
    sc0 = _dot_nt(ks_ref[0:SLC_CHUNK, :], q4)
    nrow = lax.broadcasted_iota(jnp.int32, (nslot, 1), 0)
    bias_c = jnp.where((nrow * CMP_STRIDE + (CMP_BLOCK - 1) <= tq) & (nrow < n_cmp), 0.0, NEG)
    row_ok = tq >= CMP_BLOCK - 1
    ps, psum = [], None
    for r in heads:
        e, l = softmax_cols(hcols(s, r) + bias_c)
        p = e * jnp.where(row_ok, 1.0 / l, 0.0)
        ps.append(p.astype(bf16))
        psum = p if psum is None else psum + p
    o_cmp = _dot_tn(vc_ref[0, 0], jnp.concatenate(ps, axis=1))

    imp_t = jnp.dot(ovl_ref[...], psum, precision=lax.Precision.HIGHEST,
                    preferred_element_type=f32)
    n_sb = imp_t.shape[0]
    jblk = lax.broadcasted_iota(jnp.int32, (n_sb, Q_BLOCK), 0)
    tl = t0 + lax.broadcasted_iota(jnp.int32, (n_sb, Q_BLOCK), 1)
    cur = tl >> 6
    forced = (jblk == 0) | (jblk == cur) | (jblk == cur - 1)
    valid = jblk * SEL_BLOCK <= tl
    v = jnp.where(forced, jnp.inf, jnp.where(valid, imp_t, -jnp.inf))
    n_live = min(n_sb, n_static * SLC_CHUNK // SEL_BLOCK)
    slabs = [v[k:k + SUBLANES] for k in range(0, n_live, SUBLANES)]
    jl = lax.broadcasted_iota(jnp.int32, (SUBLANES, Q_BLOCK), 0)
    cnts = [jnp.zeros((SUBLANES, Q_BLOCK), f32) for _ in slabs]
    for i in range(n_live):
        vi = jnp.broadcast_to(v[i:i + 1, :], (SUBLANES, Q_BLOCK))
        for k, vk in enumerate(slabs):
            if k * SUBLANES > i:
                ahead = vi >= vk
            elif (k + 1) * SUBLANES <= i:
                ahead = vi > vk
            else:
                ahead = (vi > vk) | ((vi == vk) & (jl > i - k * SUBLANES))
            cnts[k] = cnts[k] + jnp.where(ahead, 1.0, 0.0)
    selbias_pad = jnp.concatenate([jnp.where(c < float(N_SELECT), 0.0, NEG) for c in cnts]
                                  + [jnp.zeros((LANES - n_live, Q_BLOCK), f32)], axis=0)
    q4_aug = jnp.concatenate([q4, jnp.concatenate([selbias_pad.T.astype(bf16)] * NSA_REP, axis=0)], axis=1)

    def gate_rows(br):
        return jnp.concatenate([gs[br * NSA_REP + r:br * NSA_REP + r + 1, :] for r in heads], axis=1)

    if n_static * SLC_CHUNK > WINDOW:
        kl = lax.broadcasted_iota(jnp.int32, (Q_BLOCK, Q_BLOCK), 0)
        ql = lax.broadcasted_iota(jnp.int32, (Q_BLOCK, Q_BLOCK), 1)
        head_bias = jnp.where(kl > ql, 0.0, NEG)
        tail_bias = jnp.where(kl <= ql, 0.0, NEG)
        add_bias_w = lambda a: jnp.concatenate(
            [a[:Q_BLOCK] + head_bias, a[Q_BLOCK:WINDOW], a[WINDOW:] + tail_bias], axis=0)
    else:
        rel = tq - (start + lax.broadcasted_iota(jnp.int32, (wlen, 1), 0))
        bias_w = jnp.where((rel >= 0) & (rel < WINDOW), 0.0, NEG)
        add_bias_w = lambda a: a + bias_w
    gs = g_ref[...].T
    es, ls = [], []
    for r in heads:
        e, l = softmax_cols(add_bias_w(hcols(sw, r)))
        es.append(e.astype(bf16))
        ls.append(l)
    o_win = _dot_tn(vw_ref[pl.ds(start, wlen), :], jnp.concatenate(es, axis=1))
    o_part = (gate_rows(0) * o_cmp
              + (gate_rows(2) * (1.0 / jnp.concatenate(ls, axis=1))) * o_win)

    krow = lax.broadcasted_iota(jnp.int32, (SLC_CHUNK, 1), 0)
    seq = ks_ref.shape[0]

    def slc_branch(n):
        use_sel = n * SLC_CHUNK > N_SELECT * SEL_BLOCK

        def slc_scores(c):
            ks = ks_ref[c * SLC_CHUNK:(c + 1) * SLC_CHUNK, :]
            if not use_sel:
                return _dot_nt(ks, q4)
            return _dot_nt(jnp.concatenate([ks, exp_ref[c * SLC_CHUNK:(c + 1) * SLC_CHUNK, :]], axis=1), q4_aug)

        m_i, l_i, acc = [None] * NSA_REP, [None] * NSA_REP, None
        sc = sc0
        for c in range(n):
            bias = None
            if use_sel and c == 0:
                bias = _dot(exp_ref[0:SLC_CHUNK, :], selbias_pad.astype(bf16))
            sc_next = slc_scores(c + 1) if c + 1 < n else None
            if c == n - 1:
                causal = jnp.where(c * SLC_CHUNK + krow <= tq, 0.0, NEG)
                bias = causal if bias is None else bias + causal
            es, alphas = [], []
            for r in heads:
                s_r = hcols(sc, r) if bias is None else hcols(sc, r) + bias
                m_c = _col_reduce(jnp.maximum, s_r)
                if c == 0:
                    e = jnp.exp2(s_r - m_c)
                    m_i[r], l_i[r] = m_c, _col_reduce(jnp.add, e)
                else:
                    m_new = jnp.maximum(m_i[r], m_c)
                    alpha = jnp.exp2(m_i[r] - m_new)
                    e = jnp.exp2(s_r - m_new)
                    m_i[r], l_i[r] = m_new, alpha * l_i[r] + _col_reduce(jnp.add, e)
                    alphas.append(alpha)
                es.append(e.astype(bf16))
            pv = _dot_tn(vs_ref[c * SLC_CHUNK:(c + 1) * SLC_CHUNK, :], jnp.concatenate(es, axis=1))
            acc = pv if c == 0 else jnp.concatenate(alphas, axis=1) * acc + pv
            sc = sc_next
        o = o_part + (gate_rows(1) * (1.0 / jnp.concatenate(l_i, axis=1))) * acc
        for r in heads:
            o_ref[:, r * NSA_HEAD_DIM:(r + 1) * NSA_HEAD_DIM] = hcols(o, r).T.astype(bf16)

    slc_branch(n_static)


def _nsa_kernel(*refs):
    seq = refs[3].shape[0]
    n_chunks = (pl.program_id(2) * Q_BLOCK + Q_BLOCK + SLC_CHUNK - 1) // SLC_CHUNK
    for n in range(1, seq // SLC_CHUNK + 1):
        pl.when(n_chunks == n)(functools.partial(_nsa_tile, n, *refs))


def _nsa(q, cmp, kv, gsig, ovl_t, expand, batch, seq):
    nq = seq // Q_BLOCK
    gw = NSA_REP * NSA_HEAD_DIM
    nslot = cmp.shape[2]
    G = NSA_KV_GROUPS
    kvspec = lambda off: pl.BlockSpec((seq, NSA_HEAD_DIM), lambda b, g, i, off=off: (b, off + g))
    return pl.pallas_call(
        _nsa_kernel,
        grid=(batch, G, nq),
        in_specs=[
            pl.BlockSpec((Q_BLOCK, gw), lambda b, g, i: (b * nq + i, g)),
            pl.BlockSpec((1, 1, nslot, NSA_HEAD_DIM), lambda b, g, i: (b, g, 0, 0)),
            pl.BlockSpec((1, 1, nslot, NSA_HEAD_DIM), lambda b, g, i: (b, G + g, 0, 0)),
            kvspec(2 * G), kvspec(3 * G), kvspec(4 * G), kvspec(5 * G),
            pl.BlockSpec((Q_BLOCK, GATE_PAD), lambda b, g, i: (b * nq + i, g)),
            pl.BlockSpec(ovl_t.shape, lambda b, g, i: (0, 0)),
            pl.BlockSpec(expand.shape, lambda b, g, i: (0, 0)),
        ],
        out_specs=pl.BlockSpec((Q_BLOCK, gw), lambda b, g, i: (b * nq + i, g)),
        out_shape=jax.ShapeDtypeStruct((batch * seq, NSA_WIDTH), bf16),
        compiler_params=pltpu.CompilerParams(
            dimension_semantics=("arbitrary", "arbitrary", "arbitrary"), vmem_limit_bytes=VMEM_LIMIT),
        name="nsa",
    )(q, cmp, cmp, kv, kv, kv, kv, gsig, ovl_t, expand)


def _retention_kernel(q_ref, k_ref, v_ref, gn_ref, o_ref):
    seq = q_ref.shape[0]
    C = RET_CHUNK
    n = lax.broadcasted_iota(jnp.int32, (C, C), 0)
    mcol = lax.broadcasted_iota(jnp.int32, (C, C), 1)
    diff = (n - mcol).astype(f32)
    nvec = lax.broadcasted_iota(jnp.int32, (C, 1), 0).astype(f32)
    decay, xi, zeta, gamma_c = [], [], [], []
    for h in range(RET_HEADS):
        lg = math.log1p(-(2.0 ** (-5.0 - h)))
        decay.append(jnp.where(diff >= 0, jnp.exp(lg * jnp.maximum(diff, 0.0)), 0.0))
        xi.append(jnp.exp(lg * (nvec + 1.0)))
        zeta.append(jnp.exp(lg * (C - 1.0 - nvec)))
        gamma_c.append(math.exp(lg * C))
    gn = gn_ref[...]

    def body(c, states):
        base = pl.multiple_of(c * C, C)
        hs = range(RET_HEADS)
        qc = [q_ref[pl.ds(base, C), h * RET_QK_DIM:(h + 1) * RET_QK_DIM] for h in hs]
        kc = [k_ref[pl.ds(base, C), h * RET_QK_DIM:(h + 1) * RET_QK_DIM] for h in hs]
        vc = [v_ref[pl.ds(base, C), h * RET_V_DIM:(h + 1) * RET_V_DIM] for h in hs]
        qk = [_dot_nt(qc[h], kc[h]) for h in hs]
        cross = [_dot(qc[h], states[h].astype(bf16)) for h in hs]
        kv = [_dot_tn((kc[h].astype(f32) * zeta[h]).astype(bf16), vc[h]) for h in hs]
        intra = [_dot((qk[h] * decay[h]).astype(bf16), vc[h]) for h in hs]
        for h in hs:
            out = intra[h] + cross[h] * xi[h]
            mu = jnp.mean(out, axis=-1, keepdims=True)
            d = out - mu
            var = jnp.mean(d * d, axis=-1, keepdims=True)
            o = d * lax.rsqrt(var + EPS) * gn[:, h * RET_V_DIM:(h + 1) * RET_V_DIM]
            o_ref[pl.ds(base, C), h * RET_V_DIM:(h + 1) * RET_V_DIM] = o.astype(bf16)
        return tuple(states[h] * gamma_c[h] + kv[h] for h in hs)

    s0 = tuple(jnp.zeros((RET_QK_DIM, RET_V_DIM), f32) for _ in range(RET_HEADS))
    lax.fori_loop(0, seq // C, body, s0, unroll=2)


def _retention(rq, rk, rv, ret_gn, batch, seq):
    return pl.pallas_call(
        _retention_kernel,
        grid=(batch,),
        in_specs=[
            pl.BlockSpec((seq, RET_QK_WIDTH), lambda b: (b, 0)),
            pl.BlockSpec((seq, RET_QK_WIDTH), lambda b: (b, 0)),
            pl.BlockSpec((seq, RET_WIDTH), lambda b: (b, 0)),
            pl.BlockSpec((1, RET_WIDTH), lambda b: (0, 0)),
        ],
        out_specs=pl.BlockSpec((seq, RET_WIDTH), lambda b: (b, 0)),
        out_shape=jax.ShapeDtypeStruct((batch * seq, RET_WIDTH), bf16),
        compiler_params=pltpu.CompilerParams(
            dimension_semantics=("arbitrary",), vmem_limit_bytes=VMEM_LIMIT),
        name="retention",
    )(rq, rk, rv, ret_gn)


def _out_proj_kernel(x_ref, on_ref, or_ref, mq_ref, kvm_ref, gs_ref, w_ref, g_ref, o_ref):
    off = NSA_WIDTH + RET_WIDTH
    heads = range(MEM_HEADS)
    hsl = lambda h: slice(h * MEM_HEAD_DIM, (h + 1) * MEM_HEAD_DIM)
    s = [_dot_nt(mq_ref[:, hsl(h)], kvm_ref[:, hsl(h)]) for h in heads]
    mix_a = jnp.concatenate([on_ref[...] * gs_ref[:, 0:NSA_WIDTH],
                             or_ref[...] * gs_ref[:, NSA_WIDTH:off]], axis=1)
    y = _dot(mix_a, w_ref[0:off, :])
    mix_m = []
    for h in heads:
        m = jnp.max(s[h], axis=1, keepdims=True)
        e = jnp.exp2(s[h] - m)
        l = jnp.sum(e, axis=1, keepdims=True)
        om = _dot(e.astype(bf16), kvm_ref[:, MEM_WIDTH + h * MEM_HEAD_DIM:MEM_WIDTH + (h + 1) * MEM_HEAD_DIM])
        mix_m.append((om * (1.0 / l)).astype(bf16) * gs_ref[:, off + h * MEM_HEAD_DIM:off + (h + 1) * MEM_HEAD_DIM])
    y = y + _dot(jnp.concatenate(mix_m, axis=1), w_ref[off:, :])
    o_ref[...] = x_ref[...] + _rms(y, g_ref[...])


def _out_proj(x2, o_nsa, o_ret, mq, kvm, gsilu, w_out, norm_post, seq):
    bt = x2.shape[0]
    tm = OUT_ROW_TILE
    per_b = seq // tm
    row = lambda i: (i, 0)
    const = lambda i: (0, 0)
    return pl.pallas_call(
        _out_proj_kernel,
        grid=(bt // tm,),
        in_specs=[
            pl.BlockSpec((tm, D_MODEL), row),
            pl.BlockSpec((tm, NSA_WIDTH), row),
            pl.BlockSpec((tm, RET_WIDTH), row),
            pl.BlockSpec((tm, MEM_WIDTH), row),
            pl.BlockSpec((MEM_LEN, 2 * MEM_WIDTH), lambda i: (i // per_b, 0)),
            pl.BlockSpec((tm, MIX_WIDTH), row),
            pl.BlockSpec((MIX_WIDTH, D_MODEL), const),
            pl.BlockSpec((1, D_MODEL), const),
        ],
        out_specs=pl.BlockSpec((tm, D_MODEL), row),
        out_shape=jax.ShapeDtypeStruct((bt, D_MODEL), f32),
        compiler_params=pltpu.CompilerParams(
            dimension_semantics=("arbitrary",), vmem_limit_bytes=VMEM_LIMIT),
        name="out_proj",
    )(x2, o_nsa, o_ret, mq, kvm, gsilu, w_out, norm_post)


def _relayout_w_in(w):
    o = 0
    q_nsa = w[:, o:o + NSA_WIDTH]; o += NSA_WIDTH
    kv = w[:, o:o + NSA_KV_ALL]; o += NSA_KV_ALL
    gates = w[:, o:o + 3 * NSA_HEADS]; o += 3 * NSA_HEADS
    rest = w[:, o:]
    gates = gates.reshape(D_MODEL, 3, NSA_KV_GROUPS, NSA_REP).transpose(0, 2, 1, 3)
    gates = gates.reshape(D_MODEL, NSA_KV_GROUPS, 3 * NSA_REP)
    gates = jnp.pad(gates, ((0, 0), (0, 0), (0, GATE_PAD - 3 * NSA_REP)))
    gates = gates.reshape(D_MODEL, NSA_KV_GROUPS * GATE_PAD)
    return jnp.concatenate([q_nsa, kv, rest, gates], axis=1).astype(bf16)


def _overlap_t(seq):
    n_slot = seq // CMP_STRIDE
    n_sb = seq // SEL_BLOCK
    cmp_start = np.arange(n_slot) * CMP_STRIDE
    sel_start = np.arange(n_sb) * SEL_BLOCK
    ov = np.clip(np.minimum(cmp_start[None, :] + CMP_BLOCK, sel_start[:, None] + SEL_BLOCK)
                 - np.maximum(cmp_start[None, :], sel_start[:, None]), 0, None)
    return jnp.asarray(ov.astype(np.float32) / CMP_BLOCK)


def kernel(x, mem, positions, norm_pre, w_in, cmp_pos_k, cmp_w1_k, cmp_w2_k, cmp_pos_v, cmp_w1_v,
           cmp_w2_v, ret_gn, mem_norm, w_mem_kv, w_out, norm_post):
    depth = norm_pre.shape[0]
    batch, seq, _ = x.shape
    assert x.shape[2] == D_MODEL and mem.shape[1:] == (MEM_LEN, D_MODEL)
    assert seq % SLC_CHUNK == 0 and seq >= WINDOW + Q_BLOCK

    half = ROT_DIM // 2
    inv_n = ROPE_THETA ** (-jnp.arange(half, dtype=f32) / half)
    invn = jnp.concatenate([inv_n, inv_n, jnp.zeros((LANES - ROT_DIM,), f32)])[None, :]
    halfr = RET_QK_DIM // 2
    inv_r = RET_THETA ** (-jnp.arange(halfr, dtype=f32) / halfr)
    invr = jnp.tile(inv_r, LANES // halfr)[None, :]
    ovl_t = _overlap_t(seq)
    expand = jnp.asarray(np.arange(seq)[:, None] // SEL_BLOCK == np.arange(LANES)[None, :], bf16)
    pos2 = positions.reshape(batch * seq, 1)
    mem2 = mem.reshape(batch * MEM_LEN, D_MODEL)

    x2 = x.reshape(batch * seq, D_MODEL)
    for layer in range(depth):
        w_r = _relayout_w_in(w_in[layer])
        q, kv, rq, rk, rv, mq, gsilu, gsig = _in_proj(x2, pos2, norm_pre[layer][None, :], w_r, invn, invr)
        kvm = _mem_kv(mem2, mem_norm[layer][None, :], w_mem_kv[layer].astype(bf16))
        cmp = _compress(
            kv,
            jnp.stack([cmp_pos_k[layer], cmp_pos_v[layer]]),
            jnp.stack([cmp_w1_k[layer], cmp_w1_v[layer]]).astype(bf16),
            jnp.stack([cmp_w2_k[layer], cmp_w2_v[layer]]).astype(bf16),
            batch, seq)
        o_nsa = _nsa(q, cmp, kv, gsig, ovl_t, expand, batch, seq)
        o_ret = _retention(rq, rk, rv, ret_gn[layer][None, :], batch, seq)
        x2 = _out_proj(x2, o_nsa, o_ret, mq, kvm, gsilu, w_out[layer].astype(bf16),
                       norm_post[layer][None, :], seq)
    return x2.reshape(batch, seq, D_MODEL)
```

```python
import functools
import math

import jax
import jax.numpy as jnp
import numpy as np
from jax import lax
from jax.experimental import pallas as pl
from jax.experimental.pallas import tpu as pltpu

D_MODEL = 1024
MEM_LEN = 256
EPS = 1e-6

NSA_HEADS = 8
NSA_KV_GROUPS = 2
NSA_REP = NSA_HEADS // NSA_KV_GROUPS
NSA_HEAD_DIM = 128
CMP_BLOCK = 32
CMP_STRIDE = 16
CMP_HIDDEN = 256
SEL_BLOCK = 64
N_SELECT = 16
WINDOW = 512
Q_BLOCK = 256
ROPE_THETA = 500000.0
ROT_DIM = NSA_HEAD_DIM // 4

RET_HEADS = 4
RET_QK_DIM = 64
RET_V_DIM = 128
RET_CHUNK = 128
RET_THETA = 10000.0

MEM_HEADS = 4
MEM_HEAD_DIM = 128

NSA_WIDTH = NSA_HEADS * NSA_HEAD_DIM
NSA_KV_ALL = 3 * 2 * NSA_KV_GROUPS * NSA_HEAD_DIM
RET_QK_WIDTH = RET_HEADS * RET_QK_DIM
RET_WIDTH = RET_HEADS * RET_V_DIM
MEM_WIDTH = MEM_HEADS * MEM_HEAD_DIM
MIX_WIDTH = NSA_WIDTH + RET_WIDTH + MEM_WIDTH
GATE_PAD = 128

LANES = 128
SUBLANES = 8
NEG = -1e30
LOG2E = math.log2(math.e)
NSA_QSCALE = NSA_HEAD_DIM ** -0.5 * LOG2E
MEM_QSCALE = MEM_HEAD_DIM ** -0.5 * LOG2E
VMEM_LIMIT = 56 * 1024 * 1024

ROW_TILE = 256
OUT_ROW_TILE = 512
SLC_CHUNK = 512

_NT = (((1,), (1,)), ((), ()))
_TN = (((0,), (0,)), ((), ()))

bf16 = jnp.bfloat16
f32 = jnp.float32


def _dot(a, b):
    return jnp.dot(a, b, preferred_element_type=f32)


def _dot_nt(a, b):
    return lax.dot_general(a, b, _NT, preferred_element_type=f32)


def _dot_tn(a, b):
    return lax.dot_general(a, b, _TN, preferred_element_type=f32)


def _col_reduce(op, x):
    slabs = [x[i:i + SUBLANES] for i in range(0, x.shape[0], SUBLANES)]
    while len(slabs) > 1:
        nxt = [op(slabs[i], slabs[i + 1]) for i in range(0, len(slabs) - 1, 2)]
        if len(slabs) % 2:
            nxt.append(slabs[-1])
        slabs = nxt
    red = jnp.max if op is jnp.maximum else jnp.sum
    return red(slabs[0], axis=0, keepdims=True)


def _rms(x, g):
    return x * lax.rsqrt(jnp.mean(x * x, axis=-1, keepdims=True) + EPS) * g


def _rotate_heads(acc, c, s1, s2, shift):
    outs = []
    for j in range(acc.shape[1] // LANES):
        a = acc[:, j * LANES:(j + 1) * LANES]
        outs.append(a * c + pltpu.roll(a, shift, 1) * s1 + pltpu.roll(a, LANES - shift, 1) * s2)
    return outs


def _in_proj_kernel(x_ref, pos_ref, g_ref, w_ref, inv_ref,
                    q_ref, kv_ref, rq_ref, rk_ref, rv_ref, mq_ref, gs_ref, gg_ref):
    x = x_ref[...]
    hb = _rms(x, g_ref[...]).astype(bf16)
    c_kv = NSA_WIDTH
    c_rqk = c_kv + NSA_KV_ALL
    c_rv = c_rqk + 2 * RET_QK_WIDTH
    c_mq = c_rv + RET_WIDTH
    c_gate = c_mq + MEM_WIDTH
    c_gg = c_gate + MIX_WIDTH

    pos = pos_ref[...].astype(f32)
    lane = lax.broadcasted_iota(jnp.int32, (x.shape[0], LANES), 1)
    ang = pos * inv_ref[...]
    cs, sn = jnp.cos(ang), jnp.sin(ang)
    half = ROT_DIM // 2
    cn = jnp.where(lane < ROT_DIM, cs, 1.0)
    s1n = jnp.where((lane >= half) & (lane < ROT_DIM), sn, 0.0)
    s2n = jnp.where(lane < half, -sn, 0.0)
    halfr = RET_QK_DIM // 2
    cr = jnp.where(lane >= RET_QK_DIM, cs, pltpu.roll(cs, RET_QK_DIM, 1))
    sr = jnp.where(lane >= RET_QK_DIM, sn, pltpu.roll(sn, RET_QK_DIM, 1))
    lr = lane & (RET_QK_DIM - 1)
    s1r = jnp.where(lr >= halfr, sr, 0.0)
    s2r = jnp.where(lr < halfr, -sr, 0.0)

    for j in range(MIX_WIDTH // 512):
        acc = _dot(hb, w_ref[:, c_gate + j * 512:c_gate + (j + 1) * 512])
        gs_ref[:, j * 512:(j + 1) * 512] = (acc * (1.0 / (1.0 + jnp.exp(-acc)))).astype(bf16)
    mq_ref[...] = (_dot(hb, w_ref[:, c_mq:c_mq + MEM_WIDTH]) * MEM_QSCALE).astype(bf16)
    acc = _dot(hb, w_ref[:, c_gg:c_gg + 2 * GATE_PAD])
    gg_ref[...] = 1.0 / (1.0 + jnp.exp(-acc))

    for j in range(NSA_WIDTH // 512):
        acc = _dot(hb, w_ref[:, j * 512:(j + 1) * 512])
        for i, o in enumerate(_rotate_heads(acc, cn * NSA_QSCALE, s1n * NSA_QSCALE, s2n * NSA_QSCALE, half)):
            q_ref[:, j * 512 + i * LANES: j * 512 + (i + 1) * LANES] = o.astype(bf16)
    for br in range(3):
        acc = _dot(hb, w_ref[:, c_kv + br * 512:c_kv + (br + 1) * 512])
        for i, o in enumerate(_rotate_heads(acc[:, :256], cn, s1n, s2n, half)):
            kv_ref[:, br * 512 + i * LANES: br * 512 + (i + 1) * LANES] = o.astype(bf16)
        kv_ref[:, br * 512 + 256: br * 512 + 512] = acc[:, 256:].astype(bf16)
    acc = _dot(hb, w_ref[:, c_rqk:c_rqk + 2 * RET_QK_WIDTH])
    rot = _rotate_heads(acc, cr, s1r, s2r, halfr)
    for i in range(2):
        rq_ref[:, i * LANES:(i + 1) * LANES] = rot[i].astype(bf16)
        rk_ref[:, i * LANES:(i + 1) * LANES] = (rot[2 + i] * (RET_QK_DIM ** -0.5)).astype(bf16)
    rv_ref[...] = _dot(hb, w_ref[:, c_rv:c_rv + RET_WIDTH]).astype(bf16)


def _in_proj(x2, pos2, norm_pre, w_r, inv):
    bt = x2.shape[0]
    tm = ROW_TILE
    win = w_r.shape[1]
    row = lambda i: (i, 0)
    const = lambda i: (0, 0)
    widths = (NSA_WIDTH, NSA_KV_ALL, RET_QK_WIDTH, RET_QK_WIDTH, RET_WIDTH, MEM_WIDTH, MIX_WIDTH)
    out_shape = [jax.ShapeDtypeStruct((bt, w), bf16) for w in widths]
    out_shape.append(jax.ShapeDtypeStruct((bt, 2 * GATE_PAD), f32))
    out_specs = [pl.BlockSpec((tm, w), row) for w in widths] + [pl.BlockSpec((tm, 2 * GATE_PAD), row)]
    return pl.pallas_call(
        _in_proj_kernel,
        grid=(bt // tm,),
        in_specs=[
            pl.BlockSpec((tm, D_MODEL), row),
            pl.BlockSpec((tm, 1), row),
            pl.BlockSpec((1, D_MODEL), const),
            pl.BlockSpec((D_MODEL, win), const),
            pl.BlockSpec((1, LANES), const),
        ],
        out_specs=out_specs,
        out_shape=out_shape,
        compiler_params=pltpu.CompilerParams(
            dimension_semantics=("arbitrary",), vmem_limit_bytes=VMEM_LIMIT),
        name="in_proj",
    )(x2, pos2, norm_pre, w_r, inv)


def _mem_kv_kernel(m_ref, g_ref, w_ref, o_ref):
    hb = _rms(m_ref[...], g_ref[...]).astype(bf16)
    o_ref[...] = _dot(hb, w_ref[...]).astype(bf16)


def _mem_kv(mem2, mem_norm, w_kv):
    n = mem2.shape[0]
    tm = MEM_LEN
    return pl.pallas_call(
        _mem_kv_kernel,
        grid=(n // tm,),
        in_specs=[
            pl.BlockSpec((tm, D_MODEL), lambda i: (i, 0)),
            pl.BlockSpec((1, D_MODEL), lambda i: (0, 0)),
            pl.BlockSpec((D_MODEL, 2 * MEM_WIDTH), lambda i: (0, 0)),
        ],
        out_specs=pl.BlockSpec((tm, 2 * MEM_WIDTH), lambda i: (i, 0)),
        out_shape=jax.ShapeDtypeStruct((n, 2 * MEM_WIDTH), bf16),
        compiler_params=pltpu.CompilerParams(
            dimension_semantics=("arbitrary",), vmem_limit_bytes=VMEM_LIMIT),
        name="mem_kv",
    )(mem2, mem_norm, w_kv)


def _compress_kernel(kv_ref, pos_ref, w1_ref, w2_ref, o_ref, xf_ref):
    seq = kv_ref.shape[0]
    nslot = seq // CMP_STRIDE
    xf_ref[...] = kv_ref[...].astype(f32)
    pos = pos_ref[0]
    ha = jnp.zeros((nslot, CMP_HIDDEN), f32)
    hb = jnp.zeros((nslot, CMP_HIDDEN), f32)
    for p in range(CMP_STRIDE):
        a = xf_ref[pl.ds(p, nslot, stride=CMP_STRIDE), :]
        la = (a + pos[p:p + 1, :]).astype(bf16)
        lb = (a + pos[CMP_STRIDE + p:CMP_STRIDE + p + 1, :]).astype(bf16)
        ha = ha + _dot(la, w1_ref[0, p * NSA_HEAD_DIM:(p + 1) * NSA_HEAD_DIM, :])
        hb = hb + _dot(lb, w1_ref[0, (CMP_STRIDE + p) * NSA_HEAD_DIM:(CMP_STRIDE + p + 1) * NSA_HEAD_DIM, :])
    h = ha + pltpu.roll(hb, nslot - 1, 0)
    h = jax.nn.gelu(h)
    out = _dot(h.astype(bf16), w2_ref[0])
    rowi = lax.broadcasted_iota(jnp.int32, out.shape, 0)
    o_ref[0, 0] = jnp.where(rowi < nslot - 1, out, 0.0).astype(bf16)


def _compress(kv, cmp_pos, cmp_w1, cmp_w2, batch, seq):
    nslot = seq // CMP_STRIDE
    return pl.pallas_call(
        _compress_kernel,
        grid=(batch, 2 * NSA_KV_GROUPS),
        in_specs=[
            pl.BlockSpec((seq, NSA_HEAD_DIM), lambda b, j: (b, j)),
            pl.BlockSpec((1, CMP_BLOCK, NSA_HEAD_DIM), lambda b, j: (j // NSA_KV_GROUPS, 0, 0)),
            pl.BlockSpec((1, CMP_BLOCK * NSA_HEAD_DIM, CMP_HIDDEN), lambda b, j: (j // NSA_KV_GROUPS, 0, 0)),
            pl.BlockSpec((1, CMP_HIDDEN, NSA_HEAD_DIM), lambda b, j: (j // NSA_KV_GROUPS, 0, 0)),
        ],
        out_specs=pl.BlockSpec((1, 1, nslot, NSA_HEAD_DIM), lambda b, j: (b, j, 0, 0)),
        out_shape=jax.ShapeDtypeStruct((batch, 2 * NSA_KV_GROUPS, nslot, NSA_HEAD_DIM), bf16),
        scratch_shapes=[pltpu.VMEM((seq, NSA_HEAD_DIM), f32)],
        compiler_params=pltpu.CompilerParams(
            dimension_semantics=("arbitrary", "arbitrary"), vmem_limit_bytes=VMEM_LIMIT),
        name="compress",
    )(kv, cmp_pos, cmp_w1, cmp_w2)


def _nsa_tile(n_static, q_ref, kc_ref, vc_ref, ks_ref, vs_ref, kw_ref, vw_ref, g_ref, ovl_ref, exp_ref,
              o_ref):
    qi = pl.program_id(2)
    t0 = qi * Q_BLOCK
    nslot = kc_ref.shape[2]
    n_cmp = nslot - 1
    heads = range(NSA_REP)
    hcols = lambda a, r: a[:, r * Q_BLOCK:(r + 1) * Q_BLOCK]

    q = q_ref[...]
    q4 = jnp.concatenate([q[:, r * NSA_HEAD_DIM:(r + 1) * NSA_HEAD_DIM] for r in heads], axis=0)
    tq = t0 + lax.broadcasted_iota(jnp.int32, (1, Q_BLOCK), 1)

    def softmax_cols(s_r):
        m = _col_reduce(jnp.maximum, s_r)
        e = jnp.exp2(s_r - m)
        return e, _col_reduce(jnp.add, e)

    s = _dot_nt(kc_ref[0, 0], q4)
    wlen = WINDOW + Q_BLOCK
    start = pl.multiple_of(jnp.maximum(t0 - WINDOW, 0), Q_BLOCK)
    sw = _dot_nt(kw_ref[pl.ds(start, wlen), :], q4)
    sc0 = _dot_nt(ks_ref[0:SLC_CHUNK, :], q4)
    nrow = lax.broadcasted_iota(jnp.int32, (nslot, 1), 0)
    bias_c = jnp.where((nrow * CMP_STRIDE + (CMP_BLOCK - 1) <= tq) & (nrow < n_cmp), 0.0, NEG)
    row_ok = tq >= CMP_BLOCK - 1
    ps, psum = [], None
    for r in heads:
        e, l = softmax_cols(hcols(s, r) + bias_c)
        p = e * jnp.where(row_ok, 1.0 / l, 0.0)
        ps.append(p.astype(bf16))
        psum = p if psum is None else psum + p
    o_cmp = _dot_tn(vc_ref[0, 0], jnp.concatenate(ps, axis=1))

    imp_t = jnp.dot(ovl_ref[...], psum, precision=lax.Precision.HIGHEST,
                    preferred_element_type=f32)
    n_sb = imp_t.shape[0]
    jblk = lax.broadcasted_iota(jnp.int32, (n_sb, Q_BLOCK), 0)
    tl = t0 + lax.broadcasted_iota(jnp.int32, (n_sb, Q_BLOCK), 1)
    cur = tl >> 6
    forced = (jblk == 0) | (jblk == cur) | (jblk == cur - 1)
    valid = jblk * SEL_BLOCK <= tl
    v = jnp.where(forced, jnp.inf, jnp.where(valid, imp_t, -jnp.inf))
    n_live = min(n_sb, n_static * SLC_CHUNK // SEL_BLOCK)
    slabs = [v[k:k + SUBLANES] for k in range(0, n_live, SUBLANES)]
    jl = lax.broadcasted_iota(jnp.int32, (SUBLANES, Q_BLOCK), 0)
    cnts = [jnp.zeros((SUBLANES, Q_BLOCK), f32) for _ in slabs]
    for i in range(n_live):
        vi = jnp.broadcast_to(v[i:i + 1, :], (SUBLANES, Q_BLOCK))
        for k, vk in enumerate(slabs):
            if k * SUBLANES > i:
                ahead = vi >= vk
            elif (k + 1) * SUBLANES <= i:
                ahead = vi > vk
            else:
                ahead = (vi > vk) | ((vi == vk) & (jl > i - k * SUBLANES))
            cnts[k] = cnts[k] + jnp.where(ahead, 1.0, 0.0)
    selbias_pad = jnp.concatenate([jnp.where(c < float(N_SELECT), 0.0, NEG) for c in cnts]
                                  + [jnp.zeros((LANES - n_live, Q_BLOCK), f32)], axis=0)
    q4_aug = jnp.concatenate([q4, jnp.concatenate([selbias_pad.T.astype(bf16)] * NSA_REP, axis=0)], axis=1)

    def gate_rows(br):
        return jnp.concatenate([gs[br * NSA_REP + r:br * NSA_REP + r + 1, :] for r in heads], axis=1)

    if n_static * SLC_CHUNK > WINDOW:
        kl = lax.broadcasted_iota(jnp.int32, (Q_BLOCK, Q_BLOCK), 0)
        ql = lax.broadcasted_iota(jnp.int32, (Q_BLOCK, Q_BLOCK), 1)
        head_bias = jnp.where(kl > ql, 0.0, NEG)
        tail_bias = jnp.where(kl <= ql, 0.0, NEG)
        add_bias_w = lambda a: jnp.concatenate(
            [a[:Q_BLOCK] + head_bias, a[Q_BLOCK:WINDOW], a[WINDOW:] + tail_bias], axis=0)
    else:
        rel = tq - (start + lax.broadcasted_iota(jnp.int32, (wlen, 1), 0))
        bias_w = jnp.where((rel >= 0) & (rel < WINDOW), 0.0, NEG)
        add_bias_w = lambda a: a + bias_w
    gs = g_ref[...].T
    es, ls = [], []
    for r in heads:
        e, l = softmax_cols(add_bias_w(hcols(sw, r)))
        es.append(e.astype(bf16))
        ls.append(l)
    o_win = _dot_tn(vw_ref[pl.ds(start, wlen), :], jnp.concatenate(es, axis=1))
    o_part = (gate_rows(0) * o_cmp
              + (gate_rows(2) * (1.0 / jnp.concatenate(ls, axis=1))) * o_win)

    krow = lax.broadcasted_iota(jnp.int32, (SLC_CHUNK, 1), 0)
    seq = ks_ref.shape[0]

    def slc_branch(n):
        use_sel = n * SLC_CHUNK > N_SELECT * SEL_BLOCK

        def slc_scores(c):
            ks = ks_ref[c * SLC_CHUNK:(c + 1) * SLC_CHUNK, :]
            if not use_sel:
                return _dot_nt(ks, q4)
            return _dot_nt(jnp.concatenate([ks, exp_ref[c * SLC_CHUNK:(c + 1) * SLC_CHUNK, :]], axis=1), q4_aug)

        m_i, l_i, acc = [None] * NSA_REP, [None] * NSA_REP, None
        sc = sc0
        for c in range(n):
            bias = None
            if use_sel and c == 0:
                bias = _dot(exp_ref[0:SLC_CHUNK, :], selbias_pad.astype(bf16))
            sc_next = slc_scores(c + 1) if c + 1 < n else None
            if c == n - 1:
                causal = jnp.where(c * SLC_CHUNK + krow <= tq, 0.0, NEG)
                bias = causal if bias is None else bias + causal
            es, alphas = [], []
            for r in heads:
                s_r = hcols(sc, r) if bias is None else hcols(sc, r) + bias
                m_c = _col_reduce(jnp.maximum, s_r)
                if c == 0:
                    e = jnp.exp2(s_r - m_c)
                    m_i[r], l_i[r] = m_c, _col_reduce(jnp.add, e)
                else:
                    m_new = jnp.maximum(m_i[r], m_c)
                    alpha = jnp.exp2(m_i[r] - m_new)
                    e = jnp.exp2(s_r - m_new)
                    m_i[r], l_i[r] = m_new, alpha * l_i[r] + _col_reduce(jnp.add, e)
                    alphas.append(alpha)
                es.append(e.astype(bf16))
            pv = _dot_tn(vs_ref[c * SLC_CHUNK:(c + 1) * SLC_CHUNK, :], jnp.concatenate(es, axis=1))
            acc = pv if c == 0 else jnp.concatenate(alphas, axis=1) * acc + pv
            sc = sc_next
        o = o_part + (gate_rows(1) * (1.0 / jnp.concatenate(l_i, axis=1))) * acc
        for r in heads:
            o_ref[:, r * NSA_HEAD_DIM:(r + 1) * NSA_HEAD_DIM] = hcols(o, r).T.astype(bf16)

    slc_branch(n_static)


def _nsa_kernel(*refs):
    seq = refs[3].shape[0]
    n_chunks = (pl.program_id(2) * Q_BLOCK + Q_BLOCK + SLC_CHUNK - 1) // SLC_CHUNK
    for n in range(1, seq // SLC_CHUNK + 1):
        pl.when(n_chunks == n)(functools.partial(_nsa_tile, n, *refs))


def _nsa(q, cmp, kv, gsig, ovl_t, expand, batch, seq):
    nq = seq // Q_BLOCK
    gw = NSA_REP * NSA_HEAD_DIM
    nslot = cmp.shape[2]
    G = NSA_KV_GROUPS
    kvspec = lambda off: pl.BlockSpec((seq, NSA_HEAD_DIM), lambda b, g, i, off=off: (b, off + g))
    return pl.pallas_call(
        _nsa_kernel,
        grid=(batch, G, nq),
        in_specs=[
            pl.BlockSpec((Q_BLOCK, gw), lambda b, g, i: (b * nq + i, g)),
            pl.BlockSpec((1, 1, nslot, NSA_HEAD_DIM), lambda b, g, i: (b, g, 0, 0)),
            pl.BlockSpec((1, 1, nslot, NSA_HEAD_DIM), lambda b, g, i: (b, G + g, 0, 0)),
            kvspec(2 * G), kvspec(3 * G), kvspec(4 * G), kvspec(5 * G),
            pl.BlockSpec((Q_BLOCK, GATE_PAD), lambda b, g, i: (b * nq + i, g)),
            pl.BlockSpec(ovl_t.shape, lambda b, g, i: (0, 0)),
            pl.BlockSpec(expand.shape, lambda b, g, i: (0, 0)),
        ],
        out_specs=pl.BlockSpec((Q_BLOCK, gw), lambda b, g, i: (b * nq + i, g)),
        out_shape=jax.ShapeDtypeStruct((batch * seq, NSA_WIDTH), bf16),
        compiler_params=pltpu.CompilerParams(
            dimension_semantics=("arbitrary", "arbitrary", "arbitrary"), vmem_limit_bytes=VMEM_LIMIT),
        name="nsa",
    )(q, cmp, cmp, kv, kv, kv, kv, gsig, ovl_t, expand)


def _retention_kernel(q_ref, k_ref, v_ref, gn_ref, o_ref):
    seq = q_ref.shape[0]
    C = RET_CHUNK
    n = lax.broadcasted_iota(jnp.int32, (C, C), 0)
    mcol = lax.broadcasted_iota(jnp.int32, (C, C), 1)
    diff = (n - mcol).astype(f32)
    nvec = lax.broadcasted_iota(jnp.int32, (C, 1), 0).astype(f32)
    decay, xi, zeta, gamma_c = [], [], [], []
    for h in range(RET_HEADS):
        lg = math.log1p(-(2.0 ** (-5.0 - h)))
        decay.append(jnp.where(diff >= 0, jnp.exp(lg * jnp.maximum(diff, 0.0)), 0.0))
        xi.append(jnp.exp(lg * (nvec + 1.0)))
        zeta.append(jnp.exp(lg * (C - 1.0 - nvec)))
        gamma_c.append(math.exp(lg * C))
    gn = gn_ref[...]

    def body(c, states):
        base = pl.multiple_of(c * C, C)
        hs = range(RET_HEADS)
        qc = [q_ref[pl.ds(base, C), h * RET_QK_DIM:(h + 1) * RET_QK_DIM] for h in hs]
        kc = [k_ref[pl.ds(base, C), h * RET_QK_DIM:(h + 1) * RET_QK_DIM] for h in hs]
        vc = [v_ref[pl.ds(base, C), h * RET_V_DIM:(h + 1) * RET_V_DIM] for h in hs]
        qk = [_dot_nt(qc[h], kc[h]) for h in hs]
        cross = [_dot(qc[h], states[h].astype(bf16)) for h in hs]
        kv = [_dot_tn((kc[h].astype(f32) * zeta[h]).astype(bf16), vc[h]) for h in hs]
        intra = [_dot((qk[h] * decay[h]).astype(bf16), vc[h]) for h in hs]
        for h in hs:
            out = intra[h] + cross[h] * xi[h]
            mu = jnp.mean(out, axis=-1, keepdims=True)
            d = out - mu
            var = jnp.mean(d * d, axis=-1, keepdims=True)
            o = d * lax.rsqrt(var + EPS) * gn[:, h * RET_V_DIM:(h + 1) * RET_V_DIM]
            o_ref[pl.ds(base, C), h * RET_V_DIM:(h + 1) * RET_V_DIM] = o.astype(bf16)
        return tuple(states[h] * gamma_c[h] + kv[h] for h in hs)

    s0 = tuple(jnp.zeros((RET_QK_DIM, RET_V_DIM), f32) for _ in range(RET_HEADS))
    lax.fori_loop(0, seq // C, body, s0, unroll=2)


def _retention(rq, rk, rv, ret_gn, batch, seq):
    return pl.pallas_call(
        _retention_kernel,
        grid=(batch,),
        in_specs=[
            pl.BlockSpec((seq, RET_QK_WIDTH), lambda b: (b, 0)),
            pl.BlockSpec((seq, RET_QK_WIDTH), lambda b: (b, 0)),
            pl.BlockSpec((seq, RET_WIDTH), lambda b: (b, 0)),
            pl.BlockSpec((1, RET_WIDTH), lambda b: (0, 0)),
        ],
        out_specs=pl.BlockSpec((seq, RET_WIDTH), lambda b: (b, 0)),
        out_shape=jax.ShapeDtypeStruct((batch * seq, RET_WIDTH), bf16),
        compiler_params=pltpu.CompilerParams(
            dimension_semantics=("arbitrary",), vmem_limit_bytes=VMEM_LIMIT),
        name="retention",
    )(rq, rk, rv, ret_gn)


def _out_proj_kernel(x_ref, on_ref, or_ref, mq_ref, kvm_ref, gs_ref, w_ref, g_ref, o_ref):
    off = NSA_WIDTH + RET_WIDTH
    heads = range(MEM_HEADS)
    hsl = lambda h: slice(h * MEM_HEAD_DIM, (h + 1) * MEM_HEAD_DIM)
    s = [_dot_nt(mq_ref[:, hsl(h)], kvm_ref[:, hsl(h)]) for h in heads]
    mix_a = jnp.concatenate([on_ref[...] * gs_ref[:, 0:NSA_WIDTH],
                             or_ref[...] * gs_ref[:, NSA_WIDTH:off]], axis=1)
    y = _dot(mix_a, w_ref[0:off, :])
    mix_m = []
    for h in heads:
        m = jnp.max(s[h], axis=1, keepdims=True)
        e = jnp.exp2(s[h] - m)
        l = jnp.sum(e, axis=1, keepdims=True)
        om = _dot(e.astype(bf16), kvm_ref[:, MEM_WIDTH + h * MEM_HEAD_DIM:MEM_WIDTH + (h + 1) * MEM_HEAD_DIM])
        mix_m.append((om * (1.0 / l)).astype(bf16) * gs_ref[:, off + h * MEM_HEAD_DIM:off + (h + 1) * MEM_HEAD_DIM])
    y = y + _dot(jnp.concatenate(mix_m, axis=1), w_ref[off:, :])
    o_ref[...] = x_ref[...] + _rms(y, g_ref[...])


def _out_proj(x2, o_nsa, o_ret, mq, kvm, gsilu, w_out, norm_post, seq):
    bt = x2.shape[0]
    tm = OUT_ROW_TILE
    per_b = seq // tm
    row = lambda i: (i, 0)
    const = lambda i: (0, 0)
    return pl.pallas_call(
        _out_proj_kernel,
        grid=(bt // tm,),
        in_specs=[
            pl.BlockSpec((tm, D_MODEL), row),
            pl.BlockSpec((tm, NSA_WIDTH), row),
            pl.BlockSpec((tm, RET_WIDTH), row),
            pl.BlockSpec((tm, MEM_WIDTH), row),
            pl.BlockSpec((MEM_LEN, 2 * MEM_WIDTH), lambda i: (i // per_b, 0)),
            pl.BlockSpec((tm, MIX_WIDTH), row),
            pl.BlockSpec((MIX_WIDTH, D_MODEL), const),
            pl.BlockSpec((1, D_MODEL), const),
        ],
        out_specs=pl.BlockSpec((tm, D_MODEL), row),
        out_shape=jax.ShapeDtypeStruct((bt, D_MODEL), f32),
        compiler_params=pltpu.CompilerParams(
            dimension_semantics=("arbitrary",), vmem_limit_bytes=VMEM_LIMIT),
        name="out_proj",
    )(x2, o_nsa, o_ret, mq, kvm, gsilu, w_out, norm_post)


def _relayout_w_in(w):
    o = 0
    q_nsa = w[:, o:o + NSA_WIDTH]; o += NSA_WIDTH
    kv = w[:, o:o + NSA_KV_ALL]; o += NSA_KV_ALL
    gates = w[:, o:o + 3 * NSA_HEADS]; o += 3 * NSA_HEADS
    rest = w[:, o:]
    gates = gates.reshape(D_MODEL, 3, NSA_KV_GROUPS, NSA_REP).transpose(0, 2, 1, 3)
    gates = gates.reshape(D_MODEL, NSA_KV_GROUPS, 3 * NSA_REP)
    gates = jnp.pad(gates, ((0, 0), (0, 0), (0, GATE_PAD - 3 * NSA_REP)))
    gates = gates.reshape(D_MODEL, NSA_KV_GROUPS * GATE_PAD)
    return jnp.concatenate([q_nsa, kv, rest, gates], axis=1).astype(bf16)


def _overlap_t(seq):
    n_slot = seq // CMP_STRIDE
    n_sb = seq // SEL_BLOCK
    cmp_start = np.arange(n_slot) * CMP_STRIDE
    sel_start = np.arange(n_sb) * SEL_BLOCK
    ov = np.clip(np.minimum(cmp_start[None, :] + CMP_BLOCK, sel_start[:, None] + SEL_BLOCK)
                 - np.maximum(cmp_start[None, :], sel_start[:, None]), 0, None)
    return jnp.asarray(ov.astype(np.float32) / CMP_BLOCK)


def kernel(x, mem, positions, norm_pre, w_in, cmp_pos_k, cmp_w1_k, cmp_w2_k, cmp_pos_v, cmp_w1_v,
           cmp_w2_v, ret_gn, mem_norm, w_mem_kv, w_out, norm_post):
    depth = norm_pre.shape[0]
    batch, seq, _ = x.shape
    assert x.shape[2] == D_MODEL and mem.shape[1:] == (MEM_LEN, D_MODEL)
    assert seq % SLC_CHUNK == 0 and seq >= WINDOW + Q_BLOCK

    half = ROT_DIM // 2
    inv_n = ROPE_THETA ** (-jnp.arange(half, dtype=f32) / half)
    halfr = RET_QK_DIM // 2
    inv_r = RET_THETA ** (-jnp.arange(halfr, dtype=f32) / halfr)
    assert ROT_DIM <= RET_QK_DIM and 2 * RET_QK_DIM == LANES
    inv = jnp.concatenate([inv_n, inv_n, jnp.zeros((RET_QK_DIM - ROT_DIM,), f32), inv_r, inv_r])[None, :]
    ovl_t = _overlap_t(seq)
    expand = jnp.asarray(np.arange(seq)[:, None] // SEL_BLOCK == np.arange(LANES)[None, :], bf16)
    pos2 = positions.reshape(batch * seq, 1)
    mem2 = mem.reshape(batch * MEM_LEN, D_MODEL)

    x2 = x.reshape(batch * seq, D_MODEL)
    for layer in range(depth):
        w_r = _relayout_w_in(w_in[layer])
        q, kv, rq, rk, rv, mq, gsilu, gsig = _in_proj(x2, pos2, norm_pre[layer][None, :], w_r, inv)
        kvm = _mem_kv(mem2, mem_norm[layer][None, :], w_mem_kv[layer].astype(bf16))
        cmp = _compress(
            kv,
            jnp.stack([cmp_pos_k[layer], cmp_pos_v[layer]]),
            jnp.stack([cmp_w1_k[layer], cmp_w1_v[layer]]).astype(bf16),
            jnp.stack([cmp_w2_k[layer], cmp_w2_v[layer]]).astype(bf16),
            batch, seq)
        o_nsa = _nsa(q, cmp, kv, gsig, ovl_t, expand, batch, seq)
        o_ret = _retention(rq, rk, rv, ret_gn[layer][None, :], batch, seq)
        x2 = _out_proj(x2, o_nsa, o_ret, mq, kvm, gsilu, w_out[layer].astype(bf16),
                       norm_post[layer][None, :], seq)
    return x2.reshape(batch, seq, D_MODEL)
```

```python
import functools
import math

import jax
import jax.numpy as jnp
import numpy as np
from jax import lax
from jax.experimental import pallas as pl
from jax.experimental.pallas import tpu as pltpu

D_MODEL = 1024
MEM_LEN = 256
EPS = 1e-6

NSA_HEADS = 8
NSA_KV_GROUPS = 2
NSA_REP = NSA_HEADS // NSA_KV_GROUPS
NSA_HEAD_DIM = 128
CMP_BLOCK = 32
CMP_STRIDE = 16
CMP_HIDDEN = 256
SEL_BLOCK = 64
N_SELECT = 16
WINDOW = 512
Q_BLOCK = 256
ROPE_THETA = 500000.0
ROT_DIM = NSA_HEAD_DIM // 4

RET_HEADS = 4
RET_QK_DIM = 64
RET_V_DIM = 128
RET_CHUNK = 128
RET_THETA = 10000.0

MEM_HEADS = 4
MEM_HEAD_DIM = 128

NSA_WIDTH = NSA_HEADS * NSA_HEAD_DIM
NSA_KV_ALL = 3 * 2 * NSA_KV_GROUPS * NSA_HEAD_DIM
RET_QK_WIDTH = RET_HEADS * RET_QK_DIM
RET_WIDTH = RET_HEADS * RET_V_DIM
MEM_WIDTH = MEM_HEADS * MEM_HEAD_DIM
MIX_WIDTH = NSA_WIDTH + RET_WIDTH + MEM_WIDTH
GATE_PAD = 128

LANES = 128
SUBLANES = 8
NEG = -1e30
LOG2E = math.log2(math.e)
NSA_QSCALE = NSA_HEAD_DIM ** -0.5 * LOG2E
MEM_QSCALE = MEM_HEAD_DIM ** -0.5 * LOG2E
VMEM_LIMIT = 56 * 1024 * 1024

ROW_TILE = 256
CMP_PITCH = 24
CMP_BATCH = 4
OUT_ROW_TILE = 512
SLC_CHUNK = 512

_NT = (((1,), (1,)), ((), ()))
_TN = (((0,), (0,)), ((), ()))

bf16 = jnp.bfloat16
f32 = jnp.float32


def _dot(a, b):
    return jnp.dot(a, b, preferred_element_type=f32)


def _dot_nt(a, b):
    return lax.dot_general(a, b, _NT, preferred_element_type=f32)


def _dot_tn(a, b):
    return lax.dot_general(a, b, _TN, preferred_element_type=f32)


def _col_reduce(op, x):
    slabs = [x[i:i + SUBLANES] for i in range(0, x.shape[0], SUBLANES)]
    while len(slabs) > 1:
        nxt = [op(slabs[i], slabs[i + 1]) for i in range(0, len(slabs) - 1, 2)]
        if len(slabs) % 2:
            nxt.append(slabs[-1])
        slabs = nxt
    red = jnp.max if op is jnp.maximum else jnp.sum
    return red(slabs[0], axis=0, keepdims=True)


def _rms(x, g):
    return x * lax.rsqrt(jnp.mean(x * x, axis=-1, keepdims=True) + EPS) * g


def _rotate_heads(acc, c, s1, s2, shift):
    outs = []
    for j in range(acc.shape[1] // LANES):
        a = acc[:, j * LANES:(j + 1) * LANES]
        outs.append(a * c + pltpu.roll(a, shift, 1) * s1 + pltpu.roll(a, LANES - shift, 1) * s2)
    return outs


def _in_proj_kernel(x_ref, pos_ref, g_ref, w_ref, inv_ref,
                    q_ref, kv_ref, rq_ref, rk_ref, rv_ref, mq_ref, gs_ref, gg_ref):
    x = x_ref[...]
    hb = _rms(x, g_ref[...]).astype(bf16)
    c_kv = NSA_WIDTH
    c_rqk = c_kv + NSA_KV_ALL
    c_rv = c_rqk + 2 * RET_QK_WIDTH
    c_mq = c_rv + RET_WIDTH
    c_gate = c_mq + MEM_WIDTH
    c_gg = c_gate + MIX_WIDTH

    pos = pos_ref[...].astype(f32)
    lane = lax.broadcasted_iota(jnp.int32, (x.shape[0], LANES), 1)
    ang = pos * inv_ref[...]
    cs, sn = jnp.cos(ang), jnp.sin(ang)
    half = ROT_DIM // 2
    cn = jnp.where(lane < ROT_DIM, cs, 1.0)
    s1n = jnp.where((lane >= half) & (lane < ROT_DIM), sn, 0.0)
    s2n = jnp.where(lane < half, -sn, 0.0)
    halfr = RET_QK_DIM // 2
    cr = jnp.where(lane >= RET_QK_DIM, cs, pltpu.roll(cs, RET_QK_DIM, 1))
    sr = jnp.where(lane >= RET_QK_DIM, sn, pltpu.roll(sn, RET_QK_DIM, 1))
    lr = lane & (RET_QK_DIM - 1)
    s1r = jnp.where(lr >= halfr, sr, 0.0)
    s2r = jnp.where(lr < halfr, -sr, 0.0)

    for j in range(MIX_WIDTH // 512):
        acc = _dot(hb, w_ref[:, c_gate + j * 512:c_gate + (j + 1) * 512])
        gs_ref[:, j * 512:(j + 1) * 512] = (acc * (1.0 / (1.0 + jnp.exp(-acc)))).astype(bf16)
    mq_ref[...] = (_dot(hb, w_ref[:, c_mq:c_mq + MEM_WIDTH]) * MEM_QSCALE).astype(bf16)
    acc = _dot(hb, w_ref[:, c_gg:c_gg + 2 * GATE_PAD])
    gg_ref[...] = 1.0 / (1.0 + jnp.exp(-acc))

    for j in range(NSA_WIDTH // 512):
        acc = _dot(hb, w_ref[:, j * 512:(j + 1) * 512])
        for i, o in enumerate(_rotate_heads(acc, cn * NSA_QSCALE, s1n * NSA_QSCALE, s2n * NSA_QSCALE, half)):
            q_ref[:, j * 512 + i * LANES: j * 512 + (i + 1) * LANES] = o.astype(bf16)
    for br in range(3):
        acc = _dot(hb, w_ref[:, c_kv + br * 512:c_kv + (br + 1) * 512])
        for i, o in enumerate(_rotate_heads(acc[:, :256], cn, s1n, s2n, half)):
            kv_ref[:, br * 512 + i * LANES: br * 512 + (i + 1) * LANES] = o.astype(bf16)
        kv_ref[:, br * 512 + 256: br * 512 + 512] = acc[:, 256:].astype(bf16)
    acc = _dot(hb, w_ref[:, c_rqk:c_rqk + 2 * RET_QK_WIDTH])
    rot = _rotate_heads(acc, cr, s1r, s2r, halfr)
    for i in range(2):
        rq_ref[:, i * LANES:(i + 1) * LANES] = rot[i].astype(bf16)
        rk_ref[:, i * LANES:(i + 1) * LANES] = (rot[2 + i] * (RET_QK_DIM ** -0.5)).astype(bf16)

    rv_ref[...] = _dot(hb, w_ref[:, c_rv:c_rv + RET_WIDTH]).astype(bf16)


def _in_proj(x2, pos2, norm_pre, w_r, inv):
    bt = x2.shape[0]
    tm = ROW_TILE
    win = w_r.shape[1]
    row = lambda i: (i, 0)
    const = lambda i: (0, 0)
    widths = (NSA_WIDTH, NSA_KV_ALL, RET_QK_WIDTH, RET_QK_WIDTH, RET_WIDTH, MEM_WIDTH, MIX_WIDTH)
    out_shape = [jax.ShapeDtypeStruct((bt, w), bf16) for w in widths]
    out_shape.append(jax.ShapeDtypeStruct((bt, 2 * GATE_PAD), f32))
    out_specs = [pl.BlockSpec((tm, w), row) for w in widths] + [pl.BlockSpec((tm, 2 * GATE_PAD), row)]
    return pl.pallas_call(
        _in_proj_kernel,
        grid=(bt // tm,),
        in_specs=[
            pl.BlockSpec((tm, D_MODEL), row),
            pl.BlockSpec((tm, 1), row),
            pl.BlockSpec((1, D_MODEL), const),
            pl.BlockSpec((D_MODEL, win), const),
            pl.BlockSpec((1, LANES), const),
        ],
        out_specs=out_specs,
        out_shape=out_shape,
        compiler_params=pltpu.CompilerParams(
            dimension_semantics=("arbitrary",), vmem_limit_bytes=VMEM_LIMIT),
        name="in_proj",
    )(x2, pos2, norm_pre, w_r, inv)


def _mem_kv_kernel(m_ref, g_ref, w_ref, o_ref):
    hb = _rms(m_ref[...], g_ref[...]).astype(bf16)
    o_ref[...] = _dot(hb, w_ref[...]).astype(bf16)


def _mem_kv(mem2, mem_norm, w_kv):
    n = mem2.shape[0]
    tm = MEM_LEN
    return pl.pallas_call(
        _mem_kv_kernel,
        grid=(n // tm,),
        in_specs=[
            pl.BlockSpec((tm, D_MODEL), lambda i: (i, 0)),
            pl.BlockSpec((1, D_MODEL), lambda i: (0, 0)),
            pl.BlockSpec((D_MODEL, 2 * MEM_WIDTH), lambda i: (0, 0)),
        ],
        out_specs=pl.BlockSpec((tm, 2 * MEM_WIDTH), lambda i: (i, 0)),
        out_shape=jax.ShapeDtypeStruct((n, 2 * MEM_WIDTH), bf16),
        compiler_params=pltpu.CompilerParams(
            dimension_semantics=("arbitrary",), vmem_limit_bytes=VMEM_LIMIT),
        name="mem_kv",
    )(mem2, mem_norm, w_kv)


def _compress_kernel(kv_ref, pos_ref, w1_ref, w2_ref, o_ref, xf_ref):
    nb, _, per, _ = o_ref.shape
    nslot = nb * per
    for i in range(nslot):
        xf_ref[i * CMP_PITCH:i * CMP_PITCH + CMP_STRIDE, :] = (
            kv_ref[i * CMP_STRIDE:(i + 1) * CMP_STRIDE, :].astype(f32))
    pos = pos_ref[0]
    la, lb = [], []
    for p in range(CMP_STRIDE):
        a = xf_ref[pl.ds(p, nslot, stride=CMP_PITCH), :]
        la.append((a + pos[p:p + 1, :]).astype(bf16))
        lb.append((a + pos[CMP_STRIDE + p:CMP_STRIDE + p + 1, :]).astype(bf16))
    khalf = CMP_STRIDE * NSA_HEAD_DIM
    ha = _dot(jnp.concatenate(la, axis=1), w1_ref[0, 0:khalf, :])
    hb = _dot(jnp.concatenate(lb, axis=1), w1_ref[0, khalf:2 * khalf, :])
    h = ha + pltpu.roll(hb, nslot - 1, 0)
    h = jax.nn.gelu(h)
    out = _dot(h.astype(bf16), w2_ref[0])
    slot = lax.broadcasted_iota(jnp.int32, out.shape, 0) & (per - 1)
    out = jnp.where(slot < per - 1, out, 0.0).astype(bf16)
    for b in range(nb):
        o_ref[b, 0] = out[b * per:(b + 1) * per]


def _compress(kv, cmp_pos, cmp_w1, cmp_w2, batch, seq):
    nslot = seq // CMP_STRIDE
    assert nslot & (nslot - 1) == 0
    nb = math.gcd(batch, CMP_BATCH)
    kinds = 2 * NSA_KV_GROUPS
    return pl.pallas_call(
        _compress_kernel,
        grid=(kinds, batch // nb),
        in_specs=[
            pl.BlockSpec((nb * seq, NSA_HEAD_DIM), lambda j, b: (b, j)),
            pl.BlockSpec((1, CMP_BLOCK, NSA_HEAD_DIM), lambda j, b: (j // NSA_KV_GROUPS, 0, 0)),
            pl.BlockSpec((1, CMP_BLOCK * NSA_HEAD_DIM, CMP_HIDDEN), lambda j, b: (j // NSA_KV_GROUPS, 0, 0)),
            pl.BlockSpec((1, CMP_HIDDEN, NSA_HEAD_DIM), lambda j, b: (j // NSA_KV_GROUPS, 0, 0)),
        ],
        out_specs=pl.BlockSpec((nb, 1, nslot, NSA_HEAD_DIM), lambda j, b: (b, j, 0, 0)),
        out_shape=jax.ShapeDtypeStruct((batch, kinds, nslot, NSA_HEAD_DIM), bf16),
        scratch_shapes=[pltpu.VMEM((nb * nslot * CMP_PITCH, NSA_HEAD_DIM), f32)],
        compiler_params=pltpu.CompilerParams(
            dimension_semantics=("arbitrary", "arbitrary"), vmem_limit_bytes=VMEM_LIMIT),
        name="compress",
    )(kv, cmp_pos, cmp_w1, cmp_w2)


def _nsa_tile(n_static, q_ref, kc_ref, vc_ref, ks_ref, vs_ref, kw_ref, vw_ref, g_ref, ovl_ref, exp_ref,
              o_ref):
    qi = pl.program_id(2)
    t0 = qi * Q_BLOCK
    nslot = kc_ref.shape[2]
    n_cmp = nslot - 1
    heads = range(NSA_REP)
    hcols = lambda a, r: a[:, r * Q_BLOCK:(r + 1) * Q_BLOCK]

    q = q_ref[...]
    q4 = jnp.concatenate([q[:, r * NSA_HEAD_DIM:(r + 1) * NSA_HEAD_DIM] for r in heads], axis=0)
    tq = t0 + lax.broadcasted_iota(jnp.int32, (1, Q_BLOCK), 1)

    def softmax_cols(s_r):
        m = _col_reduce(jnp.maximum, s_r)
        e = jnp.exp2(s_r - m)
        return e, _col_reduce(jnp.add, e)

    s = _dot_nt(kc_ref[0, 0], q4)
    wlen = WINDOW + Q_BLOCK
    start = pl.multiple_of(jnp.maximum(t0 - WINDOW, 0), Q_BLOCK)
    sw = _dot_nt(kw_ref[pl.ds(start, wlen), :], q4)
    sc0 = _dot_nt(ks_ref[0:SLC_CHUNK, :], q4)
    nrow = lax.broadcasted_iota(jnp.int32, (nslot, 1), 0)
    bias_c = jnp.where((nrow * CMP_STRIDE + (CMP_BLOCK - 1) <= tq) & (nrow < n_cmp), 0.0, NEG)
    row_ok = tq >= CMP_BLOCK - 1
    ps, psum = [], None
    for r in heads:
        e, l = softmax_cols(hcols(s, r) + bias_c)
        p = e * jnp.where(row_ok, 1.0 / l, 0.0)
        ps.append(p.astype(bf16))
        psum = p if psum is None else psum + p
    o_cmp = _dot_tn(vc_ref[0, 0], jnp.concatenate(ps, axis=1))

    imp_t = jnp.dot(ovl_ref[...], psum, precision=lax.Precision.HIGHEST,
                    preferred_element_type=f32)
    n_sb = imp_t.shape[0]
    jblk = lax.broadcasted_iota(jnp.int32, (n_sb, Q_BLOCK), 0)
    tl = t0 + lax.broadcasted_iota(jnp.int32, (n_sb, Q_BLOCK), 1)
    cur = tl >> 6
    forced = (jblk == 0) | (jblk == cur) | (jblk == cur - 1)
    valid = jblk * SEL_BLOCK <= tl
    v = jnp.where(forced, jnp.inf, jnp.where(valid, imp_t, -jnp.inf))
    n_live = min(n_sb, n_static * SLC_CHUNK // SEL_BLOCK)
    slabs = [v[k:k + SUBLANES] for k in range(0, n_live, SUBLANES)]
    jl = lax.broadcasted_iota(jnp.int32, (SUBLANES, Q_BLOCK), 0)
    cnts = [jnp.zeros((SUBLANES, Q_BLOCK), f32) for _ in slabs]
    for i in range(n_live):
        vi = jnp.broadcast_to(v[i:i + 1, :], (SUBLANES, Q_BLOCK))
        for k, vk in enumerate(slabs):
            if k * SUBLANES > i:
                ahead = vi >= vk
            elif (k + 1) * SUBLANES <= i:
                ahead = vi > vk
            else:
                ahead = (vi > vk) | ((vi == vk) & (jl > i - k * SUBLANES))
            cnts[k] = cnts[k] + jnp.where(ahead, 1.0, 0.0)
    selbias_pad = jnp.concatenate([jnp.where(c < float(N_SELECT), 0.0, NEG) for c in cnts]
                                  + [jnp.zeros((LANES - n_live, Q_BLOCK), f32)], axis=0)
    q4_aug = jnp.concatenate([q4, jnp.concatenate([selbias_pad.T.astype(bf16)] * NSA_REP, axis=0)], axis=1)

    def gate_rows(br):
        return jnp.concatenate([gs[br * NSA_REP + r:br * NSA_REP + r + 1, :] for r in heads], axis=1)

    if n_static * SLC_CHUNK > WINDOW:
        kl = lax.broadcasted_iota(jnp.int32, (Q_BLOCK, Q_BLOCK), 0)
        ql = lax.broadcasted_iota(jnp.int32, (Q_BLOCK, Q_BLOCK), 1)
        head_bias = jnp.where(kl > ql, 0.0, NEG)
        tail_bias = jnp.where(kl <= ql, 0.0, NEG)
        add_bias_w = lambda a: jnp.concatenate(
            [a[:Q_BLOCK] + head_bias, a[Q_BLOCK:WINDOW], a[WINDOW:] + tail_bias], axis=0)
    else:
        rel = tq - (start + lax.broadcasted_iota(jnp.int32, (wlen, 1), 0))
        bias_w = jnp.where((rel >= 0) & (rel < WINDOW), 0.0, NEG)
        add_bias_w = lambda a: a + bias_w
    gs = g_ref[...].T
    es, ls = [], []
    for r in heads:
        e, l = softmax_cols(add_bias_w(hcols(sw, r)))
        es.append(e.astype(bf16))
        ls.append(l)
    o_win = _dot_tn(vw_ref[pl.ds(start, wlen), :], jnp.concatenate(es, axis=1))
    o_part = (gate_rows(0) * o_cmp
              + (gate_rows(2) * (1.0 / jnp.concatenate(ls, axis=1))) * o_win)

    krow = lax.broadcasted_iota(jnp.int32, (SLC_CHUNK, 1), 0)
    seq = ks_ref.shape[0]

    def slc_branch(n):
        use_sel = n * SLC_CHUNK > N_SELECT * SEL_BLOCK

        def slc_scores(c):
            ks = ks_ref[c * SLC_CHUNK:(c + 1) * SLC_CHUNK, :]
            if not use_sel:
                return _dot_nt(ks, q4)
            return _dot_nt(jnp.concatenate([ks, exp_ref[c * SLC_CHUNK:(c + 1) * SLC_CHUNK, :]], axis=1), q4_aug)

        m_i, l_i, acc = [None] * NSA_REP, [None] * NSA_REP, None
        sc = sc0
        for c in range(n):
            bias = None
            if use_sel and c == 0:
                bias = _dot(exp_ref[0:SLC_CHUNK, :], selbias_pad.astype(bf16))
            sc_next = slc_scores(c + 1) if c + 1 < n else None
            if c == n - 1:
                causal = jnp.where(c * SLC_CHUNK + krow <= tq, 0.0, NEG)
                bias = causal if bias is None else bias + causal
            es, alphas = [], []
            for r in heads:
                s_r = hcols(sc, r) if bias is None else hcols(sc, r) + bias
                m_c = _col_reduce(jnp.maximum, s_r)
                if c == 0:
                    e = jnp.exp2(s_r - m_c)
                    m_i[r], l_i[r] = m_c, _col_reduce(jnp.add, e)
                else:
                    m_new = jnp.maximum(m_i[r], m_c)
                    alpha = jnp.exp2(m_i[r] - m_new)
                    e = jnp.exp2(s_r - m_new)
                    m_i[r], l_i[r] = m_new, alpha * l_i[r] + _col_reduce(jnp.add, e)
                    alphas.append(alpha)
                es.append(e.astype(bf16))
            pv = _dot_tn(vs_ref[c * SLC_CHUNK:(c + 1) * SLC_CHUNK, :], jnp.concatenate(es, axis=1))
            acc = pv if c == 0 else jnp.concatenate(alphas, axis=1) * acc + pv
            sc = sc_next
        o = o_part + (gate_rows(1) * (1.0 / jnp.concatenate(l_i, axis=1))) * acc
        for r in heads:
            o_ref[:, r * NSA_HEAD_DIM:(r + 1) * NSA_HEAD_DIM] = hcols(o, r).T.astype(bf16)

    slc_branch(n_static)


def _nsa_kernel(*refs):
    seq = refs[3].shape[0]
    n_chunks = (pl.program_id(2) * Q_BLOCK + Q_BLOCK + SLC_CHUNK - 1) // SLC_CHUNK
    for n in range(1, seq // SLC_CHUNK + 1):
        pl.when(n_chunks == n)(functools.partial(_nsa_tile, n, *refs))


def _nsa(q, cmp, kv, gsig, ovl_t, expand, batch, seq):
    nq = seq // Q_BLOCK
    gw = NSA_REP * NSA_HEAD_DIM
    nslot = cmp.shape[2]
    G = NSA_KV_GROUPS
    kvspec = lambda off: pl.BlockSpec((seq, NSA_HEAD_DIM), lambda b, g, i, off=off: (b, off + g))
    return pl.pallas_call(
        _nsa_kernel,
        grid=(batch, G, nq),
        in_specs=[
            pl.BlockSpec((Q_BLOCK, gw), lambda b, g, i: (b * nq + i, g)),
            pl.BlockSpec((1, 1, nslot, NSA_HEAD_DIM), lambda b, g, i: (b, g, 0, 0)),
            pl.BlockSpec((1, 1, nslot, NSA_HEAD_DIM), lambda b, g, i: (b, G + g, 0, 0)),
            kvspec(2 * G), kvspec(3 * G), kvspec(4 * G), kvspec(5 * G),
            pl.BlockSpec((Q_BLOCK, GATE_PAD), lambda b, g, i: (b * nq + i, g)),
            pl.BlockSpec(ovl_t.shape, lambda b, g, i: (0, 0)),
            pl.BlockSpec(expand.shape, lambda b, g, i: (0, 0)),
        ],
        out_specs=pl.BlockSpec((Q_BLOCK, gw), lambda b, g, i: (b * nq + i, g)),
        out_shape=jax.ShapeDtypeStruct((batch * seq, NSA_WIDTH), bf16),
        compiler_params=pltpu.CompilerParams(
            dimension_semantics=("arbitrary", "arbitrary", "arbitrary"), vmem_limit_bytes=VMEM_LIMIT),
        name="nsa",
    )(q, cmp, cmp, kv, kv, kv, kv, gsig, ovl_t, expand)


def _retention_kernel(q_ref, k_ref, v_ref, gn_ref, o_ref):
    seq = q_ref.shape[0]
    C = RET_CHUNK
    n = lax.broadcasted_iota(jnp.int32, (C, C), 0)
    mcol = lax.broadcasted_iota(jnp.int32, (C, C), 1)
    diff = (n - mcol).astype(f32)
    nvec = lax.broadcasted_iota(jnp.int32, (C, 1), 0).astype(f32)
    decay, xi, zeta, gamma_c = [], [], [], []
    for h in range(RET_HEADS):
        lg = math.log1p(-(2.0 ** (-5.0 - h)))
        decay.append(jnp.where(diff >= 0, jnp.exp(lg * jnp.maximum(diff, 0.0)), 0.0))
        xi.append(jnp.exp(lg * (nvec + 1.0)))
        zeta.append(jnp.exp(lg * (C - 1.0 - nvec)))
        gamma_c.append(math.exp(lg * C))
    gn = gn_ref[...]

    def body(c, states):
        base = pl.multiple_of(c * C, C)
        hs = range(RET_HEADS)
        qc = [q_ref[pl.ds(base, C), h * RET_QK_DIM:(h + 1) * RET_QK_DIM] for h in hs]
        kc = [k_ref[pl.ds(base, C), h * RET_QK_DIM:(h + 1) * RET_QK_DIM] for h in hs]
        vc = [v_ref[pl.ds(base, C), h * RET_V_DIM:(h + 1) * RET_V_DIM] for h in hs]
        qk = [_dot_nt(qc[h], kc[h]) for h in hs]
        cross = [_dot(qc[h], states[h].astype(bf16)) for h in hs]
        kv = [_dot_tn((kc[h].astype(f32) * zeta[h]).astype(bf16), vc[h]) for h in hs]
        intra = [_dot((qk[h] * decay[h]).astype(bf16), vc[h]) for h in hs]
        for h in hs:
            out = intra[h] + cross[h] * xi[h]
            mu = jnp.mean(out, axis=-1, keepdims=True)
            d = out - mu
            var = jnp.mean(d * d, axis=-1, keepdims=True)
            o = d * lax.rsqrt(var + EPS) * gn[:, h * RET_V_DIM:(h + 1) * RET_V_DIM]
            o_ref[pl.ds(base, C), h * RET_V_DIM:(h + 1) * RET_V_DIM] = o.astype(bf16)
        return tuple(states[h] * gamma_c[h] + kv[h] for h in hs)

    s0 = tuple(jnp.zeros((RET_QK_DIM, RET_V_DIM), f32) for _ in range(RET_HEADS))
    lax.fori_loop(0, seq // C, body, s0, unroll=2)


def _retention(rq, rk, rv, ret_gn, batch, seq):
    return pl.pallas_call(
        _retention_kernel,
        grid=(batch,),
        in_specs=[
            pl.BlockSpec((seq, RET_QK_WIDTH), lambda b: (b, 0)),
            pl.BlockSpec((seq, RET_QK_WIDTH), lambda b: (b, 0)),
            pl.BlockSpec((seq, RET_WIDTH), lambda b: (b, 0)),
            pl.BlockSpec((1, RET_WIDTH), lambda b: (0, 0)),
        ],
        out_specs=pl.BlockSpec((seq, RET_WIDTH), lambda b: (b, 0)),
        out_shape=jax.ShapeDtypeStruct((batch * seq, RET_WIDTH), bf16),
        compiler_params=pltpu.CompilerParams(
            dimension_semantics=("arbitrary",), vmem_limit_bytes=VMEM_LIMIT),
        name="retention",
    )(rq, rk, rv, ret_gn)


def _out_proj_kernel(x_ref, on_ref, or_ref, mq_ref, kvm_ref, gs_ref, w_ref, g_ref, o_ref):
    off = NSA_WIDTH + RET_WIDTH
    heads = range(MEM_HEADS)
    hsl = lambda h: slice(h * MEM_HEAD_DIM, (h + 1) * MEM_HEAD_DIM)
    s = [_dot_nt(mq_ref[:, hsl(h)], kvm_ref[:, hsl(h)]) for h in heads]
    mix_a = jnp.concatenate([on_ref[...] * gs_ref[:, 0:NSA_WIDTH],
                             or_ref[...] * gs_ref[:, NSA_WIDTH:off]], axis=1)
    y = _dot(mix_a, w_ref[0:off, :])
    mix_m = []
    for h in heads:
        m = jnp.max(s[h], axis=1, keepdims=True)
        e = jnp.exp2(s[h] - m)
        l = jnp.sum(e, axis=1, keepdims=True)
        om = _dot(e.astype(bf16), kvm_ref[:, MEM_WIDTH + h * MEM_HEAD_DIM:MEM_WIDTH + (h + 1) * MEM_HEAD_DIM])
        mix_m.append((om * (1.0 / l)).astype(bf16) * gs_ref[:, off + h * MEM_HEAD_DIM:off + (h + 1) * MEM_HEAD_DIM])
    y = y + _dot(jnp.concatenate(mix_m, axis=1), w_ref[off:, :])
    o_ref[...] = x_ref[...] + _rms(y, g_ref[...])


def _out_proj(x2, o_nsa, o_ret, mq, kvm, gsilu, w_out, norm_post, seq):
    bt = x2.shape[0]
    tm = OUT_ROW_TILE
    per_b = seq // tm
    row = lambda i: (i, 0)
    const = lambda i: (0, 0)
    return pl.pallas_call(
        _out_proj_kernel,
        grid=(bt // tm,),
        in_specs=[
            pl.BlockSpec((tm, D_MODEL), row),
            pl.BlockSpec((tm, NSA_WIDTH), row),
            pl.BlockSpec((tm, RET_WIDTH), row),
            pl.BlockSpec((tm, MEM_WIDTH), row),
            pl.BlockSpec((MEM_LEN, 2 * MEM_WIDTH), lambda i: (i // per_b, 0)),
            pl.BlockSpec((tm, MIX_WIDTH), row),
            pl.BlockSpec((MIX_WIDTH, D_MODEL), const),
            pl.BlockSpec((1, D_MODEL), const),
        ],
        out_specs=pl.BlockSpec((tm, D_MODEL), row),
        out_shape=jax.ShapeDtypeStruct((bt, D_MODEL), f32),
        compiler_params=pltpu.CompilerParams(
            dimension_semantics=("arbitrary",), vmem_limit_bytes=VMEM_LIMIT),
        name="out_proj",
    )(x2, o_nsa, o_ret, mq, kvm, gsilu, w_out, norm_post)


def _relayout_w_in(w):
    o = 0
    q_nsa = w[:, o:o + NSA_WIDTH]; o += NSA_WIDTH
    kv = w[:, o:o + NSA_KV_ALL]; o += NSA_KV_ALL
    gates = w[:, o:o + 3 * NSA_HEADS]; o += 3 * NSA_HEADS
    rest = w[:, o:]
    gates = gates.reshape(D_MODEL, 3, NSA_KV_GROUPS, NSA_REP).transpose(0, 2, 1, 3)
    gates = gates.reshape(D_MODEL, NSA_KV_GROUPS, 3 * NSA_REP)
    gates = jnp.pad(gates, ((0, 0), (0, 0), (0, GATE_PAD - 3 * NSA_REP)))
    gates = gates.reshape(D_MODEL, NSA_KV_GROUPS * GATE_PAD)
    return jnp.concatenate([q_nsa, kv, rest, gates], axis=1).astype(bf16)


def _overlap_t(seq):
    n_slot = seq // CMP_STRIDE
    n_sb = seq // SEL_BLOCK
    cmp_start = np.arange(n_slot) * CMP_STRIDE
    sel_start = np.arange(n_sb) * SEL_BLOCK
    ov = np.clip(np.minimum(cmp_start[None, :] + CMP_BLOCK, sel_start[:, None] + SEL_BLOCK)
                 - np.maximum(cmp_start[None, :], sel_start[:, None]), 0, None)
    return jnp.asarray(ov.astype(np.float32) / CMP_BLOCK)


def kernel(x, mem, positions, norm_pre, w_in, cmp_pos_k, cmp_w1_k, cmp_w2_k, cmp_pos_v, cmp_w1_v,
           cmp_w2_v, ret_gn, mem_norm, w_mem_kv, w_out, norm_post):
    depth = norm_pre.shape[0]
    batch, seq, _ = x.shape
    assert x.shape[2] == D_MODEL and mem.shape[1:] == (MEM_LEN, D_MODEL)
    assert seq % SLC_CHUNK == 0 and seq >= WINDOW + Q_BLOCK

    half = ROT_DIM // 2
    inv_n = ROPE_THETA ** (-jnp.arange(half, dtype=f32) / half)
    halfr = RET_QK_DIM // 2
    inv_r = RET_THETA ** (-jnp.arange(halfr, dtype=f32) / halfr)
    assert ROT_DIM <= RET_QK_DIM and 2 * RET_QK_DIM == LANES
    inv = jnp.concatenate([inv_n, inv_n, jnp.zeros((RET_QK_DIM - ROT_DIM,), f32), inv_r, inv_r])[None, :]
    ovl_t = _overlap_t(seq)
    expand = jnp.asarray(np.arange(seq)[:, None] // SEL_BLOCK == np.arange(LANES)[None, :], bf16)
    pos2 = positions.reshape(batch * seq, 1)
    mem2 = mem.reshape(batch * MEM_LEN, D_MODEL)

    x2 = x.reshape(batch * seq, D_MODEL)
    for layer in range(depth):
        w_r = _relayout_w_in(w_in[layer])
        q, kv, rq, rk, rv, mq, gsilu, gsig = _in_proj(x2, pos2, norm_pre[layer][None, :], w_r, inv)
        kvm = _mem_kv(mem2, mem_norm[layer][None, :], w_mem_kv[layer].astype(bf16))
        cmp = _compress(
            kv,
            jnp.stack([cmp_pos_k[layer], cmp_pos_v[layer]]),
            jnp.stack([cmp_w1_k[layer], cmp_w1_v[layer]]).astype(bf16),
            jnp.stack([cmp_w2_k[layer], cmp_w2_v[layer]]).astype(bf16),
            batch, seq)
        o_nsa = _nsa(q, cmp, kv, gsig, ovl_t, expand, batch, seq)
        o_ret = _retention(rq, rk, rv, ret_gn[layer][None, :], batch, seq)
        x2 = _out_proj(x2, o_nsa, o_ret, mq, kvm, gsilu, w_out[layer].astype(bf16),
                       norm_post[layer][None, :], seq)
    return x2.reshape(batch, seq, D_MODEL)
```

```python
import functools
import math

import jax
import jax.numpy as jnp
import numpy as np
from jax import lax
from jax.experimental import pallas as pl
from jax.experimental.pallas import tpu as pltpu

D_MODEL = 1024
MEM_LEN = 256
EPS = 1e-6

NSA_HEADS = 8
NSA_KV_GROUPS = 2
NSA_REP = NSA_HEADS // NSA_KV_GROUPS
NSA_HEAD_DIM = 128
CMP_BLOCK = 32
CMP_STRIDE = 16
CMP_HIDDEN = 256
SEL_BLOCK = 64
N_SELECT = 16
WINDOW = 512
Q_BLOCK = 256
ROPE_THETA = 500000.0
ROT_DIM = NSA_HEAD_DIM // 4

RET_HEADS = 4
RET_QK_DIM = 64
RET_V_DIM = 128
RET_CHUNK = 128
RET_THETA = 10000.0

MEM_HEADS = 4
MEM_HEAD_DIM = 128

NSA_WIDTH = NSA_HEADS * NSA_HEAD_DIM
NSA_KV_ALL = 3 * 2 * NSA_KV_GROUPS * NSA_HEAD_DIM
RET_QK_WIDTH = RET_HEADS * RET_QK_DIM
RET_WIDTH = RET_HEADS * RET_V_DIM
MEM_WIDTH = MEM_HEADS * MEM_HEAD_DIM
MIX_WIDTH = NSA_WIDTH + RET_WIDTH + MEM_WIDTH
GATE_PAD = 128

LANES = 128
SUBLANES = 8
NEG = -1e30
LOG2E = math.log2(math.e)
NSA_QSCALE = NSA_HEAD_DIM ** -0.5 * LOG2E
MEM_QSCALE = MEM_HEAD_DIM ** -0.5 * LOG2E
VMEM_LIMIT = 56 * 1024 * 1024

ROW_TILE = 256
CMP_PITCH = 24
CMP_BATCH = 4
OUT_ROW_TILE = 512
SLC_CHUNK = 512

_NT = (((1,), (1,)), ((), ()))
_TN = (((0,), (0,)), ((), ()))

bf16 = jnp.bfloat16
f32 = jnp.float32


def _dot(a, b):
    return jnp.dot(a, b, preferred_element_type=f32)


def _dot_nt(a, b):
    return lax.dot_general(a, b, _NT, preferred_element_type=f32)


def _dot_tn(a, b):
    return lax.dot_general(a, b, _TN, preferred_element_type=f32)


def _col_reduce(op, x):
    slabs = [x[i:i + SUBLANES] for i in range(0, x.shape[0], SUBLANES)]
    while len(slabs) > 1:
        nxt = [op(slabs[i], slabs[i + 1]) for i in range(0, len(slabs) - 1, 2)]
        if len(slabs) % 2:
            nxt.append(slabs[-1])
        slabs = nxt
    red = jnp.max if op is jnp.maximum else jnp.sum
    return red(slabs[0], axis=0, keepdims=True)


def _rms(x, g):
    return x * lax.rsqrt(jnp.mean(x * x, axis=-1, keepdims=True) + EPS) * g


def _rotate_heads(acc, c, s1, s2, shift):
    outs = []
    for j in range(acc.shape[1] // LANES):
        a = acc[:, j * LANES:(j + 1) * LANES]
        outs.append(a * c + pltpu.roll(a, shift, 1) * s1 + pltpu.roll(a, LANES - shift, 1) * s2)
    return outs


def _in_proj_kernel(x_ref, pos_ref, g_ref, w_ref, inv_ref,
                    q_ref, kv_ref, rq_ref, rk_ref, rv_ref, mq_ref, gs_ref, gg_ref):
    x = x_ref[...]
    hb = _rms(x, g_ref[...]).astype(bf16)
    c_kv = NSA_WIDTH
    c_rqk = c_kv + NSA_KV_ALL
    c_rv = c_rqk + 2 * RET_QK_WIDTH
    c_mq = c_rv + RET_WIDTH
    c_gate = c_mq + MEM_WIDTH
    c_gg = c_gate + MIX_WIDTH

    pos = pos_ref[...].astype(f32)
    lane = lax.broadcasted_iota(jnp.int32, (x.shape[0], LANES), 1)
    ang = pos * inv_ref[...]
    cs, sn = jnp.cos(ang), jnp.sin(ang)
    half = ROT_DIM // 2
    cn = jnp.where(lane < ROT_DIM, cs, 1.0)
    s1n = jnp.where((lane >= half) & (lane < ROT_DIM), sn, 0.0)
    s2n = jnp.where(lane < half, -sn, 0.0)
    halfr = RET_QK_DIM // 2
    cr = jnp.where(lane >= RET_QK_DIM, cs, pltpu.roll(cs, RET_QK_DIM, 1))
    sr = jnp.where(lane >= RET_QK_DIM, sn, pltpu.roll(sn, RET_QK_DIM, 1))
    lr = lane & (RET_QK_DIM - 1)
    s1r = jnp.where(lr >= halfr, sr, 0.0)
    s2r = jnp.where(lr < halfr, -sr, 0.0)

    for j in range(MIX_WIDTH // 512):
        acc = _dot(hb, w_ref[:, c_gate + j * 512:c_gate + (j + 1) * 512])
        gs_ref[:, j * 512:(j + 1) * 512] = (acc * (1.0 / (1.0 + jnp.exp(-acc)))).astype(bf16)
    mq_ref[...] = (_dot(hb, w_ref[:, c_mq:c_mq + MEM_WIDTH]) * MEM_QSCALE).astype(bf16)
    acc = _dot(hb, w_ref[:, c_gg:c_gg + 2 * GATE_PAD])
    gg_ref[...] = 1.0 / (1.0 + jnp.exp(-acc))

    for j in range(NSA_WIDTH // 512):
        acc = _dot(hb, w_ref[:, j * 512:(j + 1) * 512])
        for i, o in enumerate(_rotate_heads(acc, cn * NSA_QSCALE, s1n * NSA_QSCALE, s2n * NSA_QSCALE, half)):
            q_ref[:, j * 512 + i * LANES: j * 512 + (i + 1) * LANES] = o.astype(bf16)
    for br in range(3):
        acc = _dot(hb, w_ref[:, c_kv + br * 512:c_kv + (br + 1) * 512])
        for i, o in enumerate(_rotate_heads(acc[:, :256], cn, s1n, s2n, half)):
            kv_ref[:, br * 512 + i * LANES: br * 512 + (i + 1) * LANES] = o.astype(bf16)
        kv_ref[:, br * 512 + 256: br * 512 + 512] = acc[:, 256:].astype(bf16)
    acc = _dot(hb, w_ref[:, c_rqk:c_rqk + 2 * RET_QK_WIDTH])
    rot = _rotate_heads(acc, cr, s1r, s2r, halfr)
    for i in range(2):
        rq_ref[:, i * LANES:(i + 1) * LANES] = rot[i].astype(bf16)
        rk_ref[:, i * LANES:(i + 1) * LANES] = (rot[2 + i] * (RET_QK_DIM ** -0.5)).astype(bf16)

    rv_ref[...] = _dot(hb, w_ref[:, c_rv:c_rv + RET_WIDTH]).astype(bf16)


def _in_proj(x2, pos2, norm_pre, w_r, inv):
    bt = x2.shape[0]
    tm = ROW_TILE
    win = w_r.shape[1]
    row = lambda i: (i, 0)
    const = lambda i: (0, 0)
    widths = (NSA_WIDTH, NSA_KV_ALL, RET_QK_WIDTH, RET_QK_WIDTH, RET_WIDTH, MEM_WIDTH, MIX_WIDTH)
    out_shape = [jax.ShapeDtypeStruct((bt, w), bf16) for w in widths]
    out_shape.append(jax.ShapeDtypeStruct((bt, 2 * GATE_PAD), f32))
    out_specs = [pl.BlockSpec((tm, w), row) for w in widths] + [pl.BlockSpec((tm, 2 * GATE_PAD), row)]
    return pl.pallas_call(
        _in_proj_kernel,
        grid=(bt // tm,),
        in_specs=[
            pl.BlockSpec((tm, D_MODEL), row),
            pl.BlockSpec((tm, 1), row),
            pl.BlockSpec((1, D_MODEL), const),
            pl.BlockSpec((D_MODEL, win), const),
            pl.BlockSpec((1, LANES), const),
        ],
        out_specs=out_specs,
        out_shape=out_shape,
        compiler_params=pltpu.CompilerParams(
            dimension_semantics=("arbitrary",), vmem_limit_bytes=VMEM_LIMIT),
        name="in_proj",
    )(x2, pos2, norm_pre, w_r, inv)


def _mem_kv_kernel(m_ref, g_ref, w_ref, o_ref):
    hb = _rms(m_ref[...], g_ref[...]).astype(bf16)
    o_ref[...] = _dot(hb, w_ref[...]).astype(bf16)


def _mem_kv(mem2, mem_norm, w_kv):
    n = mem2.shape[0]
    tm = MEM_LEN
    return pl.pallas_call(
        _mem_kv_kernel,
        grid=(n // tm,),
        in_specs=[
            pl.BlockSpec((tm, D_MODEL), lambda i: (i, 0)),
            pl.BlockSpec((1, D_MODEL), lambda i: (0, 0)),
            pl.BlockSpec((D_MODEL, 2 * MEM_WIDTH), lambda i: (0, 0)),
        ],
        out_specs=pl.BlockSpec((tm, 2 * MEM_WIDTH), lambda i: (i, 0)),
        out_shape=jax.ShapeDtypeStruct((n, 2 * MEM_WIDTH), bf16),
        compiler_params=pltpu.CompilerParams(
            dimension_semantics=("arbitrary",), vmem_limit_bytes=VMEM_LIMIT),
        name="mem_kv",
    )(mem2, mem_norm, w_kv)


def _compress_kernel(kv_ref, pos_ref, w1_ref, w2_ref, o_ref, xf_ref):
    nb, _, per, _ = o_ref.shape
    nslot = nb * per
    for i in range(nslot):
        xf_ref[i * CMP_PITCH:i * CMP_PITCH + CMP_STRIDE, :] = (
            kv_ref[i * CMP_STRIDE:(i + 1) * CMP_STRIDE, :].astype(f32))
    pos = pos_ref[0]
    la, lb = [], []
    for p in range(CMP_STRIDE):
        a = xf_ref[pl.ds(p, nslot, stride=CMP_PITCH), :]
        la.append((a + pos[p:p + 1, :]).astype(bf16))
        lb.append((a + pos[CMP_STRIDE + p:CMP_STRIDE + p + 1, :]).astype(bf16))
    khalf = CMP_STRIDE * NSA_HEAD_DIM
    ha = _dot(jnp.concatenate(la, axis=1), w1_ref[0, 0:khalf, :])
    hb = _dot(jnp.concatenate(lb, axis=1), w1_ref[0, khalf:2 * khalf, :])
    h = ha + pltpu.roll(hb, nslot - 1, 0)
    h = jax.nn.gelu(h)
    out = _dot(h.astype(bf16), w2_ref[0])
    slot = lax.broadcasted_iota(jnp.int32, out.shape, 0) & (per - 1)
    out = jnp.where(slot < per - 1, out, 0.0).astype(bf16)
    for b in range(nb):
        o_ref[b, 0] = out[b * per:(b + 1) * per]


def _compress(kv, cmp_pos, cmp_w1, cmp_w2, batch, seq):
    nslot = seq // CMP_STRIDE
    assert nslot & (nslot - 1) == 0
    nb = math.gcd(batch, CMP_BATCH)
    kinds = 2 * NSA_KV_GROUPS
    return pl.pallas_call(
        _compress_kernel,
        grid=(kinds, batch // nb),
        in_specs=[
            pl.BlockSpec((nb * seq, NSA_HEAD_DIM), lambda j, b: (b, j)),
            pl.BlockSpec((1, CMP_BLOCK, NSA_HEAD_DIM), lambda j, b: (j // NSA_KV_GROUPS, 0, 0)),
            pl.BlockSpec((1, CMP_BLOCK * NSA_HEAD_DIM, CMP_HIDDEN), lambda j, b: (j // NSA_KV_GROUPS, 0, 0)),
            pl.BlockSpec((1, CMP_HIDDEN, NSA_HEAD_DIM), lambda j, b: (j // NSA_KV_GROUPS, 0, 0)),
        ],
        out_specs=pl.BlockSpec((nb, 1, nslot, NSA_HEAD_DIM), lambda j, b: (b, j, 0, 0)),
        out_shape=jax.ShapeDtypeStruct((batch, kinds, nslot, NSA_HEAD_DIM), bf16),
        scratch_shapes=[pltpu.VMEM((nb * nslot * CMP_PITCH, NSA_HEAD_DIM), f32)],
        compiler_params=pltpu.CompilerParams(
            dimension_semantics=("arbitrary", "arbitrary"), vmem_limit_bytes=VMEM_LIMIT),
        name="compress",
    )(kv, cmp_pos, cmp_w1, cmp_w2)


def _nsa_tile(t0, q_ref, kc_ref, vc_ref, ks_ref, vs_ref, kw_ref, vw_ref, g_ref, ovl_ref, exp_ref, o_ref):
    n_static = (t0 + Q_BLOCK + SLC_CHUNK - 1) // SLC_CHUNK
    nslot = kc_ref.shape[2]
    n_cmp = nslot - 1
    heads = range(NSA_REP)
    hcols = lambda a, r: a[:, r * Q_BLOCK:(r + 1) * Q_BLOCK]

    q = q_ref[...]
    q4 = jnp.concatenate([q[:, r * NSA_HEAD_DIM:(r + 1) * NSA_HEAD_DIM] for r in heads], axis=0)
    tq = t0 + lax.broadcasted_iota(jnp.int32, (1, Q_BLOCK), 1)

    def softmax_cols(s_r):
        m = _col_reduce(jnp.maximum, s_r)
        e = jnp.exp2(s_r - m)
        return e, _col_reduce(jnp.add, e)

    s = _dot_nt(kc_ref[0, 0], q4)
    start = max(t0 - WINDOW, 0)
    wlen = t0 + Q_BLOCK - start
    sw = _dot_nt(kw_ref[start:start + wlen, :], q4)
    sc0 = _dot_nt(ks_ref[0:min(SLC_CHUNK, t0 + Q_BLOCK), :], q4)
    nrow = lax.broadcasted_iota(jnp.int32, (nslot, 1), 0)
    bias_c = jnp.where((nrow * CMP_STRIDE + (CMP_BLOCK - 1) <= tq) & (nrow < n_cmp), 0.0, NEG)
    row_ok = tq >= CMP_BLOCK - 1
    ps, psum = [], None
    for r in heads:
        e, l = softmax_cols(hcols(s, r) + bias_c)
        p = e * jnp.where(row_ok, 1.0 / l, 0.0)
        ps.append(p.astype(bf16))
        psum = p if psum is None else psum + p
    o_cmp = _dot_tn(vc_ref[0, 0], jnp.concatenate(ps, axis=1))

    imp_t = jnp.dot(ovl_ref[...], psum, precision=lax.Precision.HIGHEST,
                    preferred_element_type=f32)
    n_sb = imp_t.shape[0]
    jblk = lax.broadcasted_iota(jnp.int32, (n_sb, Q_BLOCK), 0)
    tl = t0 + lax.broadcasted_iota(jnp.int32, (n_sb, Q_BLOCK), 1)
    cur = tl >> 6
    forced = (jblk == 0) | (jblk == cur) | (jblk == cur - 1)
    valid = jblk * SEL_BLOCK <= tl
    v = jnp.where(forced, jnp.inf, jnp.where(valid, imp_t, -jnp.inf))
    n_live = min(n_sb, n_static * SLC_CHUNK // SEL_BLOCK)
    slabs = [v[k:k + SUBLANES] for k in range(0, n_live, SUBLANES)]
    jl = lax.broadcasted_iota(jnp.int32, (SUBLANES, Q_BLOCK), 0)
    cnts = [jnp.zeros((SUBLANES, Q_BLOCK), f32) for _ in slabs]
    for i in range(n_live):
        vi = jnp.broadcast_to(v[i:i + 1, :], (SUBLANES, Q_BLOCK))
        for k, vk in enumerate(slabs):
            if k * SUBLANES > i:
                ahead = vi >= vk
            elif (k + 1) * SUBLANES <= i:
                ahead = vi > vk
            else:
                ahead = (vi > vk) | ((vi == vk) & (jl > i - k * SUBLANES))
            cnts[k] = cnts[k] + jnp.where(ahead, 1.0, 0.0)
    selbias_pad = jnp.concatenate([jnp.where(c < float(N_SELECT), 0.0, NEG) for c in cnts]
                                  + [jnp.zeros((LANES - n_live, Q_BLOCK), f32)], axis=0)
    q4_aug = jnp.concatenate([q4, jnp.concatenate([selbias_pad.T.astype(bf16)] * NSA_REP, axis=0)], axis=1)

    def gate_rows(br):
        return jnp.concatenate([gs[br * NSA_REP + r:br * NSA_REP + r + 1, :] for r in heads], axis=1)

    kl = lax.broadcasted_iota(jnp.int32, (Q_BLOCK, Q_BLOCK), 0)
    ql = lax.broadcasted_iota(jnp.int32, (Q_BLOCK, Q_BLOCK), 1)
    head_bias = jnp.where(kl > ql, 0.0, NEG)
    tail_bias = jnp.where(kl <= ql, 0.0, NEG)

    def add_tail_bias(a):
        body = a.shape[0] - Q_BLOCK
        tail = a[body:] + tail_bias
        return tail if body == 0 else jnp.concatenate([a[:body], tail], axis=0)

    def add_bias_w(a):
        if t0 >= WINDOW:
            a = jnp.concatenate([a[:Q_BLOCK] + head_bias, a[Q_BLOCK:]], axis=0)
        return add_tail_bias(a)

    gs = g_ref[...].T
    es, ls = [], []
    for r in heads:
        e, l = softmax_cols(add_bias_w(hcols(sw, r)))
        es.append(e.astype(bf16))
        ls.append(l)
    o_win = _dot_tn(vw_ref[start:start + wlen, :], jnp.concatenate(es, axis=1))
    o_part = (gate_rows(0) * o_cmp
              + (gate_rows(2) * (1.0 / jnp.concatenate(ls, axis=1))) * o_win)

    def slc_branch(n):
        use_sel = t0 + Q_BLOCK > N_SELECT * SEL_BLOCK
        bounds = [(c * SLC_CHUNK, min((c + 1) * SLC_CHUNK, t0 + Q_BLOCK)) for c in range(n)]

        def slc_scores(c):
            lo, hi = bounds[c]
            ks = ks_ref[lo:hi, :]
            if not use_sel:
                return _dot_nt(ks, q4)
            return _dot_nt(jnp.concatenate([ks, exp_ref[lo:hi, :]], axis=1), q4_aug)

        m_i, l_i, acc = [None] * NSA_REP, [None] * NSA_REP, None
        sc = slc_scores(0) if use_sel else sc0
        for c in range(n):
            sc_next = slc_scores(c + 1) if c + 1 < n else None
            es, alphas = [], []
            for r in heads:
                s_r = add_tail_bias(hcols(sc, r)) if c == n - 1 else hcols(sc, r)
                m_c = _col_reduce(jnp.maximum, s_r)
                if c == 0:
                    e = jnp.exp2(s_r - m_c)
                    m_i[r], l_i[r] = m_c, _col_reduce(jnp.add, e)
                else:
                    m_new = jnp.maximum(m_i[r], m_c)
                    alpha = jnp.exp2(m_i[r] - m_new)
                    e = jnp.exp2(s_r - m_new)
                    m_i[r], l_i[r] = m_new, alpha * l_i[r] + _col_reduce(jnp.add, e)
                    alphas.append(alpha)
                es.append(e.astype(bf16))
            pv = _dot_tn(vs_ref[bounds[c][0]:bounds[c][1], :], jnp.concatenate(es, axis=1))
            acc = pv if c == 0 else jnp.concatenate(alphas, axis=1) * acc + pv
            sc = sc_next
        o = o_part + (gate_rows(1) * (1.0 / jnp.concatenate(l_i, axis=1))) * acc
        for r in heads:
            o_ref[:, r * NSA_HEAD_DIM:(r + 1) * NSA_HEAD_DIM] = hcols(o, r).T.astype(bf16)

    slc_branch(n_static)


def _nsa_kernel(*refs):
    seq = refs[3].shape[0]
    for qi in range(seq // Q_BLOCK):
        pl.when(pl.program_id(2) == qi)(functools.partial(_nsa_tile, qi * Q_BLOCK, *refs))


def _nsa(q, cmp, kv, gsig, ovl_t, expand, batch, seq):
    nq = seq // Q_BLOCK
    gw = NSA_REP * NSA_HEAD_DIM
    nslot = cmp.shape[2]
    G = NSA_KV_GROUPS
    kvspec = lambda off: pl.BlockSpec((seq, NSA_HEAD_DIM), lambda b, g, i, off=off: (b, off + g))
    return pl.pallas_call(
        _nsa_kernel,
        grid=(batch, G, nq),
        in_specs=[
            pl.BlockSpec((Q_BLOCK, gw), lambda b, g, i: (b * nq + i, g)),
            pl.BlockSpec((1, 1, nslot, NSA_HEAD_DIM), lambda b, g, i: (b, g, 0, 0)),
            pl.BlockSpec((1, 1, nslot, NSA_HEAD_DIM), lambda b, g, i: (b, G + g, 0, 0)),
            kvspec(2 * G), kvspec(3 * G), kvspec(4 * G), kvspec(5 * G),
            pl.BlockSpec((Q_BLOCK, GATE_PAD), lambda b, g, i: (b * nq + i, g)),
            pl.BlockSpec(ovl_t.shape, lambda b, g, i: (0, 0)),
            pl.BlockSpec(expand.shape, lambda b, g, i: (0, 0)),
        ],
        out_specs=pl.BlockSpec((Q_BLOCK, gw), lambda b, g, i: (b * nq + i, g)),
        out_shape=jax.ShapeDtypeStruct((batch * seq, NSA_WIDTH), bf16),
        compiler_params=pltpu.CompilerParams(
            dimension_semantics=("arbitrary", "arbitrary", "arbitrary"), vmem_limit_bytes=VMEM_LIMIT),
        name="nsa",
    )(q, cmp, cmp, kv, kv, kv, kv, gsig, ovl_t, expand)


def _retention_kernel(q_ref, k_ref, v_ref, gn_ref, o_ref):
    seq = q_ref.shape[0]
    C = RET_CHUNK
    n = lax.broadcasted_iota(jnp.int32, (C, C), 0)
    mcol = lax.broadcasted_iota(jnp.int32, (C, C), 1)
    diff = (n - mcol).astype(f32)
    nvec = lax.broadcasted_iota(jnp.int32, (C, 1), 0).astype(f32)
    decay, xi, zeta, gamma_c = [], [], [], []
    for h in range(RET_HEADS):
        lg = math.log1p(-(2.0 ** (-5.0 - h)))
        decay.append(jnp.where(diff >= 0, jnp.exp(lg * jnp.maximum(diff, 0.0)), 0.0))
        xi.append(jnp.exp(lg * (nvec + 1.0)))
        zeta.append(jnp.exp(lg * (C - 1.0 - nvec)))
        gamma_c.append(math.exp(lg * C))
    gn = gn_ref[...]

    def body(c, states):
        base = pl.multiple_of(c * C, C)
        hs = range(RET_HEADS)
        qc = [q_ref[pl.ds(base, C), h * RET_QK_DIM:(h + 1) * RET_QK_DIM] for h in hs]
        kc = [k_ref[pl.ds(base, C), h * RET_QK_DIM:(h + 1) * RET_QK_DIM] for h in hs]
        vc = [v_ref[pl.ds(base, C), h * RET_V_DIM:(h + 1) * RET_V_DIM] for h in hs]
        qk = [_dot_nt(qc[h], kc[h]) for h in hs]
        cross = [_dot(qc[h], states[h].astype(bf16)) for h in hs]
        kv = [_dot_tn((kc[h].astype(f32) * zeta[h]).astype(bf16), vc[h]) for h in hs]
        intra = [_dot((qk[h] * decay[h]).astype(bf16), vc[h]) for h in hs]
        for h in hs:
            out = intra[h] + cross[h] * xi[h]
            mu = jnp.mean(out, axis=-1, keepdims=True)
            d = out - mu
            var = jnp.mean(d * d, axis=-1, keepdims=True)
            o = d * lax.rsqrt(var + EPS) * gn[:, h * RET_V_DIM:(h + 1) * RET_V_DIM]
            o_ref[pl.ds(base, C), h * RET_V_DIM:(h + 1) * RET_V_DIM] = o.astype(bf16)
        return tuple(states[h] * gamma_c[h] + kv[h] for h in hs)

    s0 = tuple(jnp.zeros((RET_QK_DIM, RET_V_DIM), f32) for _ in range(RET_HEADS))
    lax.fori_loop(0, seq // C, body, s0, unroll=2)


def _retention(rq, rk, rv, ret_gn, batch, seq):
    return pl.pallas_call(
        _retention_kernel,
        grid=(batch,),
        in_specs=[
            pl.BlockSpec((seq, RET_QK_WIDTH), lambda b: (b, 0)),
            pl.BlockSpec((seq, RET_QK_WIDTH), lambda b: (b, 0)),
            pl.BlockSpec((seq, RET_WIDTH), lambda b: (b, 0)),
            pl.BlockSpec((1, RET_WIDTH), lambda b: (0, 0)),
        ],
        out_specs=pl.BlockSpec((seq, RET_WIDTH), lambda b: (b, 0)),
        out_shape=jax.ShapeDtypeStruct((batch * seq, RET_WIDTH), bf16),
        compiler_params=pltpu.CompilerParams(
            dimension_semantics=("arbitrary",), vmem_limit_bytes=VMEM_LIMIT),
        name="retention",
    )(rq, rk, rv, ret_gn)


def _out_proj_kernel(x_ref, on_ref, or_ref, mq_ref, kvm_ref, gs_ref, w_ref, g_ref, o_ref):
    off = NSA_WIDTH + RET_WIDTH
    heads = range(MEM_HEADS)
    hsl = lambda h: slice(h * MEM_HEAD_DIM, (h + 1) * MEM_HEAD_DIM)
    s = [_dot_nt(mq_ref[:, hsl(h)], kvm_ref[:, hsl(h)]) for h in heads]
    mix_a = jnp.concatenate([on_ref[...] * gs_ref[:, 0:NSA_WIDTH],
                             or_ref[...] * gs_ref[:, NSA_WIDTH:off]], axis=1)
    y = _dot(mix_a, w_ref[0:off, :])
    mix_m = []
    for h in heads:
        m = jnp.max(s[h], axis=1, keepdims=True)
        e = jnp.exp2(s[h] - m)
        l = jnp.sum(e, axis=1, keepdims=True)
        om = _dot(e.astype(bf16), kvm_ref[:, MEM_WIDTH + h * MEM_HEAD_DIM:MEM_WIDTH + (h + 1) * MEM_HEAD_DIM])
        mix_m.append((om * (1.0 / l)).astype(bf16) * gs_ref[:, off + h * MEM_HEAD_DIM:off + (h + 1) * MEM_HEAD_DIM])
    y = y + _dot(jnp.concatenate(mix_m, axis=1), w_ref[off:, :])
    o_ref[...] = x_ref[...] + _rms(y, g_ref[...])


def _out_proj(x2, o_nsa, o_ret, mq, kvm, gsilu, w_out, norm_post, seq):
    bt = x2.shape[0]
    tm = OUT_ROW_TILE
    per_b = seq // tm
    row = lambda i: (i, 0)
    const = lambda i: (0, 0)
    return pl.pallas_call(
        _out_proj_kernel,
        grid=(bt // tm,),
        in_specs=[
            pl.BlockSpec((tm, D_MODEL), row),
            pl.BlockSpec((tm, NSA_WIDTH), row),
            pl.BlockSpec((tm, RET_WIDTH), row),
            pl.BlockSpec((tm, MEM_WIDTH), row),
            pl.BlockSpec((MEM_LEN, 2 * MEM_WIDTH), lambda i: (i // per_b, 0)),
            pl.BlockSpec((tm, MIX_WIDTH), row),
            pl.BlockSpec((MIX_WIDTH, D_MODEL), const),
            pl.BlockSpec((1, D_MODEL), const),
        ],
        out_specs=pl.BlockSpec((tm, D_MODEL), row),
        out_shape=jax.ShapeDtypeStruct((bt, D_MODEL), f32),
        compiler_params=pltpu.CompilerParams(
            dimension_semantics=("arbitrary",), vmem_limit_bytes=VMEM_LIMIT),
        name="out_proj",
    )(x2, o_nsa, o_ret, mq, kvm, gsilu, w_out, norm_post)


def _relayout_w_in(w):
    o = 0
    q_nsa = w[:, o:o + NSA_WIDTH]; o += NSA_WIDTH
    kv = w[:, o:o + NSA_KV_ALL]; o += NSA_KV_ALL
    gates = w[:, o:o + 3 * NSA_HEADS]; o += 3 * NSA_HEADS
    rest = w[:, o:]
    gates = gates.reshape(D_MODEL, 3, NSA_KV_GROUPS, NSA_REP).transpose(0, 2, 1, 3)
    gates = gates.reshape(D_MODEL, NSA_KV_GROUPS, 3 * NSA_REP)
    gates = jnp.pad(gates, ((0, 0), (0, 0), (0, GATE_PAD - 3 * NSA_REP)))
    gates = gates.reshape(D_MODEL, NSA_KV_GROUPS * GATE_PAD)
    return jnp.concatenate([q_nsa, kv, rest, gates], axis=1).astype(bf16)


def _overlap_t(seq):
    n_slot = seq // CMP_STRIDE
    n_sb = seq // SEL_BLOCK
    cmp_start = np.arange(n_slot) * CMP_STRIDE
    sel_start = np.arange(n_sb) * SEL_BLOCK
    ov = np.clip(np.minimum(cmp_start[None, :] + CMP_BLOCK, sel_start[:, None] + SEL_BLOCK)
                 - np.maximum(cmp_start[None, :], sel_start[:, None]), 0, None)
    return jnp.asarray(ov.astype(np.float32) / CMP_BLOCK)


def kernel(x, mem, positions, norm_pre, w_in, cmp_pos_k, cmp_w1_k, cmp_w2_k, cmp_pos_v, cmp_w1_v,
           cmp_w2_v, ret_gn, mem_norm, w_mem_kv, w_out, norm_post):
    depth = norm_pre.shape[0]
    batch, seq, _ = x.shape
    assert x.shape[2] == D_MODEL and mem.shape[1:] == (MEM_LEN, D_MODEL)
    assert seq % SLC_CHUNK == 0 and seq >= WINDOW + Q_BLOCK

    half = ROT_DIM // 2
    inv_n = ROPE_THETA ** (-jnp.arange(half, dtype=f32) / half)
    halfr = RET_QK_DIM // 2
    inv_r = RET_THETA ** (-jnp.arange(halfr, dtype=f32) / halfr)
    assert ROT_DIM <= RET_QK_DIM and 2 * RET_QK_DIM == LANES
    inv = jnp.concatenate([inv_n, inv_n, jnp.zeros((RET_QK_DIM - ROT_DIM,), f32), inv_r, inv_r])[None, :]
    ovl_t = _overlap_t(seq)
    expand = jnp.asarray(np.arange(seq)[:, None] // SEL_BLOCK == np.arange(LANES)[None, :], bf16)
    pos2 = positions.reshape(batch * seq, 1)
    mem2 = mem.reshape(batch * MEM_LEN, D_MODEL)

    x2 = x.reshape(batch * seq, D_MODEL)
    for layer in range(depth):
        w_r = _relayout_w_in(w_in[layer])
        q, kv, rq, rk, rv, mq, gsilu, gsig = _in_proj(x2, pos2, norm_pre[layer][None, :], w_r, inv)
        kvm = _mem_kv(mem2, mem_norm[layer][None, :], w_mem_kv[layer].astype(bf16))
        cmp = _compress(
            kv,
            jnp.stack([cmp_pos_k[layer], cmp_pos_v[layer]]),
            jnp.stack([cmp_w1_k[layer], cmp_w1_v[layer]]).astype(bf16),
            jnp.stack([cmp_w2_k[layer], cmp_w2_v[layer]]).astype(bf16),
            batch, seq)
        o_nsa = _nsa(q, cmp, kv, gsig, ovl_t, expand, batch, seq)
        o_ret = _retention(rq, rk, rv, ret_gn[layer][None, :], batch, seq)
        x2 = _out_proj(x2, o_nsa, o_ret, mq, kvm, gsilu, w_out[layer].astype(bf16),
                       norm_post[layer][None, :], seq)
    return x2.reshape(batch, seq, D_MODEL)
```

```python
import functools
import math

import jax
import jax.numpy as jnp
import numpy as np
from jax import lax
from jax.experimental import pallas as pl
from jax.experimental.pallas import tpu as pltpu

D_MODEL = 1024
MEM_LEN = 256
EPS = 1e-6

NSA_HEADS = 8
NSA_KV_GROUPS = 2
NSA_REP = NSA_HEADS // NSA_KV_GROUPS
NSA_HEAD_DIM = 128
CMP_BLOCK = 32
CMP_STRIDE = 16
CMP_HIDDEN = 256
SEL_BLOCK = 64
N_SELECT = 16
WINDOW = 512
Q_BLOCK = 256
ROPE_THETA = 500000.0
ROT_DIM = NSA_HEAD_DIM // 4

RET_HEADS = 4
RET_QK_DIM = 64
RET_V_DIM = 128
RET_CHUNK = 128
RET_THETA = 10000.0

MEM_HEADS = 4
MEM_HEAD_DIM = 128

NSA_WIDTH = NSA_HEADS * NSA_HEAD_DIM
NSA_KV_ALL = 3 * 2 * NSA_KV_GROUPS * NSA_HEAD_DIM
RET_QK_WIDTH = RET_HEADS * RET_QK_DIM
RET_WIDTH = RET_HEADS * RET_V_DIM
MEM_WIDTH = MEM_HEADS * MEM_HEAD_DIM
MIX_WIDTH = NSA_WIDTH + RET_WIDTH + MEM_WIDTH
GATE_PAD = 128

LANES = 128
SUBLANES = 8
NEG = -1e30
LOG2E = math.log2(math.e)
NSA_QSCALE = NSA_HEAD_DIM ** -0.5 * LOG2E
MEM_QSCALE = MEM_HEAD_DIM ** -0.5 * LOG2E
VMEM_LIMIT = 56 * 1024 * 1024

ROW_TILE = 256
CMP_PITCH = 24
CMP_BATCH = 4
OUT_ROW_TILE = 512
SLC_CHUNK = 512

_NT = (((1,), (1,)), ((), ()))
_TN = (((0,), (0,)), ((), ()))

bf16 = jnp.bfloat16
f32 = jnp.float32


def _dot(a, b):
    return jnp.dot(a, b, preferred_element_type=f32)


def _dot_nt(a, b):
    return lax.dot_general(a, b, _NT, preferred_element_type=f32)


def _dot_tn(a, b):
    return lax.dot_general(a, b, _TN, preferred_element_type=f32)


def _col_reduce(op, x):
    slabs = [x[i:i + SUBLANES] for i in range(0, x.shape[0], SUBLANES)]
    while len(slabs) > 1:
        nxt = [op(slabs[i], slabs[i + 1]) for i in range(0, len(slabs) - 1, 2)]
        if len(slabs) % 2:
            nxt.append(slabs[-1])
        slabs = nxt
    red = jnp.max if op is jnp.maximum else jnp.sum
    return red(slabs[0], axis=0, keepdims=True)


def _rms(x, g):
    return x * lax.rsqrt(jnp.mean(x * x, axis=-1, keepdims=True) + EPS) * g


def _rotate_heads(acc, c, s1, s2, shift):
    outs = []
    for j in range(acc.shape[1] // LANES):
        a = acc[:, j * LANES:(j + 1) * LANES]
        outs.append(a * c + pltpu.roll(a, shift, 1) * s1 + pltpu.roll(a, LANES - shift, 1) * s2)
    return outs


def _in_proj_kernel(x_ref, pos_ref, g_ref, w_ref, inv_ref,
                    q_ref, kv_ref, rq_ref, rk_ref, rv_ref, mq_ref, gs_ref, gg_ref):
    x = x_ref[...]
    hb = _rms(x, g_ref[...]).astype(bf16)
    c_kv = NSA_WIDTH
    c_rqk = c_kv + NSA_KV_ALL
    c_rv = c_rqk + 2 * RET_QK_WIDTH
    c_mq = c_rv + RET_WIDTH
    c_gate = c_mq + MEM_WIDTH
    c_gg = c_gate + MIX_WIDTH

    pos = pos_ref[...].astype(f32)
    lane = lax.broadcasted_iota(jnp.int32, (x.shape[0], LANES), 1)
    ang = pos * inv_ref[...]
    cs, sn = jnp.cos(ang), jnp.sin(ang)
    half = ROT_DIM // 2
    cn = jnp.where(lane < ROT_DIM, cs, 1.0)
    s1n = jnp.where((lane >= half) & (lane < ROT_DIM), sn, 0.0)
    s2n = jnp.where(lane < half, -sn, 0.0)
    halfr = RET_QK_DIM // 2
    cr = jnp.where(lane >= RET_QK_DIM, cs, pltpu.roll(cs, RET_QK_DIM, 1))
    sr = jnp.where(lane >= RET_QK_DIM, sn, pltpu.roll(sn, RET_QK_DIM, 1))
    lr = lane & (RET_QK_DIM - 1)
    s1r = jnp.where(lr >= halfr, sr, 0.0)
    s2r = jnp.where(lr < halfr, -sr, 0.0)

    for j in range(MIX_WIDTH // 512):
        acc = _dot(hb, w_ref[:, c_gate + j * 512:c_gate + (j + 1) * 512])
        gs_ref[:, j * 512:(j + 1) * 512] = (acc * (1.0 / (1.0 + jnp.exp(-acc)))).astype(bf16)
    mq_ref[...] = (_dot(hb, w_ref[:, c_mq:c_mq + MEM_WIDTH]) * MEM_QSCALE).astype(bf16)
    acc = _dot(hb, w_ref[:, c_gg:c_gg + 2 * GATE_PAD])
    gg_ref[...] = 1.0 / (1.0 + jnp.exp(-acc))

    for j in range(NSA_WIDTH // 512):
        acc = _dot(hb, w_ref[:, j * 512:(j + 1) * 512])
        for i, o in enumerate(_rotate_heads(acc, cn * NSA_QSCALE, s1n * NSA_QSCALE, s2n * NSA_QSCALE, half)):
            q_ref[:, j * 512 + i * LANES: j * 512 + (i + 1) * LANES] = o.astype(bf16)
    for br in range(3):
        acc = _dot(hb, w_ref[:, c_kv + br * 512:c_kv + (br + 1) * 512])
        for i, o in enumerate(_rotate_heads(acc[:, :256], cn, s1n, s2n, half)):
            kv_ref[:, br * 512 + i * LANES: br * 512 + (i + 1) * LANES] = o.astype(bf16)
        kv_ref[:, br * 512 + 256: br * 512 + 512] = acc[:, 256:].astype(bf16)
    acc = _dot(hb, w_ref[:, c_rqk:c_rqk + 2 * RET_QK_WIDTH])
    rot = _rotate_heads(acc, cr, s1r, s2r, halfr)
    for i in range(2):
        rq_ref[:, i * LANES:(i + 1) * LANES] = rot[i].astype(bf16)
        rk_ref[:, i * LANES:(i + 1) * LANES] = (rot[2 + i] * (RET_QK_DIM ** -0.5)).astype(bf16)

    rv_ref[...] = _dot(hb, w_ref[:, c_rv:c_rv + RET_WIDTH]).astype(bf16)


def _in_proj(x2, pos2, norm_pre, w_r, inv):
    bt = x2.shape[0]
    tm = ROW_TILE
    win = w_r.shape[1]
    row = lambda i: (i, 0)
    const = lambda i: (0, 0)
    widths = (NSA_WIDTH, NSA_KV_ALL, RET_QK_WIDTH, RET_QK_WIDTH, RET_WIDTH, MEM_WIDTH, MIX_WIDTH)
    out_shape = [jax.ShapeDtypeStruct((bt, w), bf16) for w in widths]
    out_shape.append(jax.ShapeDtypeStruct((bt, 2 * GATE_PAD), f32))
    out_specs = [pl.BlockSpec((tm, w), row) for w in widths] + [pl.BlockSpec((tm, 2 * GATE_PAD), row)]
    return pl.pallas_call(
        _in_proj_kernel,
        grid=(bt // tm,),
        in_specs=[
            pl.BlockSpec((tm, D_MODEL), row),
            pl.BlockSpec((tm, 1), row),
            pl.BlockSpec((1, D_MODEL), const),
            pl.BlockSpec((D_MODEL, win), const),
            pl.BlockSpec((1, LANES), const),
        ],
        out_specs=out_specs,
        out_shape=out_shape,
        compiler_params=pltpu.CompilerParams(
            dimension_semantics=("arbitrary",), vmem_limit_bytes=VMEM_LIMIT),
        name="in_proj",
    )(x2, pos2, norm_pre, w_r, inv)


def _mem_kv_kernel(m_ref, g_ref, w_ref, o_ref):
    hb = _rms(m_ref[...], g_ref[...]).astype(bf16)
    o_ref[...] = _dot(hb, w_ref[...]).astype(bf16)


def _mem_kv(mem2, mem_norm, w_kv):
    n = mem2.shape[0]
    tm = MEM_LEN
    return pl.pallas_call(
        _mem_kv_kernel,
        grid=(n // tm,),
        in_specs=[
            pl.BlockSpec((tm, D_MODEL), lambda i: (i, 0)),
            pl.BlockSpec((1, D_MODEL), lambda i: (0, 0)),
            pl.BlockSpec((D_MODEL, 2 * MEM_WIDTH), lambda i: (0, 0)),
        ],
        out_specs=pl.BlockSpec((tm, 2 * MEM_WIDTH), lambda i: (i, 0)),
        out_shape=jax.ShapeDtypeStruct((n, 2 * MEM_WIDTH), bf16),
        compiler_params=pltpu.CompilerParams(
            dimension_semantics=("arbitrary",), vmem_limit_bytes=VMEM_LIMIT),
        name="mem_kv",
    )(mem2, mem_norm, w_kv)


def _compress_kernel(kv_ref, pos_ref, w1_ref, w2_ref, o_ref, xf_ref):
    nb, _, per, _ = o_ref.shape
    nslot = nb * per
    for i in range(nslot):
        xf_ref[i * CMP_PITCH:i * CMP_PITCH + CMP_STRIDE, :] = (
            kv_ref[i * CMP_STRIDE:(i + 1) * CMP_STRIDE, :].astype(f32))
    pos = pos_ref[0]
    la, lb = [], []
    for p in range(CMP_STRIDE):
        a = xf_ref[pl.ds(p, nslot, stride=CMP_PITCH), :]
        la.append((a + pos[p:p + 1, :]).astype(bf16))
        lb.append((a + pos[CMP_STRIDE + p:CMP_STRIDE + p + 1, :]).astype(bf16))
    khalf = CMP_STRIDE * NSA_HEAD_DIM
    ha = _dot(jnp.concatenate(la, axis=1), w1_ref[0, 0:khalf, :])
    hb = _dot(jnp.concatenate(lb, axis=1), w1_ref[0, khalf:2 * khalf, :])
    h = ha + pltpu.roll(hb, nslot - 1, 0)
    h = jax.nn.gelu(h)
    out = _dot(h.astype(bf16), w2_ref[0])
    slot = lax.broadcasted_iota(jnp.int32, out.shape, 0) & (per - 1)
    out = jnp.where(slot < per - 1, out, 0.0).astype(bf16)
    for b in range(nb):
        o_ref[b, 0] = out[b * per:(b + 1) * per]


def _compress(kv, cmp_pos, cmp_w1, cmp_w2, batch, seq):
    nslot = seq // CMP_STRIDE
    assert nslot & (nslot - 1) == 0
    nb = math.gcd(batch, CMP_BATCH)
    kinds = 2 * NSA_KV_GROUPS
    return pl.pallas_call(
        _compress_kernel,
        grid=(kinds, batch // nb),
        in_specs=[
            pl.BlockSpec((nb * seq, NSA_HEAD_DIM), lambda j, b: (b, j)),
            pl.BlockSpec((1, CMP_BLOCK, NSA_HEAD_DIM), lambda j, b: (j // NSA_KV_GROUPS, 0, 0)),
            pl.BlockSpec((1, CMP_BLOCK * NSA_HEAD_DIM, CMP_HIDDEN), lambda j, b: (j // NSA_KV_GROUPS, 0, 0)),
            pl.BlockSpec((1, CMP_HIDDEN, NSA_HEAD_DIM), lambda j, b: (j // NSA_KV_GROUPS, 0, 0)),
        ],
        out_specs=pl.BlockSpec((nb, 1, nslot, NSA_HEAD_DIM), lambda j, b: (b, j, 0, 0)),
        out_shape=jax.ShapeDtypeStruct((batch, kinds, nslot, NSA_HEAD_DIM), bf16),
        scratch_shapes=[pltpu.VMEM((nb * nslot * CMP_PITCH, NSA_HEAD_DIM), f32)],
        compiler_params=pltpu.CompilerParams(
            dimension_semantics=("arbitrary", "arbitrary"), vmem_limit_bytes=VMEM_LIMIT),
        name="compress",
    )(kv, cmp_pos, cmp_w1, cmp_w2)


def _nsa_tile(t0, g, q_ref, cmp_ref, ks_ref, vs_ref, kw_ref, vw_ref, g_ref, ovl_ref, exp_ref, o_ref):
    gcol = slice(g * NSA_HEAD_DIM, (g + 1) * NSA_HEAD_DIM)
    gw = NSA_REP * NSA_HEAD_DIM
    kc_ref, vc_ref = cmp_ref.at[0, g], cmp_ref.at[0, NSA_KV_GROUPS + g]
    n_static = (t0 + Q_BLOCK + SLC_CHUNK - 1) // SLC_CHUNK
    nslot = cmp_ref.shape[2]
    n_cmp = nslot - 1
    heads = range(NSA_REP)
    hcols = lambda a, r: a[:, r * Q_BLOCK:(r + 1) * Q_BLOCK]

    q = q_ref[:, g * gw:(g + 1) * gw]
    q4 =jnp.concatenate([q[:, r * NSA_HEAD_DIM:(r + 1) * NSA_HEAD_DIM] for r in heads], axis=0)
    tq = t0 + lax.broadcasted_iota(jnp.int32, (1, Q_BLOCK), 1)

    def softmax_cols(s_r):
        m = _col_reduce(jnp.maximum, s_r)
        e = jnp.exp2(s_r - m)
        return e, _col_reduce(jnp.add, e)

    s = _dot_nt(kc_ref[...], q4)
    start = max(t0 - WINDOW, 0)
    wlen = t0 + Q_BLOCK - start
    sw = _dot_nt(kw_ref[start:start + wlen, gcol], q4)
    sc0 = _dot_nt(ks_ref[0:min(SLC_CHUNK, t0 + Q_BLOCK), gcol], q4)
    nrow = lax.broadcasted_iota(jnp.int32, (nslot, 1), 0)
    bias_c = jnp.where((nrow * CMP_STRIDE + (CMP_BLOCK - 1) <= tq) & (nrow < n_cmp), 0.0, NEG)
    row_ok = tq >= CMP_BLOCK - 1
    ps, psum = [], None
    for r in heads:
        e, l = softmax_cols(hcols(s, r) + bias_c)
        p = e * jnp.where(row_ok, 1.0 / l, 0.0)
        ps.append(p.astype(bf16))
        psum = p if psum is None else psum + p
    o_cmp = _dot_tn(vc_ref[...], jnp.concatenate(ps, axis=1))

    imp_t = jnp.dot(ovl_ref[...], psum, precision=lax.Precision.HIGHEST,
                    preferred_element_type=f32)
    n_sb = imp_t.shape[0]
    jblk = lax.broadcasted_iota(jnp.int32, (n_sb, Q_BLOCK), 0)
    tl = t0 + lax.broadcasted_iota(jnp.int32, (n_sb, Q_BLOCK), 1)
    cur = tl >> 6
    forced = (jblk == 0) | (jblk == cur) | (jblk == cur - 1)
    valid = jblk * SEL_BLOCK <= tl
    v = jnp.where(forced, jnp.inf, jnp.where(valid, imp_t, -jnp.inf))
    n_live = min(n_sb, n_static * SLC_CHUNK // SEL_BLOCK)
    slabs = [v[k:k + SUBLANES] for k in range(0, n_live, SUBLANES)]
    jl = lax.broadcasted_iota(jnp.int32, (SUBLANES, Q_BLOCK), 0)
    cnts = [jnp.zeros((SUBLANES, Q_BLOCK), f32) for _ in slabs]
    for i in range(n_live):
        vi = jnp.broadcast_to(v[i:i + 1, :], (SUBLANES, Q_BLOCK))
        for k, vk in enumerate(slabs):
            if k * SUBLANES > i:
                ahead = vi >= vk
            elif (k + 1) * SUBLANES <= i:
                ahead = vi > vk
            else:
                ahead = (vi > vk) | ((vi == vk) & (jl > i - k * SUBLANES))
            cnts[k] = cnts[k] + jnp.where(ahead, 1.0, 0.0)
    selbias_pad = jnp.concatenate([jnp.where(c < float(N_SELECT), 0.0, NEG) for c in cnts]
                                  + [jnp.zeros((LANES - n_live, Q_BLOCK), f32)], axis=0)
    q4_aug = jnp.concatenate([q4, jnp.concatenate([selbias_pad.T.astype(bf16)] * NSA_REP, axis=0)], axis=1)

    def gate_rows(br):
        return jnp.concatenate([gs[br * NSA_REP + r:br * NSA_REP + r + 1, :] for r in heads], axis=1)

    kl = lax.broadcasted_iota(jnp.int32, (Q_BLOCK, Q_BLOCK), 0)
    ql = lax.broadcasted_iota(jnp.int32, (Q_BLOCK, Q_BLOCK), 1)
    head_bias = jnp.where(kl > ql, 0.0, NEG)
    tail_bias = jnp.where(kl <= ql, 0.0, NEG)

    def add_tail_bias(a):
        body = a.shape[0] - Q_BLOCK
        tail = a[body:] + tail_bias
        return tail if body == 0 else jnp.concatenate([a[:body], tail], axis=0)

    def add_bias_w(a):
        if t0 >= WINDOW:
            a = jnp.concatenate([a[:Q_BLOCK] + head_bias, a[Q_BLOCK:]], axis=0)
        return add_tail_bias(a)

    gs = g_ref[:, g * GATE_PAD:(g + 1) * GATE_PAD].T
    es, ls = [], []
    for r in heads:
        e, l = softmax_cols(add_bias_w(hcols(sw, r)))
        es.append(e.astype(bf16))
        ls.append(l)
    o_win = _dot_tn(vw_ref[start:start + wlen, gcol], jnp.concatenate(es, axis=1))
    o_part = (gate_rows(0) * o_cmp
              + (gate_rows(2) * (1.0 / jnp.concatenate(ls, axis=1))) * o_win)

    def slc_branch(n):
        use_sel = t0 + Q_BLOCK > N_SELECT * SEL_BLOCK
        bounds = [(c * SLC_CHUNK, min((c + 1) * SLC_CHUNK, t0 + Q_BLOCK)) for c in range(n)]

        def slc_scores(c):
            lo, hi = bounds[c]
            ks = ks_ref[lo:hi, gcol]
            if not use_sel:
                return _dot_nt(ks, q4)
            return _dot_nt(jnp.concatenate([ks, exp_ref[lo:hi, :]], axis=1), q4_aug)

        m_i, l_i, acc = [None] * NSA_REP, [None] * NSA_REP, None
        sc = slc_scores(0) if use_sel else sc0
        for c in range(n):
            sc_next = slc_scores(c + 1) if c + 1 < n else None
            es, alphas = [], []
            for r in heads:
                s_r = add_tail_bias(hcols(sc, r)) if c == n - 1 else hcols(sc, r)
                m_c = _col_reduce(jnp.maximum, s_r)
                if c == 0:
                    e = jnp.exp2(s_r - m_c)
                    m_i[r], l_i[r] = m_c, _col_reduce(jnp.add, e)
                else:
                    m_new = jnp.maximum(m_i[r], m_c)
                    alpha = jnp.exp2(m_i[r] - m_new)
                    e = jnp.exp2(s_r - m_new)
                    m_i[r], l_i[r] = m_new, alpha * l_i[r] + _col_reduce(jnp.add, e)
                    alphas.append(alpha)
                es.append(e.astype(bf16))
            pv = _dot_tn(vs_ref[bounds[c][0]:bounds[c][1], gcol], jnp.concatenate(es, axis=1))
            acc = pv if c == 0 else jnp.concatenate(alphas, axis=1) * acc + pv
            sc = sc_next
        o = o_part + (gate_rows(1) * (1.0 / jnp.concatenate(l_i, axis=1))) * acc
        for r in heads:
            col = g * gw + r * NSA_HEAD_DIM
            o_ref[:, col:col + NSA_HEAD_DIM] = hcols(o, r).T.astype(bf16)

    slc_branch(n_static)


def _nsa_kernel(*refs):
    seq = refs[2].shape[0]

    def tile(t0):
        for g in range(NSA_KV_GROUPS):
            _nsa_tile(t0, g, *refs)

    for qi in range(seq // Q_BLOCK):
        pl.when(pl.program_id(1) == qi)(functools.partial(tile, qi * Q_BLOCK))


def _nsa(q, cmp, kv, gsig, ovl_t, expand, batch, seq):
    nq = seq // Q_BLOCK
    nslot = cmp.shape[2]
    G = NSA_KV_GROUPS
    kvspec = lambda j: pl.BlockSpec((seq, G * NSA_HEAD_DIM), lambda b, i, j=j: (b, j))
    return pl.pallas_call(
        _nsa_kernel,
        grid=(batch, nq),
        in_specs=[
            pl.BlockSpec((Q_BLOCK, NSA_WIDTH), lambda b, i: (b * nq + i, 0)),
            pl.BlockSpec((1, 2 * G, nslot, NSA_HEAD_DIM), lambda b, i: (b, 0, 0, 0)),
            kvspec(2), kvspec(3), kvspec(4), kvspec(5),
            pl.BlockSpec((Q_BLOCK, G * GATE_PAD), lambda b, i: (b * nq + i, 0)),
            pl.BlockSpec(ovl_t.shape, lambda b, i: (0, 0)),
            pl.BlockSpec(expand.shape, lambda b, i: (0, 0)),
        ],
        out_specs=pl.BlockSpec((Q_BLOCK, NSA_WIDTH), lambda b, i: (b * nq + i, 0)),
        out_shape=jax.ShapeDtypeStruct((batch * seq, NSA_WIDTH), bf16),
        compiler_params=pltpu.CompilerParams(
            dimension_semantics=("arbitrary", "arbitrary"), vmem_limit_bytes=VMEM_LIMIT),
        name="nsa",
    )(q, cmp, kv, kv, kv, kv, gsig, ovl_t, expand)


def _retention_kernel(q_ref, k_ref, v_ref, gn_ref, o_ref):
    seq = q_ref.shape[0]
    C = RET_CHUNK
    n = lax.broadcasted_iota(jnp.int32, (C, C), 0)
    mcol = lax.broadcasted_iota(jnp.int32, (C, C), 1)
    diff = (n - mcol).astype(f32)
    nvec = lax.broadcasted_iota(jnp.int32, (C, 1), 0).astype(f32)
    decay, xi, zeta, gamma_c = [], [], [], []
    for h in range(RET_HEADS):
        lg = math.log1p(-(2.0 ** (-5.0 - h)))
        decay.append(jnp.where(diff >= 0, jnp.exp(lg * jnp.maximum(diff, 0.0)), 0.0))
        xi.append(jnp.exp(lg * (nvec + 1.0)))
        zeta.append(jnp.exp(lg * (C - 1.0 - nvec)))
        gamma_c.append(math.exp(lg * C))
    gn = gn_ref[...]

    def body(c, states):
        base = pl.multiple_of(c * C, C)
        hs = range(RET_HEADS)
        qc = [q_ref[pl.ds(base, C), h * RET_QK_DIM:(h + 1) * RET_QK_DIM] for h in hs]
        kc = [k_ref[pl.ds(base, C), h * RET_QK_DIM:(h + 1) * RET_QK_DIM] for h in hs]
        vc = [v_ref[pl.ds(base, C), h * RET_V_DIM:(h + 1) * RET_V_DIM] for h in hs]
        qk = [_dot_nt(qc[h], kc[h]) for h in hs]
        cross = [_dot(qc[h], states[h].astype(bf16)) for h in hs]
        kv = [_dot_tn((kc[h].astype(f32) * zeta[h]).astype(bf16), vc[h]) for h in hs]
        intra = [_dot((qk[h] * decay[h]).astype(bf16), vc[h]) for h in hs]
        for h in hs:
            out = intra[h] + cross[h] * xi[h]
            mu = jnp.mean(out, axis=-1, keepdims=True)
            d = out - mu
            var = jnp.mean(d * d, axis=-1, keepdims=True)
            o = d * lax.rsqrt(var + EPS) * gn[:, h * RET_V_DIM:(h + 1) * RET_V_DIM]
            o_ref[pl.ds(base, C), h * RET_V_DIM:(h + 1) * RET_V_DIM] = o.astype(bf16)
        return tuple(states[h] * gamma_c[h] + kv[h] for h in hs)

    s0 = tuple(jnp.zeros((RET_QK_DIM, RET_V_DIM), f32) for _ in range(RET_HEADS))
    lax.fori_loop(0, seq // C, body, s0, unroll=2)


def _retention(rq, rk, rv, ret_gn, batch, seq):
    return pl.pallas_call(
        _retention_kernel,
        grid=(batch,),
        in_specs=[
            pl.BlockSpec((seq, RET_QK_WIDTH), lambda b: (b, 0)),
            pl.BlockSpec((seq, RET_QK_WIDTH), lambda b: (b, 0)),
            pl.BlockSpec((seq, RET_WIDTH), lambda b: (b, 0)),
            pl.BlockSpec((1, RET_WIDTH), lambda b: (0, 0)),
        ],
        out_specs=pl.BlockSpec((seq, RET_WIDTH), lambda b: (b, 0)),
        out_shape=jax.ShapeDtypeStruct((batch * seq, RET_WIDTH), bf16),
        compiler_params=pltpu.CompilerParams(
            dimension_semantics=("arbitrary",), vmem_limit_bytes=VMEM_LIMIT),
        name="retention",
    )(rq, rk, rv, ret_gn)


def _out_proj_kernel(x_ref, on_ref, or_ref, mq_ref, kvm_ref, gs_ref, w_ref, g_ref, o_ref):
    off = NSA_WIDTH + RET_WIDTH
    heads = range(MEM_HEADS)
    hsl = lambda h: slice(h * MEM_HEAD_DIM, (h + 1) * MEM_HEAD_DIM)
    s = [_dot_nt(mq_ref[:, hsl(h)], kvm_ref[:, hsl(h)]) for h in heads]
    mix_a = jnp.concatenate([on_ref[...] * gs_ref[:, 0:NSA_WIDTH],
                             or_ref[...] * gs_ref[:, NSA_WIDTH:off]], axis=1)
    y = _dot(mix_a, w_ref[0:off, :])
    mix_m = []
    for h in heads:
        m = jnp.max(s[h], axis=1, keepdims=True)
        e = jnp.exp2(s[h] - m)
        l = jnp.sum(e, axis=1, keepdims=True)
        om = _dot(e.astype(bf16), kvm_ref[:, MEM_WIDTH + h * MEM_HEAD_DIM:MEM_WIDTH + (h + 1) * MEM_HEAD_DIM])
        mix_m.append((om * (1.0 / l)).astype(bf16) * gs_ref[:, off + h * MEM_HEAD_DIM:off + (h + 1) * MEM_HEAD_DIM])
    y = y + _dot(jnp.concatenate(mix_m, axis=1), w_ref[off:, :])
    o_ref[...] = x_ref[...] + _rms(y, g_ref[...])


def _out_proj(x2, o_nsa, o_ret, mq, kvm, gsilu, w_out, norm_post, seq):
    bt = x2.shape[0]
    tm = OUT_ROW_TILE
    per_b = seq // tm
    row = lambda i: (i, 0)
    const = lambda i: (0, 0)
    return pl.pallas_call(
        _out_proj_kernel,
        grid=(bt // tm,),
        in_specs=[
            pl.BlockSpec((tm, D_MODEL), row),
            pl.BlockSpec((tm, NSA_WIDTH), row),
            pl.BlockSpec((tm, RET_WIDTH), row),
            pl.BlockSpec((tm, MEM_WIDTH), row),
            pl.BlockSpec((MEM_LEN, 2 * MEM_WIDTH), lambda i: (i // per_b, 0)),
            pl.BlockSpec((tm, MIX_WIDTH), row),
            pl.BlockSpec((MIX_WIDTH, D_MODEL), const),
            pl.BlockSpec((1, D_MODEL), const),
        ],
        out_specs=pl.BlockSpec((tm, D_MODEL), row),
        out_shape=jax.ShapeDtypeStruct((bt, D_MODEL), f32),
        compiler_params=pltpu.CompilerParams(
            dimension_semantics=("arbitrary",), vmem_limit_bytes=VMEM_LIMIT),
        name="out_proj",
    )(x2, o_nsa, o_ret, mq, kvm, gsilu, w_out, norm_post)


def _relayout_w_in(w):
    o = 0
    q_nsa = w[:, o:o + NSA_WIDTH]; o += NSA_WIDTH
    kv = w[:, o:o + NSA_KV_ALL]; o += NSA_KV_ALL
    gates = w[:, o:o + 3 * NSA_HEADS]; o += 3 * NSA_HEADS
    rest = w[:, o:]
    gates = gates.reshape(D_MODEL, 3, NSA_KV_GROUPS, NSA_REP).transpose(0, 2, 1, 3)
    gates = gates.reshape(D_MODEL, NSA_KV_GROUPS, 3 * NSA_REP)
    gates = jnp.pad(gates, ((0, 0), (0, 0), (0, GATE_PAD - 3 * NSA_REP)))
    gates = gates.reshape(D_MODEL, NSA_KV_GROUPS * GATE_PAD)
    return jnp.concatenate([q_nsa, kv, rest, gates], axis=1).astype(bf16)


def _overlap_t(seq):
    n_slot = seq // CMP_STRIDE
    n_sb = seq // SEL_BLOCK
    cmp_start = np.arange(n_slot) * CMP_STRIDE
    sel_start = np.arange(n_sb) * SEL_BLOCK
    ov = np.clip(np.minimum(cmp_start[None, :] + CMP_BLOCK, sel_start[:, None] + SEL_BLOCK)
                 - np.maximum(cmp_start[None, :], sel_start[:, None]), 0, None)
    return jnp.asarray(ov.astype(np.float32) / CMP_BLOCK)


def kernel(x, mem, positions, norm_pre, w_in, cmp_pos_k, cmp_w1_k, cmp_w2_k, cmp_pos_v, cmp_w1_v,
           cmp_w2_v, ret_gn, mem_norm, w_mem_kv, w_out, norm_post):
    depth = norm_pre.shape[0]
    batch, seq, _ = x.shape
    assert x.shape[2] == D_MODEL and mem.shape[1:] == (MEM_LEN, D_MODEL)
    assert seq % SLC_CHUNK == 0 and seq >= WINDOW + Q_BLOCK

    half = ROT_DIM // 2
    inv_n = ROPE_THETA ** (-jnp.arange(half, dtype=f32) / half)
    halfr = RET_QK_DIM // 2
    inv_r = RET_THETA ** (-jnp.arange(halfr, dtype=f32) / halfr)
    assert ROT_DIM <= RET_QK_DIM and 2 * RET_QK_DIM == LANES
    inv = jnp.concatenate([inv_n, inv_n, jnp.zeros((RET_QK_DIM - ROT_DIM,), f32), inv_r, inv_r])[None, :]
    ovl_t = _overlap_t(seq)
    expand = jnp.asarray(np.arange(seq)[:, None] // SEL_BLOCK == np.arange(LANES)[None, :], bf16)
    pos2 = positions.reshape(batch * seq, 1)
    mem2 = mem.reshape(batch * MEM_LEN, D_MODEL)

    x2 = x.reshape(batch * seq, D_MODEL)
    for layer in range(depth):
        w_r = _relayout_w_in(w_in[layer])
        q, kv, rq, rk, rv, mq, gsilu, gsig = _in_proj(x2, pos2, norm_pre[layer][None, :], w_r, inv)
        kvm = _mem_kv(mem2, mem_norm[layer][None, :], w_mem_kv[layer].astype(bf16))
        cmp = _compress(
            kv,
            jnp.stack([cmp_pos_k[layer], cmp_pos_v[layer]]),
            jnp.stack([cmp_w1_k[layer], cmp_w1_v[layer]]).astype(bf16),
            jnp.stack([cmp_w2_k[layer], cmp_w2_v[layer]]).astype(bf16),
            batch, seq)
        o_nsa = _nsa(q, cmp, kv, gsig, ovl_t, expand, batch, seq)
        o_ret = _retention(rq, rk, rv, ret_gn[layer][None, :], batch, seq)
        x2 = _out_proj(x2, o_nsa, o_ret, mq, kvm, gsilu, w_out[layer].astype(bf16),
                       norm_post[layer][None, :], seq)
    return x2.reshape(batch, seq, D_MODEL)
```

```python
import functools
import math

import jax
import jax.numpy as jnp
import numpy as np
from jax import lax
from jax.experimental import pallas as pl
from jax.experimental.pallas import tpu as pltpu

D_MODEL = 1024
MEM_LEN = 256
EPS = 1e-6

NSA_HEADS = 8
NSA_KV_GROUPS = 2
NSA_REP = NSA_HEADS // NSA_KV_GROUPS
NSA_HEAD_DIM = 128
CMP_BLOCK = 32
CMP_STRIDE = 16
CMP_HIDDEN = 256
SEL_BLOCK = 64
N_SELECT = 16
WINDOW = 512
Q_BLOCK = 256
ROPE_THETA = 500000.0
ROT_DIM = NSA_HEAD_DIM // 4

RET_HEADS = 4
RET_QK_DIM = 64
RET_V_DIM = 128
RET_CHUNK = 128
RET_THETA = 10000.0

MEM_HEADS = 4
MEM_HEAD_DIM = 128

NSA_WIDTH = NSA_HEADS * NSA_HEAD_DIM
NSA_KV_ALL = 3 * 2 * NSA_KV_GROUPS * NSA_HEAD_DIM
RET_QK_WIDTH = RET_HEADS * RET_QK_DIM
RET_WIDTH = RET_HEADS * RET_V_DIM
MEM_WIDTH = MEM_HEADS * MEM_HEAD_DIM
MIX_WIDTH = NSA_WIDTH + RET_WIDTH + MEM_WIDTH
GATE_PAD = 128

LANES = 128
SUBLANES = 8
NEG = -1e30
LOG2E = math.log2(math.e)
NSA_QSCALE = NSA_HEAD_DIM ** -0.5 * LOG2E
MEM_QSCALE = MEM_HEAD_DIM ** -0.5 * LOG2E
VMEM_LIMIT = 56 * 1024 * 1024

ROW_TILE = 256
CMP_PITCH = 24
CMP_BATCH = 4
OUT_ROW_TILE = 512
SLC_CHUNK = 512

_NT = (((1,), (1,)), ((), ()))
_TN = (((0,), (0,)), ((), ()))

bf16 = jnp.bfloat16
f32 = jnp.float32


def _dot(a, b):
    return jnp.dot(a, b, preferred_element_type=f32)


def _dot_nt(a, b):
    return lax.dot_general(a, b, _NT, preferred_element_type=f32)


def _dot_tn(a, b):
    return lax.dot_general(a, b, _TN, preferred_element_type=f32)


def _col_reduce(op, x):
    slabs = [x[i:i + SUBLANES] for i in range(0, x.shape[0], SUBLANES)]
    while len(slabs) > 1:
        nxt = [op(slabs[i], slabs[i + 1]) for i in range(0, len(slabs) - 1, 2)]
        if len(slabs) % 2:
            nxt.append(slabs[-1])
        slabs = nxt
    red = jnp.max if op is jnp.maximum else jnp.sum
    return red(slabs[0], axis=0, keepdims=True)


def _rms(x, g):
    return x * lax.rsqrt(jnp.mean(x * x, axis=-1, keepdims=True) + EPS) * g


def _rotate_heads(acc, c, s1, s2, shift):
    outs = []
    for j in range(acc.shape[1] // LANES):
        a = acc[:, j * LANES:(j + 1) * LANES]
        outs.append(a * c + pltpu.roll(a, shift, 1) * s1 + pltpu.roll(a, LANES - shift, 1) * s2)
    return outs


def _in_proj_kernel(x_ref, pos_ref, g_ref, w_ref, inv_ref,
                    q_ref, kv_ref, rq_ref, rk_ref, rv_ref, mq_ref, gs_ref, gg_ref):
    x = x_ref[...]
    hb = _rms(x, g_ref[...]).astype(bf16)
    c_kv = NSA_WIDTH
    c_rqk = c_kv + NSA_KV_ALL
    c_rv = c_rqk + 2 * RET_QK_WIDTH
    c_mq = c_rv + RET_WIDTH
    c_gate = c_mq + MEM_WIDTH
    c_gg = c_gate + MIX_WIDTH

    pos = pos_ref[...].astype(f32)
    lane = lax.broadcasted_iota(jnp.int32, (x.shape[0], LANES), 1)
    ang = pos * inv_ref[...]
    cs, sn = jnp.cos(ang), jnp.sin(ang)
    half = ROT_DIM // 2
    cn = jnp.where(lane < ROT_DIM, cs, 1.0)
    s1n = jnp.where((lane >= half) & (lane < ROT_DIM), sn, 0.0)
    s2n = jnp.where(lane < half, -sn, 0.0)
    halfr = RET_QK_DIM // 2
    cr = jnp.where(lane >= RET_QK_DIM, cs, pltpu.roll(cs, RET_QK_DIM, 1))
    sr = jnp.where(lane >= RET_QK_DIM, sn, pltpu.roll(sn, RET_QK_DIM, 1))
    lr = lane & (RET_QK_DIM - 1)
    s1r = jnp.where(lr >= halfr, sr, 0.0)
    s2r = jnp.where(lr < halfr, -sr, 0.0)

    for j in range(MIX_WIDTH // 512):
        acc = _dot(hb, w_ref[:, c_gate + j * 512:c_gate + (j + 1) * 512])
        gs_ref[:, j * 512:(j + 1) * 512] = (acc * (1.0 / (1.0 + jnp.exp(-acc)))).astype(bf16)
    mq_ref[...] = (_dot(hb, w_ref[:, c_mq:c_mq + MEM_WIDTH]) * MEM_QSCALE).astype(bf16)
    acc = _dot(hb, w_ref[:, c_gg:c_gg + 2 * GATE_PAD])
    gg_ref[...] = 1.0 / (1.0 + jnp.exp(-acc))

    for j in range(NSA_WIDTH // 512):
        acc = _dot(hb, w_ref[:, j * 512:(j + 1) * 512])
        for i, o in enumerate(_rotate_heads(acc, cn * NSA_QSCALE, s1n * NSA_QSCALE, s2n * NSA_QSCALE, half)):
            q_ref[:, j * 512 + i * LANES: j * 512 + (i + 1) * LANES] = o.astype(bf16)
    for br in range(3):
        acc = _dot(hb, w_ref[:, c_kv + br * 512:c_kv + (br + 1) * 512])
        for i, o in enumerate(_rotate_heads(acc[:, :256], cn, s1n, s2n, half)):
            kv_ref[:, br * 512 + i * LANES: br * 512 + (i + 1) * LANES] = o.astype(bf16)
        kv_ref[:, br * 512 + 256: br * 512 + 512] = acc[:, 256:].astype(bf16)
    acc = _dot(hb, w_ref[:, c_rqk:c_rqk + 2 * RET_QK_WIDTH])
    rot = _rotate_heads(acc, cr, s1r, s2r, halfr)
    for i in range(2):
        rq_ref[:, i * LANES:(i + 1) * LANES] = rot[i].astype(bf16)
        rk_ref[:, i * LANES:(i + 1) * LANES] = (rot[2 + i] * (RET_QK_DIM ** -0.5)).astype(bf16)

    rv_ref[...] = _dot(hb, w_ref[:, c_rv:c_rv + RET_WIDTH]).astype(bf16)


def _in_proj(x2, pos2, norm_pre, w_r, inv):
    bt = x2.shape[0]
    tm = ROW_TILE
    win = w_r.shape[1]
    row = lambda i: (i, 0)
    const = lambda i: (0, 0)
    widths = (NSA_WIDTH, NSA_KV_ALL, RET_QK_WIDTH, RET_QK_WIDTH, RET_WIDTH, MEM_WIDTH, MIX_WIDTH)
    out_shape = [jax.ShapeDtypeStruct((bt, w), bf16) for w in widths]
    out_shape.append(jax.ShapeDtypeStruct((bt, 2 * GATE_PAD), f32))
    out_specs = [pl.BlockSpec((tm, w), row) for w in widths] + [pl.BlockSpec((tm, 2 * GATE_PAD), row)]
    return pl.pallas_call(
        _in_proj_kernel,
        grid=(bt // tm,),
        in_specs=[
            pl.BlockSpec((tm, D_MODEL), row),
            pl.BlockSpec((tm, 1), row),
            pl.BlockSpec((1, D_MODEL), const),
            pl.BlockSpec((D_MODEL, win), const),
            pl.BlockSpec((1, LANES), const),
        ],
        out_specs=out_specs,
        out_shape=out_shape,
        compiler_params=pltpu.CompilerParams(
            dimension_semantics=("arbitrary",), vmem_limit_bytes=VMEM_LIMIT),
        name="in_proj",
    )(x2, pos2, norm_pre, w_r, inv)


def _mem_kv_kernel(m_ref, g_ref, w_ref, o_ref):
    hb = _rms(m_ref[...], g_ref[...]).astype(bf16)
    o_ref[...] = _dot(hb, w_ref[...]).astype(bf16)


def _mem_kv(mem2, mem_norm, w_kv):
    n = mem2.shape[0]
    tm = MEM_LEN
    return pl.pallas_call(
        _mem_kv_kernel,
        grid=(n // tm,),
        in_specs=[
            pl.BlockSpec((tm, D_MODEL), lambda i: (i, 0)),
            pl.BlockSpec((1, D_MODEL), lambda i: (0, 0)),
            pl.BlockSpec((D_MODEL, 2 * MEM_WIDTH), lambda i: (0, 0)),
        ],
        out_specs=pl.BlockSpec((tm, 2 * MEM_WIDTH), lambda i: (i, 0)),
        out_shape=jax.ShapeDtypeStruct((n, 2 * MEM_WIDTH), bf16),
        compiler_params=pltpu.CompilerParams(
            dimension_semantics=("arbitrary",), vmem_limit_bytes=VMEM_LIMIT),
        name="mem_kv",
    )(mem2, mem_norm, w_kv)


def _compress_kernel(kv_ref, pos_ref, w1_ref, w2_ref, o_ref, xf_ref):
    nb, _, per, _ = o_ref.shape
    nslot = nb * per
    for i in range(nslot):
        xf_ref[i * CMP_PITCH:i * CMP_PITCH + CMP_STRIDE, :] = (
            kv_ref[i * CMP_STRIDE:(i + 1) * CMP_STRIDE, :].astype(f32))
    pos = pos_ref[0]
    la, lb = [], []
    for p in range(CMP_STRIDE):
        a = xf_ref[pl.ds(p, nslot, stride=CMP_PITCH), :]
        la.append((a + pos[p:p + 1, :]).astype(bf16))
        lb.append((a + pos[CMP_STRIDE + p:CMP_STRIDE + p + 1, :]).astype(bf16))
    khalf = CMP_STRIDE * NSA_HEAD_DIM
    ha = _dot(jnp.concatenate(la, axis=1), w1_ref[0, 0:khalf, :])
    hb = _dot(jnp.concatenate(lb, axis=1), w1_ref[0, khalf:2 * khalf, :])
    h = ha + pltpu.roll(hb, nslot - 1, 0)
    h = jax.nn.gelu(h)
    out = _dot(h.astype(bf16), w2_ref[0])
    slot = lax.broadcasted_iota(jnp.int32, out.shape, 0) & (per - 1)
    out = jnp.where(slot < per - 1, out, 0.0).astype(bf16)
    for b in range(nb):
        o_ref[b, 0] = out[b * per:(b + 1) * per]


def _compress(kv, cmp_pos, cmp_w1, cmp_w2, batch, seq):
    nslot = seq // CMP_STRIDE
    assert nslot & (nslot - 1) == 0
    nb = math.gcd(batch, CMP_BATCH)
    kinds = 2 * NSA_KV_GROUPS
    return pl.pallas_call(
        _compress_kernel,
        grid=(kinds, batch // nb),
        in_specs=[
            pl.BlockSpec((nb * seq, NSA_HEAD_DIM), lambda j, b: (b, j)),
            pl.BlockSpec((1, CMP_BLOCK, NSA_HEAD_DIM), lambda j, b: (j // NSA_KV_GROUPS, 0, 0)),
            pl.BlockSpec((1, CMP_BLOCK * NSA_HEAD_DIM, CMP_HIDDEN), lambda j, b: (j // NSA_KV_GROUPS, 0, 0)),
            pl.BlockSpec((1, CMP_HIDDEN, NSA_HEAD_DIM), lambda j, b: (j // NSA_KV_GROUPS, 0, 0)),
        ],
        out_specs=pl.BlockSpec((nb, 1, nslot, NSA_HEAD_DIM), lambda j, b: (b, j, 0, 0)),
        out_shape=jax.ShapeDtypeStruct((batch, kinds, nslot, NSA_HEAD_DIM), bf16),
        scratch_shapes=[pltpu.VMEM((nb * nslot * CMP_PITCH, NSA_HEAD_DIM), f32)],
        compiler_params=pltpu.CompilerParams(
            dimension_semantics=("arbitrary", "arbitrary"), vmem_limit_bytes=VMEM_LIMIT),
        name="compress",
    )(kv, cmp_pos, cmp_w1, cmp_w2)


def _nsa_tile(t0, q_ref, kc_ref, vc_ref, ks_ref, vs_ref, kw_ref, vw_ref, g_ref, ovl_ref, exp_ref, o_ref):
    n_static = (t0 + Q_BLOCK + SLC_CHUNK - 1) // SLC_CHUNK
    nslot = kc_ref.shape[2]
    n_cmp = nslot - 1
    heads = range(NSA_REP)
    hcols = lambda a, r: a[:, r * Q_BLOCK:(r + 1) * Q_BLOCK]

    qrows = slice(t0, t0 + Q_BLOCK)
    q = q_ref[qrows, :]
    q4 = jnp.concatenate([q[:, r * NSA_HEAD_DIM:(r + 1) * NSA_HEAD_DIM] for r in heads], axis=0)
    tq = t0 + lax.broadcasted_iota(jnp.int32, (1, Q_BLOCK), 1)

    def softmax_cols(s_r):
        m = _col_reduce(jnp.maximum, s_r)
        e = jnp.exp2(s_r - m)
        return e, _col_reduce(jnp.add, e)

    s = _dot_nt(kc_ref[0, 0], q4)
    start = max(t0 - WINDOW, 0)
    wlen = t0 + Q_BLOCK - start
    sw = _dot_nt(kw_ref[start:start + wlen, :], q4)
    sc0 = _dot_nt(ks_ref[0:min(SLC_CHUNK, t0 + Q_BLOCK), :], q4)
    nrow = lax.broadcasted_iota(jnp.int32, (nslot, 1), 0)
    bias_c = jnp.where((nrow * CMP_STRIDE + (CMP_BLOCK - 1) <= tq) & (nrow < n_cmp), 0.0, NEG)
    row_ok = tq >= CMP_BLOCK - 1
    ps, psum = [], None
    for r in heads:
        e, l = softmax_cols(hcols(s, r) + bias_c)
        p = e * jnp.where(row_ok, 1.0 / l, 0.0)
        ps.append(p.astype(bf16))
        psum = p if psum is None else psum + p
    o_cmp = _dot_tn(vc_ref[0, 0], jnp.concatenate(ps, axis=1))

    imp_t = jnp.dot(ovl_ref[...], psum, precision=lax.Precision.HIGHEST,
                    preferred_element_type=f32)
    n_sb = imp_t.shape[0]
    jblk = lax.broadcasted_iota(jnp.int32, (n_sb, Q_BLOCK), 0)
    tl = t0 + lax.broadcasted_iota(jnp.int32, (n_sb, Q_BLOCK), 1)
    cur = tl >> 6
    forced = (jblk == 0) | (jblk == cur) | (jblk == cur - 1)
    valid = jblk * SEL_BLOCK <= tl
    v = jnp.where(forced, jnp.inf, jnp.where(valid, imp_t, -jnp.inf))
    n_live = min(n_sb, n_static * SLC_CHUNK // SEL_BLOCK)
    slabs = [v[k:k + SUBLANES] for k in range(0, n_live, SUBLANES)]
    jl = lax.broadcasted_iota(jnp.int32, (SUBLANES, Q_BLOCK), 0)
    cnts = [jnp.zeros((SUBLANES, Q_BLOCK), f32) for _ in slabs]
    for i in range(n_live):
        vi = jnp.broadcast_to(v[i:i + 1, :], (SUBLANES, Q_BLOCK))
        for k, vk in enumerate(slabs):
            if k * SUBLANES > i:
                ahead = vi >= vk
            elif (k + 1) * SUBLANES <= i:
                ahead = vi > vk
            else:
                ahead = (vi > vk) | ((vi == vk) & (jl > i - k * SUBLANES))
            cnts[k] = cnts[k] + jnp.where(ahead, 1.0, 0.0)
    selbias_pad = jnp.concatenate([jnp.where(c < float(N_SELECT), 0.0, NEG) for c in cnts]
                                  + [jnp.zeros((LANES - n_live, Q_BLOCK), f32)], axis=0)
    q4_aug = jnp.concatenate([q4, jnp.concatenate([selbias_pad.T.astype(bf16)] * NSA_REP, axis=0)], axis=1)

    def gate_rows(br):
        return jnp.concatenate([gs[br * NSA_REP + r:br * NSA_REP + r + 1, :] for r in heads], axis=1)

    kl = lax.broadcasted_iota(jnp.int32, (Q_BLOCK, Q_BLOCK), 0)
    ql = lax.broadcasted_iota(jnp.int32, (Q_BLOCK, Q_BLOCK), 1)
    head_bias = jnp.where(kl > ql, 0.0, NEG)
    tail_bias = jnp.where(kl <= ql, 0.0, NEG)

    def add_tail_bias(a):
        body = a.shape[0] - Q_BLOCK
        tail = a[body:] + tail_bias
        return tail if body == 0 else jnp.concatenate([a[:body], tail], axis=0)

    def add_bias_w(a):
        if t0 >= WINDOW:
            a = jnp.concatenate([a[:Q_BLOCK] + head_bias, a[Q_BLOCK:]], axis=0)
        return add_tail_bias(a)

    gs = g_ref[qrows, :].T
    es, ls = [], []
    for r in heads:
        e, l = softmax_cols(add_bias_w(hcols(sw, r)))
        es.append(e.astype(bf16))
        ls.append(l)
    o_win = _dot_tn(vw_ref[start:start + wlen, :], jnp.concatenate(es, axis=1))
    o_part = (gate_rows(0) * o_cmp
              + (gate_rows(2) * (1.0 / jnp.concatenate(ls, axis=1))) * o_win)

    def slc_branch(n):
        use_sel = t0 + Q_BLOCK > N_SELECT * SEL_BLOCK
        bounds = [(c * SLC_CHUNK, min((c + 1) * SLC_CHUNK, t0 + Q_BLOCK)) for c in range(n)]

        def slc_scores(c):
            lo, hi = bounds[c]
            ks = ks_ref[lo:hi, :]
            if not use_sel:
                return _dot_nt(ks, q4)
            return _dot_nt(jnp.concatenate([ks, exp_ref[lo:hi, :]], axis=1), q4_aug)

        m_i, l_i, acc = [None] * NSA_REP, [None] * NSA_REP, None
        sc = slc_scores(0) if use_sel else sc0
        for c in range(n):
            sc_next = slc_scores(c + 1) if c + 1 < n else None
            es, alphas = [], []
            for r in heads:
                s_r = add_tail_bias(hcols(sc, r)) if c == n - 1 else hcols(sc, r)
                m_c = _col_reduce(jnp.maximum, s_r)
                if c == 0:
                    e = jnp.exp2(s_r - m_c)
                    m_i[r], l_i[r] = m_c, _col_reduce(jnp.add, e)
                else:
                    m_new = jnp.maximum(m_i[r], m_c)
                    alpha = jnp.exp2(m_i[r] - m_new)
                    e = jnp.exp2(s_r - m_new)
                    m_i[r], l_i[r] = m_new, alpha * l_i[r] + _col_reduce(jnp.add, e)
                    alphas.append(alpha)
                es.append(e.astype(bf16))
            pv = _dot_tn(vs_ref[bounds[c][0]:bounds[c][1], :], jnp.concatenate(es, axis=1))
            acc = pv if c == 0 else jnp.concatenate(alphas, axis=1) * acc + pv
            sc = sc_next
        o = o_part + (gate_rows(1) * (1.0 / jnp.concatenate(l_i, axis=1))) * acc
        for r in heads:
            o_ref[qrows, r * NSA_HEAD_DIM:(r + 1) * NSA_HEAD_DIM] = hcols(o, r).T.astype(bf16)

    slc_branch(n_static)


def _nsa_kernel(*refs):
    seq = refs[3].shape[0]
    nq = seq // Q_BLOCK

    def pair(p):
        _nsa_tile((nq - 1 - p) * Q_BLOCK, *refs)
        _nsa_tile(p * Q_BLOCK, *refs)

    for p in range(nq // 2):
        pl.when(pl.program_id(2) == p)(functools.partial(pair, p))


def _nsa(q, cmp, kv, gsig, ovl_t, expand, batch, seq):
    nq = seq // Q_BLOCK
    gw = NSA_REP * NSA_HEAD_DIM
    nslot = cmp.shape[2]
    G = NSA_KV_GROUPS
    kvspec = lambda off: pl.BlockSpec((seq, NSA_HEAD_DIM), lambda b, g, i, off=off: (b, off + g))
    return pl.pallas_call(
        _nsa_kernel,
        grid=(batch, G, nq // 2),
        in_specs=[
            pl.BlockSpec((seq, gw), lambda b, g, i: (b, g)),
            pl.BlockSpec((1, 1, nslot, NSA_HEAD_DIM), lambda b, g, i: (b, g, 0, 0)),
            pl.BlockSpec((1, 1, nslot, NSA_HEAD_DIM), lambda b, g, i: (b, G + g, 0, 0)),
            kvspec(2 * G), kvspec(3 * G), kvspec(4 * G), kvspec(5 * G),
            pl.BlockSpec((seq, GATE_PAD), lambda b, g, i: (b, g)),
            pl.BlockSpec(ovl_t.shape, lambda b, g, i: (0, 0)),
            pl.BlockSpec(expand.shape, lambda b, g, i: (0, 0)),
        ],
        out_specs=pl.BlockSpec((seq, gw), lambda b, g, i: (b, g)),
        out_shape=jax.ShapeDtypeStruct((batch * seq, NSA_WIDTH), bf16),
        compiler_params=pltpu.CompilerParams(
            dimension_semantics=("arbitrary", "arbitrary", "arbitrary"), vmem_limit_bytes=VMEM_LIMIT),
        name="nsa",
    )(q, cmp, cmp, kv, kv, kv, kv, gsig, ovl_t, expand)


def _retention_kernel(q_ref, k_ref, v_ref, gn_ref, o_ref):
    seq = q_ref.shape[0]
    C = RET_CHUNK
    n = lax.broadcasted_iota(jnp.int32, (C, C), 0)
    mcol = lax.broadcasted_iota(jnp.int32, (C, C), 1)
    diff = (n - mcol).astype(f32)
    nvec = lax.broadcasted_iota(jnp.int32, (C, 1), 0).astype(f32)
    decay, xi, zeta, gamma_c = [], [], [], []
    for h in range(RET_HEADS):
        lg = math.log1p(-(2.0 ** (-5.0 - h)))
        decay.append(jnp.where(diff >= 0, jnp.exp(lg * jnp.maximum(diff, 0.0)), 0.0))
        xi.append(jnp.exp(lg * (nvec + 1.0)))
        zeta.append(jnp.exp(lg * (C - 1.0 - nvec)))
        gamma_c.append(math.exp(lg * C))
    gn = gn_ref[...]

    def body(c, states):
        base = pl.multiple_of(c * C, C)
        hs = range(RET_HEADS)
        qc = [q_ref[pl.ds(base, C), h * RET_QK_DIM:(h + 1) * RET_QK_DIM] for h in hs]
        kc = [k_ref[pl.ds(base, C), h * RET_QK_DIM:(h + 1) * RET_QK_DIM] for h in hs]
        vc = [v_ref[pl.ds(base, C), h * RET_V_DIM:(h + 1) * RET_V_DIM] for h in hs]
        qk = [_dot_nt(qc[h], kc[h]) for h in hs]
        cross = [_dot(qc[h], states[h].astype(bf16)) for h in hs]
        kv = [_dot_tn((kc[h].astype(f32) * zeta[h]).astype(bf16), vc[h]) for h in hs]
        intra = [_dot((qk[h] * decay[h]).astype(bf16), vc[h]) for h in hs]
        for h in hs:
            out = intra[h] + cross[h] * xi[h]
            mu = jnp.mean(out, axis=-1, keepdims=True)
            d = out - mu
            var = jnp.mean(d * d, axis=-1, keepdims=True)
            o = d * lax.rsqrt(var + EPS) * gn[:, h * RET_V_DIM:(h + 1) * RET_V_DIM]
            o_ref[pl.ds(base, C), h * RET_V_DIM:(h + 1) * RET_V_DIM] = o.astype(bf16)
        return tuple(states[h] * gamma_c[h] + kv[h] for h in hs)

    s0 = tuple(jnp.zeros((RET_QK_DIM, RET_V_DIM), f32) for _ in range(RET_HEADS))
    lax.fori_loop(0, seq // C, body, s0, unroll=2)


def _retention(rq, rk, rv, ret_gn, batch, seq):
    return pl.pallas_call(
        _retention_kernel,
        grid=(batch,),
        in_specs=[
            pl.BlockSpec((seq, RET_QK_WIDTH), lambda b: (b, 0)),
            pl.BlockSpec((seq, RET_QK_WIDTH), lambda b: (b, 0)),
            pl.BlockSpec((seq, RET_WIDTH), lambda b: (b, 0)),
            pl.BlockSpec((1, RET_WIDTH), lambda b: (0, 0)),
        ],
        out_specs=pl.BlockSpec((seq, RET_WIDTH), lambda b: (b, 0)),
        out_shape=jax.ShapeDtypeStruct((batch * seq, RET_WIDTH), bf16),
        compiler_params=pltpu.CompilerParams(
            dimension_semantics=("arbitrary",), vmem_limit_bytes=VMEM_LIMIT),
        name="retention",
    )(rq, rk, rv, ret_gn)


def _out_proj_kernel(x_ref, on_ref, or_ref, mq_ref, kvm_ref, gs_ref, w_ref, g_ref, o_ref):
    off = NSA_WIDTH + RET_WIDTH
    heads = range(MEM_HEADS)
    hsl = lambda h: slice(h * MEM_HEAD_DIM, (h + 1) * MEM_HEAD_DIM)
    s = [_dot_nt(mq_ref[:, hsl(h)], kvm_ref[:, hsl(h)]) for h in heads]
    mix_a = jnp.concatenate([on_ref[...] * gs_ref[:, 0:NSA_WIDTH],
                             or_ref[...] * gs_ref[:, NSA_WIDTH:off]], axis=1)
    y = _dot(mix_a, w_ref[0:off, :])
    mix_m = []
    for h in heads:
        m = jnp.max(s[h], axis=1, keepdims=True)
        e = jnp.exp2(s[h] - m)
        l = jnp.sum(e, axis=1, keepdims=True)
        om = _dot(e.astype(bf16), kvm_ref[:, MEM_WIDTH + h * MEM_HEAD_DIM:MEM_WIDTH + (h + 1) * MEM_HEAD_DIM])
        mix_m.append((om * (1.0 / l)).astype(bf16) * gs_ref[:, off + h * MEM_HEAD_DIM:off + (h + 1) * MEM_HEAD_DIM])
    y = y + _dot(jnp.concatenate(mix_m, axis=1), w_ref[off:, :])
    o_ref[...] = x_ref[...] + _rms(y, g_ref[...])


def _out_proj(x2, o_nsa, o_ret, mq, kvm, gsilu, w_out, norm_post, seq):
    bt = x2.shape[0]
    tm = OUT_ROW_TILE
    per_b = seq // tm
    row = lambda i: (i, 0)
    const = lambda i: (0, 0)
    return pl.pallas_call(
        _out_proj_kernel,
        grid=(bt // tm,),
        in_specs=[
            pl.BlockSpec((tm, D_MODEL), row),
            pl.BlockSpec((tm, NSA_WIDTH), row),
            pl.BlockSpec((tm, RET_WIDTH), row),
            pl.BlockSpec((tm, MEM_WIDTH), row),
            pl.BlockSpec((MEM_LEN, 2 * MEM_WIDTH), lambda i: (i // per_b, 0)),
            pl.BlockSpec((tm, MIX_WIDTH), row),
            pl.BlockSpec((MIX_WIDTH, D_MODEL), const),
            pl.BlockSpec((1, D_MODEL), const),
        ],
        out_specs=pl.BlockSpec((tm, D_MODEL), row),
        out_shape=jax.ShapeDtypeStruct((bt, D_MODEL), f32),
        compiler_params=pltpu.CompilerParams(
            dimension_semantics=("arbitrary",), vmem_limit_bytes=VMEM_LIMIT),
        name="out_proj",
    )(x2, o_nsa, o_ret, mq, kvm, gsilu, w_out, norm_post)


def _relayout_w_in(w):
    o = 0
    q_nsa = w[:, o:o + NSA_WIDTH]; o += NSA_WIDTH
    kv = w[:, o:o + NSA_KV_ALL]; o += NSA_KV_ALL
    gates = w[:, o:o + 3 * NSA_HEADS]; o += 3 * NSA_HEADS
    rest = w[:, o:]
    gates = gates.reshape(D_MODEL, 3, NSA_KV_GROUPS, NSA_REP).transpose(0, 2, 1, 3)
    gates = gates.reshape(D_MODEL, NSA_KV_GROUPS, 3 * NSA_REP)
    gates = jnp.pad(gates, ((0, 0), (0, 0), (0, GATE_PAD - 3 * NSA_REP)))
    gates = gates.reshape(D_MODEL, NSA_KV_GROUPS * GATE_PAD)
    return jnp.concatenate([q_nsa, kv, rest, gates], axis=1).astype(bf16)


def _overlap_t(seq):
    n_slot = seq // CMP_STRIDE
    n_sb = seq // SEL_BLOCK
    cmp_start = np.arange(n_slot) * CMP_STRIDE
    sel_start = np.arange(n_sb) * SEL_BLOCK
    ov = np.clip(np.minimum(cmp_start[None, :] + CMP_BLOCK, sel_start[:, None] + SEL_BLOCK)
                 - np.maximum(cmp_start[None, :], sel_start[:, None]), 0, None)
    return jnp.asarray(ov.astype(np.float32) / CMP_BLOCK)


def kernel(x, mem, positions, norm_pre, w_in, cmp_pos_k, cmp_w1_k, cmp_w2_k, cmp_pos_v, cmp_w1_v,
           cmp_w2_v, ret_gn, mem_norm, w_mem_kv, w_out, norm_post):
    depth = norm_pre.shape[0]
    batch, seq, _ = x.shape
    assert x.shape[2] == D_MODEL and mem.shape[1:] == (MEM_LEN, D_MODEL)
    assert seq % SLC_CHUNK == 0 and seq >= WINDOW + Q_BLOCK

    half = ROT_DIM // 2
    inv_n = ROPE_THETA ** (-jnp.arange(half, dtype=f32) / half)
    halfr = RET_QK_DIM // 2
    inv_r = RET_THETA ** (-jnp.arange(halfr, dtype=f32) / halfr)
    assert ROT_DIM <= RET_QK_DIM and 2 * RET_QK_DIM == LANES
    inv = jnp.concatenate([inv_n, inv_n, jnp.zeros((RET_QK_DIM - ROT_DIM,), f32), inv_r, inv_r])[None, :]
    ovl_t = _overlap_t(seq)
    expand = jnp.asarray(np.arange(seq)[:, None] // SEL_BLOCK == np.arange(LANES)[None, :], bf16)
    pos2 = positions.reshape(batch * seq, 1)
    mem2 = mem.reshape(batch * MEM_LEN, D_MODEL)

    x2 = x.reshape(batch * seq, D_MODEL)
    for layer in range(depth):
        w_r = _relayout_w_in(w_in[layer])
        q, kv, rq, rk, rv, mq, gsilu, gsig = _in_proj(x2, pos2, norm_pre[layer][None, :], w_r, inv)
        kvm = _mem_kv(mem2, mem_norm[layer][None, :], w_mem_kv[layer].astype(bf16))
        cmp = _compress(
            kv,
            jnp.stack([cmp_pos_k[layer], cmp_pos_v[layer]]),
            jnp.stack([cmp_w1_k[layer], cmp_w1_v[layer]]).astype(bf16),
            jnp.stack([cmp_w2_k[layer], cmp_w2_v[layer]]).astype(bf16),
            batch, seq)
        o_nsa = _nsa(q, cmp, kv, gsig, ovl_t, expand, batch, seq)
        o_ret = _retention(rq, rk, rv, ret_gn[layer][None, :], batch, seq)
        x2 = _out_proj(x2, o_nsa, o_ret, mq, kvm, gsilu, w_out[layer].astype(bf16),
                       norm_post[layer][None, :], seq)
    return x2.reshape(batch, seq, D_MODEL)
```

```python
import functools
import math

import jax
import jax.numpy as jnp
import numpy as np
from jax import lax
from jax.experimental import pallas as pl
from jax.experimental.pallas import tpu as pltpu

D_MODEL = 1024
MEM_LEN = 256
EPS = 1e-6

NSA_HEADS = 8
NSA_KV_GROUPS = 2
NSA_REP = NSA_HEADS // NSA_KV_GROUPS
NSA_HEAD_DIM = 128
CMP_BLOCK = 32
CMP_STRIDE = 16
CMP_HIDDEN = 256
SEL_BLOCK = 64
N_SELECT = 16
WINDOW = 512
Q_BLOCK = 256
ROPE_THETA = 500000.0
ROT_DIM = NSA_HEAD_DIM // 4

RET_HEADS = 4
RET_QK_DIM = 64
RET_V_DIM = 128
RET_CHUNK = 128
RET_THETA = 10000.0

MEM_HEADS = 4
MEM_HEAD_DIM = 128

NSA_WIDTH = NSA_HEADS * NSA_HEAD_DIM
NSA_KV_ALL = 3 * 2 * NSA_KV_GROUPS * NSA_HEAD_DIM
RET_QK_WIDTH = RET_HEADS * RET_QK_DIM
RET_WIDTH = RET_HEADS * RET_V_DIM
MEM_WIDTH = MEM_HEADS * MEM_HEAD_DIM
MIX_WIDTH = NSA_WIDTH + RET_WIDTH + MEM_WIDTH
GATE_PAD = 128

LANES = 128
SUBLANES = 8
NEG = -1e30
LOG2E = math.log2(math.e)
NSA_QSCALE = NSA_HEAD_DIM ** -0.5 * LOG2E
MEM_QSCALE = MEM_HEAD_DIM ** -0.5 * LOG2E
VMEM_LIMIT = 56 * 1024 * 1024

ROW_TILE = 256
CMP_PITCH = 24
CMP_BATCH = 4
OUT_ROW_TILE = 512
NSA_PAIRS = 4
SLC_CHUNK = 512

_NT = (((1,), (1,)), ((), ()))
_TN = (((0,), (0,)), ((), ()))

bf16 = jnp.bfloat16
f32 = jnp.float32


def _dot(a, b):
    return jnp.dot(a, b, preferred_element_type=f32)


def _dot_nt(a, b):
    return lax.dot_general(a, b, _NT, preferred_element_type=f32)


def _dot_tn(a, b):
    return lax.dot_general(a, b, _TN, preferred_element_type=f32)


def _col_reduce(op, x):
    slabs = [x[i:i + SUBLANES] for i in range(0, x.shape[0], SUBLANES)]
    while len(slabs) > 1:
        nxt = [op(slabs[i], slabs[i + 1]) for i in range(0, len(slabs) - 1, 2)]
        if len(slabs) % 2:
            nxt.append(slabs[-1])
        slabs = nxt
    red = jnp.max if op is jnp.maximum else jnp.sum
    return red(slabs[0], axis=0, keepdims=True)


def _rms(x, g):
    return x * lax.rsqrt(jnp.mean(x * x, axis=-1, keepdims=True) + EPS) * g


def _rotate_heads(acc, c, s1, s2, shift):
    outs = []
    for j in range(acc.shape[1] // LANES):
        a = acc[:, j * LANES:(j + 1) * LANES]
        outs.append(a * c + pltpu.roll(a, shift, 1) * s1 + pltpu.roll(a, LANES - shift, 1) * s2)
    return outs


def _in_proj_kernel(x_ref, pos_ref, g_ref, w_ref, inv_ref,
                    q_ref, kv_ref, rq_ref, rk_ref, rv_ref, mq_ref, gs_ref, gg_ref):
    x = x_ref[...]
    hb = _rms(x, g_ref[...]).astype(bf16)
    c_kv = NSA_WIDTH
    c_rqk = c_kv + NSA_KV_ALL
    c_rv = c_rqk + 2 * RET_QK_WIDTH
    c_mq = c_rv + RET_WIDTH
    c_gate = c_mq + MEM_WIDTH
    c_gg = c_gate + MIX_WIDTH

    pos = pos_ref[...].astype(f32)
    lane = lax.broadcasted_iota(jnp.int32, (x.shape[0], LANES), 1)
    ang = pos * inv_ref[...]
    cs, sn = jnp.cos(ang), jnp.sin(ang)
    half = ROT_DIM // 2
    cn = jnp.where(lane < ROT_DIM, cs, 1.0)
    s1n = jnp.where((lane >= half) & (lane < ROT_DIM), sn, 0.0)
    s2n = jnp.where(lane < half, -sn, 0.0)
    halfr = RET_QK_DIM // 2
    cr = jnp.where(lane >= RET_QK_DIM, cs, pltpu.roll(cs, RET_QK_DIM, 1))
    sr = jnp.where(lane >= RET_QK_DIM, sn, pltpu.roll(sn, RET_QK_DIM, 1))
    lr = lane & (RET_QK_DIM - 1)
    s1r = jnp.where(lr >= halfr, sr, 0.0)
    s2r = jnp.where(lr < halfr, -sr, 0.0)

    for j in range(MIX_WIDTH // 512):
        acc = _dot(hb, w_ref[:, c_gate + j * 512:c_gate + (j + 1) * 512])
        gs_ref[:, j * 512:(j + 1) * 512] = (acc * (1.0 / (1.0 + jnp.exp(-acc)))).astype(bf16)
    mq_ref[...] = (_dot(hb, w_ref[:, c_mq:c_mq + MEM_WIDTH]) * MEM_QSCALE).astype(bf16)
    acc = _dot(hb, w_ref[:, c_gg:c_gg + 2 * GATE_PAD])
    gg_ref[...] = 1.0 / (1.0 + jnp.exp(-acc))

    for j in range(NSA_WIDTH // 512):
        acc = _dot(hb, w_ref[:, j * 512:(j + 1) * 512])
        for i, o in enumerate(_rotate_heads(acc, cn * NSA_QSCALE, s1n * NSA_QSCALE, s2n * NSA_QSCALE, half)):
            q_ref[:, j * 512 + i * LANES: j * 512 + (i + 1) * LANES] = o.astype(bf16)
    for br in range(3):
        acc = _dot(hb, w_ref[:, c_kv + br * 512:c_kv + (br + 1) * 512])
        for i, o in enumerate(_rotate_heads(acc[:, :256], cn, s1n, s2n, half)):
            kv_ref[:, br * 512 + i * LANES: br * 512 + (i + 1) * LANES] = o.astype(bf16)
        kv_ref[:, br * 512 + 256: br * 512 + 512] = acc[:, 256:].astype(bf16)
    acc = _dot(hb, w_ref[:, c_rqk:c_rqk + 2 * RET_QK_WIDTH])
    rot = _rotate_heads(acc, cr, s1r, s2r, halfr)
    for i in range(2):
        rq_ref[:, i * LANES:(i + 1) * LANES] = rot[i].astype(bf16)
        rk_ref[:, i * LANES:(i + 1) * LANES] = (rot[2 + i] * (RET_QK_DIM ** -0.5)).astype(bf16)

    rv_ref[...] = _dot(hb, w_ref[:, c_rv:c_rv + RET_WIDTH]).astype(bf16)


def _in_proj(x2, pos2, norm_pre, w_r, inv):
    bt = x2.shape[0]
    tm = ROW_TILE
    win = w_r.shape[1]
    row = lambda i: (i, 0)
    const = lambda i: (0, 0)
    widths = (NSA_WIDTH, NSA_KV_ALL, RET_QK_WIDTH, RET_QK_WIDTH, RET_WIDTH, MEM_WIDTH, MIX_WIDTH)
    out_shape = [jax.ShapeDtypeStruct((bt, w), bf16) for w in widths]
    out_shape.append(jax.ShapeDtypeStruct((bt, 2 * GATE_PAD), f32))
    out_specs = [pl.BlockSpec((tm, w), row) for w in widths] + [pl.BlockSpec((tm, 2 * GATE_PAD), row)]
    return pl.pallas_call(
        _in_proj_kernel,
        grid=(bt // tm,),
        in_specs=[
            pl.BlockSpec((tm, D_MODEL), row),
            pl.BlockSpec((tm, 1), row),
            pl.BlockSpec((1, D_MODEL), const),
            pl.BlockSpec((D_MODEL, win), const),
            pl.BlockSpec((1, LANES), const),
        ],
        out_specs=out_specs,
        out_shape=out_shape,
        compiler_params=pltpu.CompilerParams(
            dimension_semantics=("arbitrary",), vmem_limit_bytes=VMEM_LIMIT),
        name="in_proj",
    )(x2, pos2, norm_pre, w_r, inv)


def _mem_kv_kernel(m_ref, g_ref, w_ref, o_ref):
    hb = _rms(m_ref[...], g_ref[...]).astype(bf16)
    o_ref[...] = _dot(hb, w_ref[...]).astype(bf16)


def _mem_kv(mem2, mem_norm, w_kv):
    n = mem2.shape[0]
    tm = MEM_LEN
    return pl.pallas_call(
        _mem_kv_kernel,
        grid=(n // tm,),
        in_specs=[
            pl.BlockSpec((tm, D_MODEL), lambda i: (i, 0)),
            pl.BlockSpec((1, D_MODEL), lambda i: (0, 0)),
            pl.BlockSpec((D_MODEL, 2 * MEM_WIDTH), lambda i: (0, 0)),
        ],
        out_specs=pl.BlockSpec((tm, 2 * MEM_WIDTH), lambda i: (i, 0)),
        out_shape=jax.ShapeDtypeStruct((n, 2 * MEM_WIDTH), bf16),
        compiler_params=pltpu.CompilerParams(
            dimension_semantics=("arbitrary",), vmem_limit_bytes=VMEM_LIMIT),
        name="mem_kv",
    )(mem2, mem_norm, w_kv)


def _compress_kernel(kv_ref, pos_ref, w1_ref, w2_ref, o_ref, xf_ref):
    nb, _, per, _ = o_ref.shape
    nslot = nb * per
    for i in range(nslot):
        xf_ref[i * CMP_PITCH:i * CMP_PITCH + CMP_STRIDE, :] = (
            kv_ref[i * CMP_STRIDE:(i + 1) * CMP_STRIDE, :].astype(f32))
    pos = pos_ref[0]
    la, lb = [], []
    for p in range(CMP_STRIDE):
        a = xf_ref[pl.ds(p, nslot, stride=CMP_PITCH), :]
        la.append((a + pos[p:p + 1, :]).astype(bf16))
        lb.append((a + pos[CMP_STRIDE + p:CMP_STRIDE + p + 1, :]).astype(bf16))
    khalf = CMP_STRIDE * NSA_HEAD_DIM
    ha = _dot(jnp.concatenate(la, axis=1), w1_ref[0, 0:khalf, :])
    hb = _dot(jnp.concatenate(lb, axis=1), w1_ref[0, khalf:2 * khalf, :])
    h = ha + pltpu.roll(hb, nslot - 1, 0)
    h = jax.nn.gelu(h)
    out = _dot(h.astype(bf16), w2_ref[0])
    slot = lax.broadcasted_iota(jnp.int32, out.shape, 0) & (per - 1)
    out = jnp.where(slot < per - 1, out, 0.0).astype(bf16)
    for b in range(nb):
        o_ref[b, 0] = out[b * per:(b + 1) * per]


def _compress(kv, cmp_pos, cmp_w1, cmp_w2, batch, seq):
    nslot = seq // CMP_STRIDE
    assert nslot & (nslot - 1) == 0
    nb = math.gcd(batch, CMP_BATCH)
    kinds = 2 * NSA_KV_GROUPS
    return pl.pallas_call(
        _compress_kernel,
        grid=(kinds, batch // nb),
        in_specs=[
            pl.BlockSpec((nb * seq, NSA_HEAD_DIM), lambda j, b: (b, j)),
            pl.BlockSpec((1, CMP_BLOCK, NSA_HEAD_DIM), lambda j, b: (j // NSA_KV_GROUPS, 0, 0)),
            pl.BlockSpec((1, CMP_BLOCK * NSA_HEAD_DIM, CMP_HIDDEN), lambda j, b: (j // NSA_KV_GROUPS, 0, 0)),
            pl.BlockSpec((1, CMP_HIDDEN, NSA_HEAD_DIM), lambda j, b: (j // NSA_KV_GROUPS, 0, 0)),
        ],
        out_specs=pl.BlockSpec((nb, 1, nslot, NSA_HEAD_DIM), lambda j, b: (b, j, 0, 0)),
        out_shape=jax.ShapeDtypeStruct((batch, kinds, nslot, NSA_HEAD_DIM), bf16),
        scratch_shapes=[pltpu.VMEM((nb * nslot * CMP_PITCH, NSA_HEAD_DIM), f32)],
        compiler_params=pltpu.CompilerParams(
            dimension_semantics=("arbitrary", "arbitrary"), vmem_limit_bytes=VMEM_LIMIT),
        name="compress",
    )(kv, cmp_pos, cmp_w1, cmp_w2)


def _nsa_tile(t0, q_ref, kc_ref, vc_ref, ks_ref, vs_ref, kw_ref, vw_ref, g_ref, ovl_ref, exp_ref, o_ref):
    n_static = (t0 + Q_BLOCK + SLC_CHUNK - 1) // SLC_CHUNK
    nslot = kc_ref.shape[2]
    n_cmp = nslot - 1
    heads = range(NSA_REP)
    hcols = lambda a, r: a[:, r * Q_BLOCK:(r + 1) * Q_BLOCK]

    qrows = slice(t0, t0 + Q_BLOCK)
    q = q_ref[qrows, :]
    q4 = jnp.concatenate([q[:, r * NSA_HEAD_DIM:(r + 1) * NSA_HEAD_DIM] for r in heads], axis=0)
    tq = t0 + lax.broadcasted_iota(jnp.int32, (1, Q_BLOCK), 1)

    def softmax_cols(s_r):
        m = _col_reduce(jnp.maximum, s_r)
        e = jnp.exp2(s_r - m)
        return e, _col_reduce(jnp.add, e)

    s = _dot_nt(kc_ref[0, 0], q4)
    start = max(t0 - WINDOW, 0)
    wlen = t0 + Q_BLOCK - start
    sw = _dot_nt(kw_ref[start:start + wlen, :], q4)
    sc0 = _dot_nt(ks_ref[0:min(SLC_CHUNK, t0 + Q_BLOCK), :], q4)
    nrow = lax.broadcasted_iota(jnp.int32, (nslot, 1), 0)
    bias_c = jnp.where((nrow * CMP_STRIDE + (CMP_BLOCK - 1) <= tq) & (nrow < n_cmp), 0.0, NEG)
    row_ok = tq >= CMP_BLOCK - 1
    ps, psum = [], None
    for r in heads:
        e, l = softmax_cols(hcols(s, r) + bias_c)
        p = e * jnp.where(row_ok, 1.0 / l, 0.0)
        ps.append(p.astype(bf16))
        psum = p if psum is None else psum + p
    o_cmp = _dot_tn(vc_ref[0, 0], jnp.concatenate(ps, axis=1))

    imp_t = jnp.dot(ovl_ref[...], psum, precision=lax.Precision.HIGHEST,
                    preferred_element_type=f32)
    n_sb = imp_t.shape[0]
    jblk = lax.broadcasted_iota(jnp.int32, (n_sb, Q_BLOCK), 0)
    tl = t0 + lax.broadcasted_iota(jnp.int32, (n_sb, Q_BLOCK), 1)
    cur = tl >> 6
    forced = (jblk == 0) | (jblk == cur) | (jblk == cur - 1)
    valid = jblk * SEL_BLOCK <= tl
    v = jnp.where(forced, jnp.inf, jnp.where(valid, imp_t, -jnp.inf))
    n_live = min(n_sb, n_static * SLC_CHUNK // SEL_BLOCK)
    slabs = [v[k:k + SUBLANES] for k in range(0, n_live, SUBLANES)]
    jl = lax.broadcasted_iota(jnp.int32, (SUBLANES, Q_BLOCK), 0)
    cnts = [jnp.zeros((SUBLANES, Q_BLOCK), f32) for _ in slabs]
    for i in range(n_live):
        vi = jnp.broadcast_to(v[i:i + 1, :], (SUBLANES, Q_BLOCK))
        for k, vk in enumerate(slabs):
            if k * SUBLANES > i:
                ahead = vi >= vk
            elif (k + 1) * SUBLANES <= i:
                ahead = vi > vk
            else:
                ahead = (vi > vk) | ((vi == vk) & (jl > i - k * SUBLANES))
            cnts[k] = cnts[k] + jnp.where(ahead, 1.0, 0.0)
    selbias_pad = jnp.concatenate([jnp.where(c < float(N_SELECT), 0.0, NEG) for c in cnts]
                                  + [jnp.zeros((LANES - n_live, Q_BLOCK), f32)], axis=0)
    q4_aug = jnp.concatenate([q4, jnp.concatenate([selbias_pad.T.astype(bf16)] * NSA_REP, axis=0)], axis=1)

    def gate_rows(br):
        return jnp.concatenate([gs[br * NSA_REP + r:br * NSA_REP + r + 1, :] for r in heads], axis=1)

    kl = lax.broadcasted_iota(jnp.int32, (Q_BLOCK, Q_BLOCK), 0)
    ql = lax.broadcasted_iota(jnp.int32, (Q_BLOCK, Q_BLOCK), 1)
    head_bias = jnp.where(kl > ql, 0.0, NEG)
    tail_bias = jnp.where(kl <= ql, 0.0, NEG)

    def add_tail_bias(a):
        body = a.shape[0] - Q_BLOCK
        tail = a[body:] + tail_bias
        return tail if body == 0 else jnp.concatenate([a[:body], tail], axis=0)

    def add_bias_w(a):
        if t0 >= WINDOW:
            a = jnp.concatenate([a[:Q_BLOCK] + head_bias, a[Q_BLOCK:]], axis=0)
        return add_tail_bias(a)

    gs = g_ref[qrows, :].T
    es, ls = [], []
    for r in heads:
        e, l = softmax_cols(add_bias_w(hcols(sw, r)))
        es.append(e.astype(bf16))
        ls.append(l)
    o_win = _dot_tn(vw_ref[start:start + wlen, :], jnp.concatenate(es, axis=1))
    o_part = (gate_rows(0) * o_cmp
              + (gate_rows(2) * (1.0 / jnp.concatenate(ls, axis=1))) * o_win)

    def slc_branch(n):
        use_sel = t0 + Q_BLOCK > N_SELECT * SEL_BLOCK
        bounds = [(c * SLC_CHUNK, min((c + 1) * SLC_CHUNK, t0 + Q_BLOCK)) for c in range(n)]

        def slc_scores(c):
            lo, hi = bounds[c]
            ks = ks_ref[lo:hi, :]
            if not use_sel:
                return _dot_nt(ks, q4)
            return _dot_nt(jnp.concatenate([ks, exp_ref[lo:hi, :]], axis=1), q4_aug)

        m_i, l_i, acc = [None] * NSA_REP, [None] * NSA_REP, None
        sc = slc_scores(0) if use_sel else sc0
        for c in range(n):
            sc_next = slc_scores(c + 1) if c + 1 < n else None
            es, alphas = [], []
            for r in heads:
                s_r = add_tail_bias(hcols(sc, r)) if c == n - 1 else hcols(sc, r)
                m_c = _col_reduce(jnp.maximum, s_r)
                if c == 0:
                    e = jnp.exp2(s_r - m_c)
                    m_i[r], l_i[r] = m_c, _col_reduce(jnp.add, e)
                else:
                    m_new = jnp.maximum(m_i[r], m_c)
                    alpha = jnp.exp2(m_i[r] - m_new)
                    e = jnp.exp2(s_r - m_new)
                    m_i[r], l_i[r] = m_new, alpha * l_i[r] + _col_reduce(jnp.add, e)
                    alphas.append(alpha)
                es.append(e.astype(bf16))
            pv = _dot_tn(vs_ref[bounds[c][0]:bounds[c][1], :], jnp.concatenate(es, axis=1))
            acc = pv if c == 0 else jnp.concatenate(alphas, axis=1) * acc + pv
            sc = sc_next
        o = o_part + (gate_rows(1) * (1.0 / jnp.concatenate(l_i, axis=1))) * acc
        for r in heads:
            o_ref[qrows, r * NSA_HEAD_DIM:(r + 1) * NSA_HEAD_DIM] = hcols(o, r).T.astype(bf16)

    slc_branch(n_static)


def _nsa_kernel(*refs):
    seq = refs[3].shape[0]
    nq = seq // Q_BLOCK

    def pairs(step):
        for p in range(step * NSA_PAIRS, (step + 1) * NSA_PAIRS):
            _nsa_tile((nq - 1 - p) * Q_BLOCK, *refs)
            _nsa_tile(p * Q_BLOCK, *refs)

    for step in range(nq // (2 * NSA_PAIRS)):
        pl.when(pl.program_id(2) == step)(functools.partial(pairs, step))


def _nsa(q, cmp, kv, gsig, ovl_t, expand, batch, seq):
    nq = seq // Q_BLOCK
    gw = NSA_REP * NSA_HEAD_DIM
    nslot = cmp.shape[2]
    G = NSA_KV_GROUPS
    kvspec = lambda off: pl.BlockSpec((seq, NSA_HEAD_DIM), lambda b, g, i, off=off: (b, off + g))
    return pl.pallas_call(
        _nsa_kernel,
        grid=(batch, G, nq // (2 * NSA_PAIRS)),
        in_specs=[
            pl.BlockSpec((seq, gw), lambda b, g, i: (b, g)),
            pl.BlockSpec((1, 1, nslot, NSA_HEAD_DIM), lambda b, g, i: (b, g, 0, 0)),
            pl.BlockSpec((1, 1, nslot, NSA_HEAD_DIM), lambda b, g, i: (b, G + g, 0, 0)),
            kvspec(2 * G), kvspec(3 * G), kvspec(4 * G), kvspec(5 * G),
            pl.BlockSpec((seq, GATE_PAD), lambda b, g, i: (b, g)),
            pl.BlockSpec(ovl_t.shape, lambda b, g, i: (0, 0)),
            pl.BlockSpec(expand.shape, lambda b, g, i: (0, 0)),
        ],
        out_specs=pl.BlockSpec((seq, gw), lambda b, g, i: (b, g)),
        out_shape=jax.ShapeDtypeStruct((batch * seq, NSA_WIDTH), bf16),
        compiler_params=pltpu.CompilerParams(
            dimension_semantics=("arbitrary", "arbitrary", "arbitrary"), vmem_limit_bytes=VMEM_LIMIT),
        name="nsa",
    )(q, cmp, cmp, kv, kv, kv, kv, gsig, ovl_t, expand)


def _retention_kernel(q_ref, k_ref, v_ref, gn_ref, o_ref):
    seq = q_ref.shape[0]
    C = RET_CHUNK
    n = lax.broadcasted_iota(jnp.int32, (C, C), 0)
    mcol = lax.broadcasted_iota(jnp.int32, (C, C), 1)
    diff = (n - mcol).astype(f32)
    nvec = lax.broadcasted_iota(jnp.int32, (C, 1), 0).astype(f32)
    decay, xi, zeta, gamma_c = [], [], [], []
    for h in range(RET_HEADS):
        lg = math.log1p(-(2.0 ** (-5.0 - h)))
        decay.append(jnp.where(diff >= 0, jnp.exp(lg * jnp.maximum(diff, 0.0)), 0.0))
        xi.append(jnp.exp(lg * (nvec + 1.0)))
        zeta.append(jnp.exp(lg * (C - 1.0 - nvec)))
        gamma_c.append(math.exp(lg * C))
    gn = gn_ref[...]

    def body(c, states):
        base = pl.multiple_of(c * C, C)
        hs = range(RET_HEADS)
        qc = [q_ref[pl.ds(base, C), h * RET_QK_DIM:(h + 1) * RET_QK_DIM] for h in hs]
        kc = [k_ref[pl.ds(base, C), h * RET_QK_DIM:(h + 1) * RET_QK_DIM] for h in hs]
        vc = [v_ref[pl.ds(base, C), h * RET_V_DIM:(h + 1) * RET_V_DIM] for h in hs]
        qk = [_dot_nt(qc[h], kc[h]) for h in hs]
        cross = [_dot(qc[h], states[h].astype(bf16)) for h in hs]
        kv = [_dot_tn((kc[h].astype(f32) * zeta[h]).astype(bf16), vc[h]) for h in hs]
        intra = [_dot((qk[h] * decay[h]).astype(bf16), vc[h]) for h in hs]
        for h in hs:
            out = intra[h] + cross[h] * xi[h]
            mu = jnp.mean(out, axis=-1, keepdims=True)
            d = out - mu
            var = jnp.mean(d * d, axis=-1, keepdims=True)
            o = d * lax.rsqrt(var + EPS) * gn[:, h * RET_V_DIM:(h + 1) * RET_V_DIM]
            o_ref[pl.ds(base, C), h * RET_V_DIM:(h + 1) * RET_V_DIM] = o.astype(bf16)
        return tuple(states[h] * gamma_c[h] + kv[h] for h in hs)

    s0 = tuple(jnp.zeros((RET_QK_DIM, RET_V_DIM), f32) for _ in range(RET_HEADS))
    lax.fori_loop(0, seq // C, body, s0, unroll=2)


def _retention(rq, rk, rv, ret_gn, batch, seq):
    return pl.pallas_call(
        _retention_kernel,
        grid=(batch,),
        in_specs=[
            pl.BlockSpec((seq, RET_QK_WIDTH), lambda b: (b, 0)),
            pl.BlockSpec((seq, RET_QK_WIDTH), lambda b: (b, 0)),
            pl.BlockSpec((seq, RET_WIDTH), lambda b: (b, 0)),
            pl.BlockSpec((1, RET_WIDTH), lambda b: (0, 0)),
        ],
        out_specs=pl.BlockSpec((seq, RET_WIDTH), lambda b: (b, 0)),
        out_shape=jax.ShapeDtypeStruct((batch * seq, RET_WIDTH), bf16),
        compiler_params=pltpu.CompilerParams(
            dimension_semantics=("arbitrary",), vmem_limit_bytes=VMEM_LIMIT),
        name="retention",
    )(rq, rk, rv, ret_gn)


def _out_proj_kernel(x_ref, on_ref, or_ref, mq_ref, kvm_ref, gs_ref, w_ref, g_ref, o_ref):
    off = NSA_WIDTH + RET_WIDTH
    heads = range(MEM_HEADS)
    hsl = lambda h: slice(h * MEM_HEAD_DIM, (h + 1) * MEM_HEAD_DIM)
    s = [_dot_nt(mq_ref[:, hsl(h)], kvm_ref[:, hsl(h)]) for h in heads]
    mix_a = jnp.concatenate([on_ref[...] * gs_ref[:, 0:NSA_WIDTH],
                             or_ref[...] * gs_ref[:, NSA_WIDTH:off]], axis=1)
    y = _dot(mix_a, w_ref[0:off, :])
    mix_m = []
    for h in heads:
        m = jnp.max(s[h], axis=1, keepdims=True)
        e = jnp.exp2(s[h] - m)
        l = jnp.sum(e, axis=1, keepdims=True)
        om = _dot(e.astype(bf16), kvm_ref[:, MEM_WIDTH + h * MEM_HEAD_DIM:MEM_WIDTH + (h + 1) * MEM_HEAD_DIM])
        mix_m.append((om * (1.0 / l)).astype(bf16) * gs_ref[:, off + h * MEM_HEAD_DIM:off + (h + 1) * MEM_HEAD_DIM])
    y = y + _dot(jnp.concatenate(mix_m, axis=1), w_ref[off:, :])
    o_ref[...] = x_ref[...] + _rms(y, g_ref[...])


def _out_proj(x2, o_nsa, o_ret, mq, kvm, gsilu, w_out, norm_post, seq):
    bt = x2.shape[0]
    tm = OUT_ROW_TILE
    per_b = seq // tm
    row = lambda i: (i, 0)
    const = lambda i: (0, 0)
    return pl.pallas_call(
        _out_proj_kernel,
        grid=(bt // tm,),
        in_specs=[
            pl.BlockSpec((tm, D_MODEL), row),
            pl.BlockSpec((tm, NSA_WIDTH), row),
            pl.BlockSpec((tm, RET_WIDTH), row),
            pl.BlockSpec((tm, MEM_WIDTH), row),
            pl.BlockSpec((MEM_LEN, 2 * MEM_WIDTH), lambda i: (i // per_b, 0)),
            pl.BlockSpec((tm, MIX_WIDTH), row),
            pl.BlockSpec((MIX_WIDTH, D_MODEL), const),
            pl.BlockSpec((1, D_MODEL), const),
        ],
        out_specs=pl.BlockSpec((tm, D_MODEL), row),
        out_shape=jax.ShapeDtypeStruct((bt, D_MODEL), f32),
        compiler_params=pltpu.CompilerParams(
            dimension_semantics=("arbitrary",), vmem_limit_bytes=VMEM_LIMIT),
        name="out_proj",
    )(x2, o_nsa, o_ret, mq, kvm, gsilu, w_out, norm_post)


def _relayout_w_in(w):
    o = 0
    q_nsa = w[:, o:o + NSA_WIDTH]; o += NSA_WIDTH
    kv = w[:, o:o + NSA_KV_ALL]; o += NSA_KV_ALL
    gates = w[:, o:o + 3 * NSA_HEADS]; o += 3 * NSA_HEADS
    rest = w[:, o:]
    gates = gates.reshape(D_MODEL, 3, NSA_KV_GROUPS, NSA_REP).transpose(0, 2, 1, 3)
    gates = gates.reshape(D_MODEL, NSA_KV_GROUPS, 3 * NSA_REP)
    gates = jnp.pad(gates, ((0, 0), (0, 0), (0, GATE_PAD - 3 * NSA_REP)))
    gates = gates.reshape(D_MODEL, NSA_KV_GROUPS * GATE_PAD)
    return jnp.concatenate([q_nsa, kv, rest, gates], axis=1).astype(bf16)


def _overlap_t(seq):
    n_slot = seq // CMP_STRIDE
    n_sb = seq // SEL_BLOCK
    cmp_start = np.arange(n_slot) * CMP_STRIDE
    sel_start = np.arange(n_sb) * SEL_BLOCK
    ov = np.clip(np.minimum(cmp_start[None, :] + CMP_BLOCK, sel_start[:, None] + SEL_BLOCK)
                 - np.maximum(cmp_start[None, :], sel_start[:, None]), 0, None)
    return jnp.asarray(ov.astype(np.float32) / CMP_BLOCK)


def kernel(x, mem, positions, norm_pre, w_in, cmp_pos_k, cmp_w1_k, cmp_w2_k, cmp_pos_v, cmp_w1_v,
           cmp_w2_v, ret_gn, mem_norm, w_mem_kv, w_out, norm_post):
    depth = norm_pre.shape[0]
    batch, seq, _ = x.shape
    assert x.shape[2] == D_MODEL and mem.shape[1:] == (MEM_LEN, D_MODEL)
    assert seq % SLC_CHUNK == 0 and seq >= WINDOW + Q_BLOCK

    half = ROT_DIM // 2
    inv_n = ROPE_THETA ** (-jnp.arange(half, dtype=f32) / half)
    halfr = RET_QK_DIM // 2
    inv_r = RET_THETA ** (-jnp.arange(halfr, dtype=f32) / halfr)
    assert ROT_DIM <= RET_QK_DIM and 2 * RET_QK_DIM == LANES
    inv = jnp.concatenate([inv_n, inv_n, jnp.zeros((RET_QK_DIM - ROT_DIM,), f32), inv_r, inv_r])[None, :]
    ovl_t = _overlap_t(seq)
    expand = jnp.asarray(np.arange(seq)[:, None] // SEL_BLOCK == np.arange(LANES)[None, :], bf16)
    pos2 = positions.reshape(batch * seq, 1)
    mem2 = mem.reshape(batch * MEM_LEN, D_MODEL)

    x2 = x.reshape(batch * seq, D_MODEL)
    for layer in range(depth):
        w_r = _relayout_w_in(w_in[layer])
        q, kv, rq, rk, rv, mq, gsilu, gsig = _in_proj(x2, pos2, norm_pre[layer][None, :], w_r, inv)
        kvm = _mem_kv(mem2, mem_norm[layer][None, :], w_mem_kv[layer].astype(bf16))
        cmp = _compress(
            kv,
            jnp.stack([cmp_pos_k[layer], cmp_pos_v[layer]]),
            jnp.stack([cmp_w1_k[layer], cmp_w1_v[layer]]).astype(bf16),
            jnp.stack([cmp_w2_k[layer], cmp_w2_v[layer]]).astype(bf16),
            batch, seq)
        o_nsa = _nsa(q, cmp, kv, gsig, ovl_t, expand, batch, seq)
        o_ret = _retention(rq, rk, rv, ret_gn[layer][None, :], batch, seq)
        x2 = _out_proj(x2, o_nsa, o_ret, mq, kvm, gsilu, w_out[layer].astype(bf16),
                       norm_post[layer][None, :], seq)
    return x2.reshape(batch, seq, D_MODEL)
```

```python
import functools
import math

import jax
import jax.numpy as jnp
import numpy as np
from jax import lax
from jax.experimental import pallas as pl
from jax.experimental.pallas import tpu as pltpu

D_MODEL = 1024
MEM_LEN = 256
EPS = 1e-6

NSA_HEADS = 8
NSA_KV_GROUPS = 2
NSA_REP = NSA_HEADS // NSA_KV_GROUPS
NSA_HEAD_DIM = 128
CMP_BLOCK = 32
CMP_STRIDE = 16
CMP_HIDDEN = 256
SEL_BLOCK = 64
N_SELECT = 16
WINDOW = 512
Q_BLOCK = 256
ROPE_THETA = 500000.0
ROT_DIM = NSA_HEAD_DIM // 4

RET_HEADS = 4
RET_QK_DIM = 64
RET_V_DIM = 128
RET_CHUNK = 128
RET_THETA = 10000.0

MEM_HEADS = 4
MEM_HEAD_DIM = 128

NSA_WIDTH = NSA_HEADS * NSA_HEAD_DIM
NSA_KV_ALL = 3 * 2 * NSA_KV_GROUPS * NSA_HEAD_DIM
RET_QK_WIDTH = RET_HEADS * RET_QK_DIM
RET_WIDTH = RET_HEADS * RET_V_DIM
MEM_WIDTH = MEM_HEADS * MEM_HEAD_DIM
MIX_WIDTH = NSA_WIDTH + RET_WIDTH + MEM_WIDTH
GATE_PAD = 128

LANES = 128
SUBLANES = 8
NEG = -1e30
LOG2E = math.log2(math.e)
NSA_QSCALE = NSA_HEAD_DIM ** -0.5 * LOG2E
MEM_QSCALE = MEM_HEAD_DIM ** -0.5 * LOG2E
VMEM_LIMIT = 56 * 1024 * 1024

ROW_TILE = 256
CMP_PITCH = 24
CMP_BATCH = 4
OUT_SUBTILES = 2
OUT_ROW_TILE = 512
NSA_PAIRS = 4
SLC_CHUNK = 512

_NT = (((1,), (1,)), ((), ()))
_TN = (((0,), (0,)), ((), ()))

bf16 = jnp.bfloat16
f32 = jnp.float32


def _dot(a, b):
    return jnp.dot(a, b, preferred_element_type=f32)


def _dot_nt(a, b):
    return lax.dot_general(a, b, _NT, preferred_element_type=f32)


def _dot_tn(a, b):
    return lax.dot_general(a, b, _TN, preferred_element_type=f32)


def _col_reduce(op, x):
    slabs = [x[i:i + SUBLANES] for i in range(0, x.shape[0], SUBLANES)]
    while len(slabs) > 1:
        nxt = [op(slabs[i], slabs[i + 1]) for i in range(0, len(slabs) - 1, 2)]
        if len(slabs) % 2:
            nxt.append(slabs[-1])
        slabs = nxt
    red = jnp.max if op is jnp.maximum else jnp.sum
    return red(slabs[0], axis=0, keepdims=True)


def _rms(x, g):
    return x * lax.rsqrt(jnp.mean(x * x, axis=-1, keepdims=True) + EPS) * g


def _rotate_heads(acc, c, s1, s2, shift):
    outs = []
    for j in range(acc.shape[1] // LANES):
        a = acc[:, j * LANES:(j + 1) * LANES]
        outs.append(a * c + pltpu.roll(a, shift, 1) * s1 + pltpu.roll(a, LANES - shift, 1) * s2)
    return outs


def _in_proj_kernel(x_ref, pos_ref, g_ref, w_ref, inv_ref,
                    q_ref, kv_ref, rq_ref, rk_ref, rv_ref, mq_ref, gs_ref, gg_ref):
    x = x_ref[...]
    hb = _rms(x, g_ref[...]).astype(bf16)
    c_kv = NSA_WIDTH
    c_rqk = c_kv + NSA_KV_ALL
    c_rv = c_rqk + 2 * RET_QK_WIDTH
    c_mq = c_rv + RET_WIDTH
    c_gate = c_mq + MEM_WIDTH
    c_gg = c_gate + MIX_WIDTH

    pos = pos_ref[...].astype(f32)
    lane = lax.broadcasted_iota(jnp.int32, (x.shape[0], LANES), 1)
    ang = pos * inv_ref[...]
    cs, sn = jnp.cos(ang), jnp.sin(ang)
    half = ROT_DIM // 2
    cn = jnp.where(lane < ROT_DIM, cs, 1.0)
    s1n = jnp.where((lane >= half) & (lane < ROT_DIM), sn, 0.0)
    s2n = jnp.where(lane < half, -sn, 0.0)
    halfr = RET_QK_DIM // 2
    cr = jnp.where(lane >= RET_QK_DIM, cs, pltpu.roll(cs, RET_QK_DIM, 1))
    sr = jnp.where(lane >= RET_QK_DIM, sn, pltpu.roll(sn, RET_QK_DIM, 1))
    lr = lane & (RET_QK_DIM - 1)
    s1r = jnp.where(lr >= halfr, sr, 0.0)
    s2r = jnp.where(lr < halfr, -sr, 0.0)

    for j in range(MIX_WIDTH // 512):
        acc = _dot(hb, w_ref[:, c_gate + j * 512:c_gate + (j + 1) * 512])
        gs_ref[:, j * 512:(j + 1) * 512] = (acc * (1.0 / (1.0 + jnp.exp(-acc)))).astype(bf16)
    mq_ref[...] = (_dot(hb, w_ref[:, c_mq:c_mq + MEM_WIDTH]) * MEM_QSCALE).astype(bf16)
    acc = _dot(hb, w_ref[:, c_gg:c_gg + 2 * GATE_PAD])
    gg_ref[...] = 1.0 / (1.0 + jnp.exp(-acc))

    for j in range(NSA_WIDTH // 512):
        acc = _dot(hb, w_ref[:, j * 512:(j + 1) * 512])
        for i, o in enumerate(_rotate_heads(acc, cn * NSA_QSCALE, s1n * NSA_QSCALE, s2n * NSA_QSCALE, half)):
            q_ref[:, j * 512 + i * LANES: j * 512 + (i + 1) * LANES] = o.astype(bf16)
    for br in range(3):
        acc = _dot(hb, w_ref[:, c_kv + br * 512:c_kv + (br + 1) * 512])
        for i, o in enumerate(_rotate_heads(acc[:, :256], cn, s1n, s2n, half)):
            kv_ref[:, br * 512 + i * LANES: br * 512 + (i + 1) * LANES] = o.astype(bf16)
        kv_ref[:, br * 512 + 256: br * 512 + 512] = acc[:, 256:].astype(bf16)
    acc = _dot(hb, w_ref[:, c_rqk:c_rqk + 2 * RET_QK_WIDTH])
    rot = _rotate_heads(acc, cr, s1r, s2r, halfr)
    for i in range(2):
        rq_ref[:, i * LANES:(i + 1) * LANES] = rot[i].astype(bf16)
        rk_ref[:, i * LANES:(i + 1) * LANES] = (rot[2 + i] * (RET_QK_DIM ** -0.5)).astype(bf16)

    rv_ref[...] = _dot(hb, w_ref[:, c_rv:c_rv + RET_WIDTH]).astype(bf16)


def _in_proj(x2, pos2, norm_pre, w_r, inv):
    bt = x2.shape[0]
    tm = ROW_TILE
    win = w_r.shape[1]
    row = lambda i: (i, 0)
    const = lambda i: (0, 0)
    widths = (NSA_WIDTH, NSA_KV_ALL, RET_QK_WIDTH, RET_QK_WIDTH, RET_WIDTH, MEM_WIDTH, MIX_WIDTH)
    out_shape = [jax.ShapeDtypeStruct((bt, w), bf16) for w in widths]
    out_shape.append(jax.ShapeDtypeStruct((bt, 2 * GATE_PAD), f32))
    out_specs = [pl.BlockSpec((tm, w), row) for w in widths] + [pl.BlockSpec((tm, 2 * GATE_PAD), row)]
    return pl.pallas_call(
        _in_proj_kernel,
        grid=(bt // tm,),
        in_specs=[
            pl.BlockSpec((tm, D_MODEL), row),
            pl.BlockSpec((tm, 1), row),
            pl.BlockSpec((1, D_MODEL), const),
            pl.BlockSpec((D_MODEL, win), const),
            pl.BlockSpec((1, LANES), const),
        ],
        out_specs=out_specs,
        out_shape=out_shape,
        compiler_params=pltpu.CompilerParams(
            dimension_semantics=("arbitrary",), vmem_limit_bytes=VMEM_LIMIT),
        name="in_proj",
    )(x2, pos2, norm_pre, w_r, inv)


def _mem_kv_kernel(m_ref, g_ref, w_ref, o_ref):
    hb = _rms(m_ref[...], g_ref[...]).astype(bf16)
    o_ref[...] = _dot(hb, w_ref[...]).astype(bf16)


def _mem_kv(mem2, mem_norm, w_kv):
    n = mem2.shape[0]
    tm = math.gcd(n, 4 * MEM_LEN)
    return pl.pallas_call(
        _mem_kv_kernel,
        grid=(n // tm,),
        in_specs=[
            pl.BlockSpec((tm, D_MODEL), lambda i: (i, 0)),
            pl.BlockSpec((1, D_MODEL), lambda i: (0, 0)),
            pl.BlockSpec((D_MODEL, 2 * MEM_WIDTH), lambda i: (0, 0)),
        ],
        out_specs=pl.BlockSpec((tm, 2 * MEM_WIDTH), lambda i: (i, 0)),
        out_shape=jax.ShapeDtypeStruct((n, 2 * MEM_WIDTH), bf16),
        compiler_params=pltpu.CompilerParams(
            dimension_semantics=("arbitrary",), vmem_limit_bytes=VMEM_LIMIT),
        name="mem_kv",
    )(mem2, mem_norm, w_kv)


def _compress_kernel(kv_ref, pos_ref, w1_ref, w2_ref, o_ref, xf_ref):
    nb, _, per, _ = o_ref.shape
    nslot = nb * per
    for i in range(nslot):
        xf_ref[i * CMP_PITCH:i * CMP_PITCH + CMP_STRIDE, :] = (
            kv_ref[i * CMP_STRIDE:(i + 1) * CMP_STRIDE, :].astype(f32))
    pos = pos_ref[0]
    la, lb = [], []
    for p in range(CMP_STRIDE):
        a = xf_ref[pl.ds(p, nslot, stride=CMP_PITCH), :]
        la.append((a + pos[p:p + 1, :]).astype(bf16))
        lb.append((a + pos[CMP_STRIDE + p:CMP_STRIDE + p + 1, :]).astype(bf16))
    khalf = CMP_STRIDE * NSA_HEAD_DIM
    ha = _dot(jnp.concatenate(la, axis=1), w1_ref[0, 0:khalf, :])
    hb = _dot(jnp.concatenate(lb, axis=1), w1_ref[0, khalf:2 * khalf, :])
    h = ha + pltpu.roll(hb, nslot - 1, 0)
    h = jax.nn.gelu(h)
    out = _dot(h.astype(bf16), w2_ref[0])
    slot = lax.broadcasted_iota(jnp.int32, out.shape, 0) & (per - 1)
    out = jnp.where(slot < per - 1, out, 0.0).astype(bf16)
    for b in range(nb):
        o_ref[b, 0] = out[b * per:(b + 1) * per]


def _compress(kv, cmp_pos, cmp_w1, cmp_w2, batch, seq):
    nslot = seq // CMP_STRIDE
    assert nslot & (nslot - 1) == 0
    nb = math.gcd(batch, CMP_BATCH)
    kinds = 2 * NSA_KV_GROUPS
    return pl.pallas_call(
        _compress_kernel,
        grid=(kinds, batch // nb),
        in_specs=[
            pl.BlockSpec((nb * seq, NSA_HEAD_DIM), lambda j, b: (b, j)),
            pl.BlockSpec((1, CMP_BLOCK, NSA_HEAD_DIM), lambda j, b: (j // NSA_KV_GROUPS, 0, 0)),
            pl.BlockSpec((1, CMP_BLOCK * NSA_HEAD_DIM, CMP_HIDDEN), lambda j, b: (j // NSA_KV_GROUPS, 0, 0)),
            pl.BlockSpec((1, CMP_HIDDEN, NSA_HEAD_DIM), lambda j, b: (j // NSA_KV_GROUPS, 0, 0)),
        ],
        out_specs=pl.BlockSpec((nb, 1, nslot, NSA_HEAD_DIM), lambda j, b: (b, j, 0, 0)),
        out_shape=jax.ShapeDtypeStruct((batch, kinds, nslot, NSA_HEAD_DIM), bf16),
        scratch_shapes=[pltpu.VMEM((nb * nslot * CMP_PITCH, NSA_HEAD_DIM), f32)],
        compiler_params=pltpu.CompilerParams(
            dimension_semantics=("arbitrary", "arbitrary"), vmem_limit_bytes=VMEM_LIMIT),
        name="compress",
    )(kv, cmp_pos, cmp_w1, cmp_w2)


def _nsa_tile(t0, q_ref, kc_ref, vc_ref, ks_ref, vs_ref, kw_ref, vw_ref, g_ref, ovl_ref, exp_ref, o_ref):
    n_static = (t0 + Q_BLOCK + SLC_CHUNK - 1) // SLC_CHUNK
    nslot = kc_ref.shape[2]
    n_cmp = nslot - 1
    heads = range(NSA_REP)
    hcols = lambda a, r: a[:, r * Q_BLOCK:(r + 1) * Q_BLOCK]

    qrows = slice(t0, t0 + Q_BLOCK)
    q = q_ref[qrows, :]
    q4 = jnp.concatenate([q[:, r * NSA_HEAD_DIM:(r + 1) * NSA_HEAD_DIM] for r in heads], axis=0)
    tq = t0 + lax.broadcasted_iota(jnp.int32, (1, Q_BLOCK), 1)

    def softmax_cols(s_r):
        m = _col_reduce(jnp.maximum, s_r)
        e = jnp.exp2(s_r - m)
        return e, _col_reduce(jnp.add, e)

    s = _dot_nt(kc_ref[0, 0], q4)
    start = max(t0 - WINDOW, 0)
    wlen = t0 + Q_BLOCK - start
    sw = _dot_nt(kw_ref[start:start + wlen, :], q4)
    sc0 = _dot_nt(ks_ref[0:min(SLC_CHUNK, t0 + Q_BLOCK), :], q4)
    nrow = lax.broadcasted_iota(jnp.int32, (nslot, 1), 0)
    bias_c = jnp.where((nrow * CMP_STRIDE + (CMP_BLOCK - 1) <= tq) & (nrow < n_cmp), 0.0, NEG)
    row_ok = tq >= CMP_BLOCK - 1
    ps, psum = [], None
    for r in heads:
        e, l = softmax_cols(hcols(s, r) + bias_c)
        p = e * jnp.where(row_ok, 1.0 / l, 0.0)
        ps.append(p.astype(bf16))
        psum = p if psum is None else psum + p
    o_cmp = _dot_tn(vc_ref[0, 0], jnp.concatenate(ps, axis=1))

    imp_t = jnp.dot(ovl_ref[...], psum, precision=lax.Precision.HIGHEST,
                    preferred_element_type=f32)
    n_sb = imp_t.shape[0]
    jblk = lax.broadcasted_iota(jnp.int32, (n_sb, Q_BLOCK), 0)
    tl = t0 + lax.broadcasted_iota(jnp.int32, (n_sb, Q_BLOCK), 1)
    cur = tl >> 6
    forced = (jblk == 0) | (jblk == cur) | (jblk == cur - 1)
    valid = jblk * SEL_BLOCK <= tl
    v = jnp.where(forced, jnp.inf, jnp.where(valid, imp_t, -jnp.inf))
    n_live = min(n_sb, n_static * SLC_CHUNK // SEL_BLOCK)
    slabs = [v[k:k + SUBLANES] for k in range(0, n_live, SUBLANES)]
    jl = lax.broadcasted_iota(jnp.int32, (SUBLANES, Q_BLOCK), 0)
    cnts = [jnp.zeros((SUBLANES, Q_BLOCK), f32) for _ in slabs]
    for i in range(n_live):
        vi = jnp.broadcast_to(v[i:i + 1, :], (SUBLANES, Q_BLOCK))
        for k, vk in enumerate(slabs):
            if k * SUBLANES > i:
                ahead = vi >= vk
            elif (k + 1) * SUBLANES <= i:
                ahead = vi > vk
            else:
                ahead = (vi > vk) | ((vi == vk) & (jl > i - k * SUBLANES))
            cnts[k] = cnts[k] + jnp.where(ahead, 1.0, 0.0)
    selbias_pad = jnp.concatenate([jnp.where(c < float(N_SELECT), 0.0, NEG) for c in cnts]
                                  + [jnp.zeros((LANES - n_live, Q_BLOCK), f32)], axis=0)
    q4_aug = jnp.concatenate([q4, jnp.concatenate([selbias_pad.T.astype(bf16)] * NSA_REP, axis=0)], axis=1)

    def gate_rows(br):
        return jnp.concatenate([gs[br * NSA_REP + r:br * NSA_REP + r + 1, :] for r in heads], axis=1)

    kl = lax.broadcasted_iota(jnp.int32, (Q_BLOCK, Q_BLOCK), 0)
    ql = lax.broadcasted_iota(jnp.int32, (Q_BLOCK, Q_BLOCK), 1)
    head_bias = jnp.where(kl > ql, 0.0, NEG)
    tail_bias = jnp.where(kl <= ql, 0.0, NEG)

    def add_tail_bias(a):
        body = a.shape[0] - Q_BLOCK
        tail = a[body:] + tail_bias
        return tail if body == 0 else jnp.concatenate([a[:body], tail], axis=0)

    def add_bias_w(a):
        if t0 >= WINDOW:
            a = jnp.concatenate([a[:Q_BLOCK] + head_bias, a[Q_BLOCK:]], axis=0)
        return add_tail_bias(a)

    gs = g_ref[qrows, :].T
    es, ls = [], []
    for r in heads:
        e, l = softmax_cols(add_bias_w(hcols(sw, r)))
        es.append(e.astype(bf16))
        ls.append(l)
    o_win = _dot_tn(vw_ref[start:start + wlen, :], jnp.concatenate(es, axis=1))
    o_part = (gate_rows(0) * o_cmp
              + (gate_rows(2) * (1.0 / jnp.concatenate(ls, axis=1))) * o_win)

    def slc_branch(n):
        use_sel = t0 + Q_BLOCK > N_SELECT * SEL_BLOCK
        bounds = [(c * SLC_CHUNK, min((c + 1) * SLC_CHUNK, t0 + Q_BLOCK)) for c in range(n)]

        def slc_scores(c):
            lo, hi = bounds[c]
            ks = ks_ref[lo:hi, :]
            if not use_sel:
                return _dot_nt(ks, q4)
            return _dot_nt(jnp.concatenate([ks, exp_ref[lo:hi, :]], axis=1), q4_aug)

        m_i, l_i, acc = [None] * NSA_REP, [None] * NSA_REP, None
        sc = slc_scores(0) if use_sel else sc0
        for c in range(n):
            sc_next = slc_scores(c + 1) if c + 1 < n else None
            es, alphas = [], []
            for r in heads:
                s_r = add_tail_bias(hcols(sc, r)) if c == n - 1 else hcols(sc, r)
                m_c = _col_reduce(jnp.maximum, s_r)
                if c == 0:
                    e = jnp.exp2(s_r - m_c)
                    m_i[r], l_i[r] = m_c, _col_reduce(jnp.add, e)
                else:
                    m_new = jnp.maximum(m_i[r], m_c)
                    alpha = jnp.exp2(m_i[r] - m_new)
                    e = jnp.exp2(s_r - m_new)
                    m_i[r], l_i[r] = m_new, alpha * l_i[r] + _col_reduce(jnp.add, e)
                    alphas.append(alpha)
                es.append(e.astype(bf16))
            pv = _dot_tn(vs_ref[bounds[c][0]:bounds[c][1], :], jnp.concatenate(es, axis=1))
            acc = pv if c == 0 else jnp.concatenate(alphas, axis=1) * acc + pv
            sc = sc_next
        o = o_part + (gate_rows(1) * (1.0 / jnp.concatenate(l_i, axis=1))) * acc
        for r in heads:
            o_ref[qrows, r * NSA_HEAD_DIM:(r + 1) * NSA_HEAD_DIM] = hcols(o, r).T.astype(bf16)

    slc_branch(n_static)


def _nsa_kernel(*refs):
    seq = refs[3].shape[0]
    nq = seq // Q_BLOCK

    def pairs(step):
        for p in range(step * NSA_PAIRS, (step + 1) * NSA_PAIRS):
            _nsa_tile((nq - 1 - p) * Q_BLOCK, *refs)
            _nsa_tile(p * Q_BLOCK, *refs)

    for step in range(nq // (2 * NSA_PAIRS)):
        pl.when(pl.program_id(2) == step)(functools.partial(pairs, step))


def _nsa(q, cmp, kv, gsig, ovl_t, expand, batch, seq):
    nq = seq // Q_BLOCK
    gw = NSA_REP * NSA_HEAD_DIM
    nslot = cmp.shape[2]
    G = NSA_KV_GROUPS
    kvspec = lambda off: pl.BlockSpec((seq, NSA_HEAD_DIM), lambda b, g, i, off=off: (b, off + g))
    return pl.pallas_call(
        _nsa_kernel,
        grid=(batch, G, nq // (2 * NSA_PAIRS)),
        in_specs=[
            pl.BlockSpec((seq, gw), lambda b, g, i: (b, g)),
            pl.BlockSpec((1, 1, nslot, NSA_HEAD_DIM), lambda b, g, i: (b, g, 0, 0)),
            pl.BlockSpec((1, 1, nslot, NSA_HEAD_DIM), lambda b, g, i: (b, G + g, 0, 0)),
            kvspec(2 * G), kvspec(3 * G), kvspec(4 * G), kvspec(5 * G),
            pl.BlockSpec((seq, GATE_PAD), lambda b, g, i: (b, g)),
            pl.BlockSpec(ovl_t.shape, lambda b, g, i: (0, 0)),
            pl.BlockSpec(expand.shape, lambda b, g, i: (0, 0)),
        ],
        out_specs=pl.BlockSpec((seq, gw), lambda b, g, i: (b, g)),
        out_shape=jax.ShapeDtypeStruct((batch * seq, NSA_WIDTH), bf16),
        compiler_params=pltpu.CompilerParams(
            dimension_semantics=("arbitrary", "arbitrary", "arbitrary"), vmem_limit_bytes=VMEM_LIMIT),
        name="nsa",
    )(q, cmp, cmp, kv, kv, kv, kv, gsig, ovl_t, expand)


def _retention_kernel(q_ref, k_ref, v_ref, gn_ref, o_ref):
    seq = q_ref.shape[0]
    C = RET_CHUNK
    n = lax.broadcasted_iota(jnp.int32, (C, C), 0)
    mcol = lax.broadcasted_iota(jnp.int32, (C, C), 1)
    diff = (n - mcol).astype(f32)
    nvec = lax.broadcasted_iota(jnp.int32, (C, 1), 0).astype(f32)
    decay, xi, zeta, gamma_c = [], [], [], []
    for h in range(RET_HEADS):
        lg = math.log1p(-(2.0 ** (-5.0 - h)))
        decay.append(jnp.where(diff >= 0, jnp.exp(lg * jnp.maximum(diff, 0.0)), 0.0))
        xi.append(jnp.exp(lg * (nvec + 1.0)))
        zeta.append(jnp.exp(lg * (C - 1.0 - nvec)))
        gamma_c.append(math.exp(lg * C))
    gn = gn_ref[...]

    def body(c, states):
        base = pl.multiple_of(c * C, C)
        hs = range(RET_HEADS)
        qc = [q_ref[pl.ds(base, C), h * RET_QK_DIM:(h + 1) * RET_QK_DIM] for h in hs]
        kc = [k_ref[pl.ds(base, C), h * RET_QK_DIM:(h + 1) * RET_QK_DIM] for h in hs]
        vc = [v_ref[pl.ds(base, C), h * RET_V_DIM:(h + 1) * RET_V_DIM] for h in hs]
        qk = [_dot_nt(qc[h], kc[h]) for h in hs]
        cross = [_dot(qc[h], states[h].astype(bf16)) for h in hs]
        kv = [_dot_tn((kc[h].astype(f32) * zeta[h]).astype(bf16), vc[h]) for h in hs]
        intra = [_dot((qk[h] * decay[h]).astype(bf16), vc[h]) for h in hs]
        for h in hs:
            out = intra[h] + cross[h] * xi[h]
            mu = jnp.mean(out, axis=-1, keepdims=True)
            d = out - mu
            var = jnp.mean(d * d, axis=-1, keepdims=True)
            o = d * lax.rsqrt(var + EPS) * gn[:, h * RET_V_DIM:(h + 1) * RET_V_DIM]
            o_ref[pl.ds(base, C), h * RET_V_DIM:(h + 1) * RET_V_DIM] = o.astype(bf16)
        return tuple(states[h] * gamma_c[h] + kv[h] for h in hs)

    s0 = tuple(jnp.zeros((RET_QK_DIM, RET_V_DIM), f32) for _ in range(RET_HEADS))
    lax.fori_loop(0, seq // C, body, s0, unroll=2)


def _retention(rq, rk, rv, ret_gn, batch, seq):
    return pl.pallas_call(
        _retention_kernel,
        grid=(batch,),
        in_specs=[
            pl.BlockSpec((seq, RET_QK_WIDTH), lambda b: (b, 0)),
            pl.BlockSpec((seq, RET_QK_WIDTH), lambda b: (b, 0)),
            pl.BlockSpec((seq, RET_WIDTH), lambda b: (b, 0)),
            pl.BlockSpec((1, RET_WIDTH), lambda b: (0, 0)),
        ],
        out_specs=pl.BlockSpec((seq, RET_WIDTH), lambda b: (b, 0)),
        out_shape=jax.ShapeDtypeStruct((batch * seq, RET_WIDTH), bf16),
        compiler_params=pltpu.CompilerParams(
            dimension_semantics=("arbitrary",), vmem_limit_bytes=VMEM_LIMIT),
        name="retention",
    )(rq, rk, rv, ret_gn)


def _out_proj_kernel(*refs):
    for i in range(refs[0].shape[0] // OUT_ROW_TILE):
        _out_proj_rows(slice(i * OUT_ROW_TILE, (i + 1) * OUT_ROW_TILE), *refs)


def _out_proj_rows(rows, x_ref, on_ref, or_ref, mq_ref, kvm_ref, gs_ref, w_ref, g_ref, o_ref):
    off = NSA_WIDTH + RET_WIDTH
    heads = range(MEM_HEADS)
    hsl = lambda h: slice(h * MEM_HEAD_DIM, (h + 1) * MEM_HEAD_DIM)
    s = [_dot_nt(mq_ref[rows, hsl(h)], kvm_ref[:, hsl(h)]) for h in heads]
    mix_a = jnp.concatenate([on_ref[rows, :] * gs_ref[rows, 0:NSA_WIDTH],
                             or_ref[rows, :] * gs_ref[rows, NSA_WIDTH:off]], axis=1)
    y = _dot(mix_a, w_ref[0:off, :])
    mix_m = []
    for h in heads:
        m = jnp.max(s[h], axis=1, keepdims=True)
        e = jnp.exp2(s[h] - m)
        l = jnp.sum(e, axis=1, keepdims=True)
        om = _dot(e.astype(bf16), kvm_ref[:, MEM_WIDTH + h * MEM_HEAD_DIM:MEM_WIDTH + (h + 1) * MEM_HEAD_DIM])
        mix_m.append((om * (1.0 / l)).astype(bf16)
                     * gs_ref[rows, off + h * MEM_HEAD_DIM:off + (h + 1) * MEM_HEAD_DIM])
    y = y + _dot(jnp.concatenate(mix_m, axis=1), w_ref[off:, :])
    o_ref[rows, :] = x_ref[rows, :] + _rms(y, g_ref[...])


def _out_proj(x2, o_nsa, o_ret, mq, kvm, gsilu, w_out, norm_post, seq):
    bt = x2.shape[0]
    tm = math.gcd(seq, OUT_SUBTILES * OUT_ROW_TILE)
    per_b = seq // tm
    row = lambda i: (i, 0)
    const = lambda i: (0, 0)
    return pl.pallas_call(
        _out_proj_kernel,
        grid=(bt // tm,),
        in_specs=[
            pl.BlockSpec((tm, D_MODEL), row),
            pl.BlockSpec((tm, NSA_WIDTH), row),
            pl.BlockSpec((tm, RET_WIDTH), row),
            pl.BlockSpec((tm, MEM_WIDTH), row),
            pl.BlockSpec((MEM_LEN, 2 * MEM_WIDTH), lambda i: (i // per_b, 0)),
            pl.BlockSpec((tm, MIX_WIDTH), row),
            pl.BlockSpec((MIX_WIDTH, D_MODEL), const),
            pl.BlockSpec((1, D_MODEL), const),
        ],
        out_specs=pl.BlockSpec((tm, D_MODEL), row),
        out_shape=jax.ShapeDtypeStruct((bt, D_MODEL), f32),
        compiler_params=pltpu.CompilerParams(
            dimension_semantics=("arbitrary",), vmem_limit_bytes=VMEM_LIMIT),
        name="out_proj",
    )(x2, o_nsa, o_ret, mq, kvm, gsilu, w_out, norm_post)


def _relayout_w_in(w):
    o = 0
    q_nsa = w[:, o:o + NSA_WIDTH]; o += NSA_WIDTH
    kv = w[:, o:o + NSA_KV_ALL]; o += NSA_KV_ALL
    gates = w[:, o:o + 3 * NSA_HEADS]; o += 3 * NSA_HEADS
    rest = w[:, o:]
    gates = gates.reshape(D_MODEL, 3, NSA_KV_GROUPS, NSA_REP).transpose(0, 2, 1, 3)
    gates = gates.reshape(D_MODEL, NSA_KV_GROUPS, 3 * NSA_REP)
    gates = jnp.pad(gates, ((0, 0), (0, 0), (0, GATE_PAD - 3 * NSA_REP)))
    gates = gates.reshape(D_MODEL, NSA_KV_GROUPS * GATE_PAD)
    return jnp.concatenate([q_nsa, kv, rest, gates], axis=1).astype(bf16)


def _overlap_t(seq):
    n_slot = seq // CMP_STRIDE
    n_sb = seq // SEL_BLOCK
    cmp_start = np.arange(n_slot) * CMP_STRIDE
    sel_start = np.arange(n_sb) * SEL_BLOCK
    ov = np.clip(np.minimum(cmp_start[None, :] + CMP_BLOCK, sel_start[:, None] + SEL_BLOCK)
                 - np.maximum(cmp_start[None, :], sel_start[:, None]), 0, None)
    return jnp.asarray(ov.astype(np.float32) / CMP_BLOCK)


def kernel(x, mem, positions, norm_pre, w_in, cmp_pos_k, cmp_w1_k, cmp_w2_k, cmp_pos_v, cmp_w1_v,
           cmp_w2_v, ret_gn, mem_norm, w_mem_kv, w_out, norm_post):
    depth = norm_pre.shape[0]
    batch, seq, _ = x.shape
    assert x.shape[2] == D_MODEL and mem.shape[1:] == (MEM_LEN, D_MODEL)
    assert seq % SLC_CHUNK == 0 and seq >= WINDOW + Q_BLOCK

    half = ROT_DIM // 2
    inv_n = ROPE_THETA ** (-jnp.arange(half, dtype=f32) / half)
    halfr = RET_QK_DIM // 2
    inv_r = RET_THETA ** (-jnp.arange(halfr, dtype=f32) / halfr)
    assert ROT_DIM <= RET_QK_DIM and 2 * RET_QK_DIM == LANES
    inv = jnp.concatenate([inv_n, inv_n, jnp.zeros((RET_QK_DIM - ROT_DIM,), f32), inv_r, inv_r])[None, :]
    ovl_t = _overlap_t(seq)
    expand = jnp.asarray(np.arange(seq)[:, None] // SEL_BLOCK == np.arange(LANES)[None, :], bf16)
    pos2 = positions.reshape(batch * seq, 1)
    mem2 = mem.reshape(batch * MEM_LEN, D_MODEL)

    x2 = x.reshape(batch * seq, D_MODEL)
    for layer in range(depth):
        w_r = _relayout_w_in(w_in[layer])
        q, kv, rq, rk, rv, mq, gsilu, gsig = _in_proj(x2, pos2, norm_pre[layer][None, :], w_r, inv)
        kvm = _mem_kv(mem2, mem_norm[layer][None, :], w_mem_kv[layer].astype(bf16))
        cmp = _compress(
            kv,
            jnp.stack([cmp_pos_k[layer], cmp_pos_v[layer]]),
            jnp.stack([cmp_w1_k[layer], cmp_w1_v[layer]]).astype(bf16),
            jnp.stack([cmp_w2_k[layer], cmp_w2_v[layer]]).astype(bf16),
            batch, seq)
        o_nsa = _nsa(q, cmp, kv, gsig, ovl_t, expand, batch, seq)
        o_ret = _retention(rq, rk, rv, ret_gn[layer][None, :], batch, seq)
        x2 = _out_proj(x2, o_nsa, o_ret, mq, kvm, gsilu, w_out[layer].astype(bf16),
                       norm_post[layer][None, :], seq)
    return x2.reshape(batch, seq, D_MODEL)
```

```python
import functools
import math

import jax
import jax.numpy as jnp
import numpy as np
from jax import lax
from jax.experimental import pallas as pl
from jax.experimental.pallas import tpu as pltpu

D_MODEL = 1024
MEM_LEN = 256
EPS = 1e-6

NSA_HEADS = 8
NSA_KV_GROUPS = 2
NSA_REP = NSA_HEADS // NSA_KV_GROUPS
NSA_HEAD_DIM = 128
CMP_BLOCK = 32
CMP_STRIDE = 16
CMP_HIDDEN = 256
SEL_BLOCK = 64
N_SELECT = 16
WINDOW = 512
Q_BLOCK = 256
ROPE_THETA = 500000.0
ROT_DIM = NSA_HEAD_DIM // 4

RET_HEADS = 4
RET_QK_DIM = 64
RET_V_DIM = 128
RET_CHUNK = 128
RET_THETA = 10000.0

MEM_HEADS = 4
MEM_HEAD_DIM = 128

NSA_WIDTH = NSA_HEADS * NSA_HEAD_DIM
NSA_KV_ALL = 3 * 2 * NSA_KV_GROUPS * NSA_HEAD_DIM
RET_QK_WIDTH = RET_HEADS * RET_QK_DIM
RET_WIDTH = RET_HEADS * RET_V_DIM
MEM_WIDTH = MEM_HEADS * MEM_HEAD_DIM
MIX_WIDTH = NSA_WIDTH + RET_WIDTH + MEM_WIDTH
GATE_PAD = 128

LANES = 128
SUBLANES = 8
NEG = -1e30
LOG2E = math.log2(math.e)
NSA_QSCALE = NSA_HEAD_DIM ** -0.5 * LOG2E
MEM_QSCALE = MEM_HEAD_DIM ** -0.5 * LOG2E
VMEM_LIMIT = 56 * 1024 * 1024

ROW_TILE = 256
CMP_PITCH = 24
CMP_BATCH = 4
OUT_SUBTILES = 2
OUT_ROW_TILE = 512
NSA_PAIRS = 4
SLC_CHUNK = 512

_NT = (((1,), (1,)), ((), ()))
_TN = (((0,), (0,)), ((), ()))

bf16 = jnp.bfloat16
f32 = jnp.float32


def _dot(a, b):
    return jnp.dot(a, b, preferred_element_type=f32)


def _dot_nt(a, b):
    return lax.dot_general(a, b, _NT, preferred_element_type=f32)


def _dot_tn(a, b):
    return lax.dot_general(a, b, _TN, preferred_element_type=f32)


def _col_reduce(op, x):
    slabs = [x[i:i + SUBLANES] for i in range(0, x.shape[0], SUBLANES)]
    while len(slabs) > 1:
        nxt = [op(slabs[i], slabs[i + 1]) for i in range(0, len(slabs) - 1, 2)]
        if len(slabs) % 2:
            nxt.append(slabs[-1])
        slabs = nxt
    red = jnp.max if op is jnp.maximum else jnp.sum
    return red(slabs[0], axis=0, keepdims=True)


def _rms(x, g):
    return x * lax.rsqrt(jnp.mean(x * x, axis=-1, keepdims=True) + EPS) * g


def _rotate_heads(acc, c, s1, s2, shift):
    outs = []
    for j in range(acc.shape[1] // LANES):
        a = acc[:, j * LANES:(j + 1) * LANES]
        outs.append(a * c + pltpu.roll(a, shift, 1) * s1 + pltpu.roll(a, LANES - shift, 1) * s2)
    return outs


def _in_proj_kernel(x_ref, pos_ref, g_ref, wa_ref, w_ref, wg_ref, inv_ref,
                    q_ref, kv_ref, rq_ref, rk_ref, rv_ref, mq_ref, gs_ref, gg_ref):
    x = x_ref[...]
    hb = _rms(x, g_ref[...]).astype(bf16)
    c_kv = NSA_WIDTH
    c_rqk = 0
    c_rv = c_rqk + 2 * RET_QK_WIDTH
    c_mq = c_rv + RET_WIDTH
    c_gate = c_mq + MEM_WIDTH

    pos = pos_ref[...].astype(f32)
    lane = lax.broadcasted_iota(jnp.int32, (x.shape[0], LANES), 1)
    ang = pos * inv_ref[...]
    cs, sn = jnp.cos(ang), jnp.sin(ang)
    half = ROT_DIM // 2
    cn = jnp.where(lane < ROT_DIM, cs, 1.0)
    s1n = jnp.where((lane >= half) & (lane < ROT_DIM), sn, 0.0)
    s2n = jnp.where(lane < half, -sn, 0.0)
    halfr = RET_QK_DIM // 2
    cr = jnp.where(lane >= RET_QK_DIM, cs, pltpu.roll(cs, RET_QK_DIM, 1))
    sr = jnp.where(lane >= RET_QK_DIM, sn, pltpu.roll(sn, RET_QK_DIM, 1))
    lr = lane & (RET_QK_DIM - 1)
    s1r = jnp.where(lr >= halfr, sr, 0.0)
    s2r = jnp.where(lr < halfr, -sr, 0.0)

    for j in range(MIX_WIDTH // 512):
        acc = _dot(hb, w_ref[:, c_gate + j * 512:c_gate + (j + 1) * 512])
        gs_ref[:, j * 512:(j + 1) * 512] = (acc * (1.0 / (1.0 + jnp.exp(-acc)))).astype(bf16)
    mq_ref[...] = (_dot(hb, w_ref[:, c_mq:c_mq + MEM_WIDTH]) * MEM_QSCALE).astype(bf16)
    acc = _dot(hb, wg_ref[...])
    gg_ref[...] = 1.0 / (1.0 + jnp.exp(-acc))

    for j in range(NSA_WIDTH // 512):
        acc = _dot(hb, wa_ref[:, j * 512:(j + 1) * 512])
        for i, o in enumerate(_rotate_heads(acc, cn * NSA_QSCALE, s1n * NSA_QSCALE, s2n * NSA_QSCALE, half)):
            q_ref[:, j * 512 + i * LANES: j * 512 + (i + 1) * LANES] = o.astype(bf16)
    for br in range(3):
        acc = _dot(hb, wa_ref[:, c_kv + br * 512:c_kv + (br + 1) * 512])
        for i, o in enumerate(_rotate_heads(acc[:, :256], cn, s1n, s2n, half)):
            kv_ref[:, br * 512 + i * LANES: br * 512 + (i + 1) * LANES] = o.astype(bf16)
        kv_ref[:, br * 512 + 256: br * 512 + 512] = acc[:, 256:].astype(bf16)
    acc = _dot(hb, w_ref[:, c_rqk:c_rqk + 2 * RET_QK_WIDTH])
    rot = _rotate_heads(acc, cr, s1r, s2r, halfr)
    for i in range(2):
        rq_ref[:, i * LANES:(i + 1) * LANES] = rot[i].astype(bf16)
        rk_ref[:, i * LANES:(i + 1) * LANES] = (rot[2 + i] * (RET_QK_DIM ** -0.5)).astype(bf16)

    rv_ref[...] = _dot(hb, w_ref[:, c_rv:c_rv + RET_WIDTH]).astype(bf16)


def _in_proj(x2, pos2, norm_pre, w_parts, inv):
    bt = x2.shape[0]
    tm = ROW_TILE
    row = lambda i: (i, 0)
    const = lambda i: (0, 0)
    widths = (NSA_WIDTH, NSA_KV_ALL, RET_QK_WIDTH, RET_QK_WIDTH, RET_WIDTH, MEM_WIDTH, MIX_WIDTH)
    out_shape = [jax.ShapeDtypeStruct((bt, w), bf16) for w in widths]
    out_shape.append(jax.ShapeDtypeStruct((bt, 2 * GATE_PAD), f32))
    out_specs = [pl.BlockSpec((tm, w), row) for w in widths] + [pl.BlockSpec((tm, 2 * GATE_PAD), row)]
    return pl.pallas_call(
        _in_proj_kernel,
        grid=(bt // tm,),
        in_specs=[
            pl.BlockSpec((tm, D_MODEL), row),
            pl.BlockSpec((tm, 1), row),
            pl.BlockSpec((1, D_MODEL), const),
            *[pl.BlockSpec(w.shape, const) for w in w_parts],
            pl.BlockSpec((1, LANES), const),
        ],
        out_specs=out_specs,
        out_shape=out_shape,
        compiler_params=pltpu.CompilerParams(
            dimension_semantics=("arbitrary",), vmem_limit_bytes=VMEM_LIMIT),
        name="in_proj",
    )(x2, pos2, norm_pre, *w_parts, inv)


def _mem_kv_kernel(m_ref, g_ref, w_ref, o_ref):
    hb = _rms(m_ref[...], g_ref[...]).astype(bf16)
    o_ref[...] = _dot(hb, w_ref[...]).astype(bf16)


def _mem_kv(mem2, mem_norm, w_kv):
    n = mem2.shape[0]
    tm = math.gcd(n, 4 * MEM_LEN)
    return pl.pallas_call(
        _mem_kv_kernel,
        grid=(n // tm,),
        in_specs=[
            pl.BlockSpec((tm, D_MODEL), lambda i: (i, 0)),
            pl.BlockSpec((1, D_MODEL), lambda i: (0, 0)),
            pl.BlockSpec((D_MODEL, 2 * MEM_WIDTH), lambda i: (0, 0)),
        ],
        out_specs=pl.BlockSpec((tm, 2 * MEM_WIDTH), lambda i: (i, 0)),
        out_shape=jax.ShapeDtypeStruct((n, 2 * MEM_WIDTH), bf16),
        compiler_params=pltpu.CompilerParams(
            dimension_semantics=("arbitrary",), vmem_limit_bytes=VMEM_LIMIT),
        name="mem_kv",
    )(mem2, mem_norm, w_kv)


def _compress_kernel(kv_ref, pos_ref, w1_ref, w2_ref, o_ref, xf_ref):
    nb, _, per, _ = o_ref.shape
    nslot = nb * per
    for i in range(nslot):
        xf_ref[i * CMP_PITCH:i * CMP_PITCH + CMP_STRIDE, :] = (
            kv_ref[i * CMP_STRIDE:(i + 1) * CMP_STRIDE, :].astype(f32))
    pos = pos_ref[0]
    la, lb = [], []
    for p in range(CMP_STRIDE):
        a = xf_ref[pl.ds(p, nslot, stride=CMP_PITCH), :]
        la.append((a + pos[p:p + 1, :]).astype(bf16))
        lb.append((a + pos[CMP_STRIDE + p:CMP_STRIDE + p + 1, :]).astype(bf16))
    khalf = CMP_STRIDE * NSA_HEAD_DIM
    ha = _dot(jnp.concatenate(la, axis=1), w1_ref[0, 0:khalf, :])
    hb = _dot(jnp.concatenate(lb, axis=1), w1_ref[0, khalf:2 * khalf, :])
    h = ha + pltpu.roll(hb, nslot - 1, 0)
    h = jax.nn.gelu(h)
    out = _dot(h.astype(bf16), w2_ref[0])
    slot = lax.broadcasted_iota(jnp.int32, out.shape, 0) & (per - 1)
    out = jnp.where(slot < per - 1, out, 0.0).astype(bf16)
    for b in range(nb):
        o_ref[b, 0] = out[b * per:(b + 1) * per]


def _compress(kv, cmp_pos, cmp_w1, cmp_w2, batch, seq):
    nslot = seq // CMP_STRIDE
    assert nslot & (nslot - 1) == 0
    nb = math.gcd(batch, CMP_BATCH)
    kinds = 2 * NSA_KV_GROUPS
    return pl.pallas_call(
        _compress_kernel,
        grid=(kinds, batch // nb),
        in_specs=[
            pl.BlockSpec((nb * seq, NSA_HEAD_DIM), lambda j, b: (b, j)),
            pl.BlockSpec((1, CMP_BLOCK, NSA_HEAD_DIM), lambda j, b: (j // NSA_KV_GROUPS, 0, 0)),
            pl.BlockSpec((1, CMP_BLOCK * NSA_HEAD_DIM, CMP_HIDDEN), lambda j, b: (j // NSA_KV_GROUPS, 0, 0)),
            pl.BlockSpec((1, CMP_HIDDEN, NSA_HEAD_DIM), lambda j, b: (j // NSA_KV_GROUPS, 0, 0)),
        ],
        out_specs=pl.BlockSpec((nb, 1, nslot, NSA_HEAD_DIM), lambda j, b: (b, j, 0, 0)),
        out_shape=jax.ShapeDtypeStruct((batch, kinds, nslot, NSA_HEAD_DIM), bf16),
        scratch_shapes=[pltpu.VMEM((nb * nslot * CMP_PITCH, NSA_HEAD_DIM), f32)],
        compiler_params=pltpu.CompilerParams(
            dimension_semantics=("arbitrary", "arbitrary"), vmem_limit_bytes=VMEM_LIMIT),
        name="compress",
    )(kv, cmp_pos, cmp_w1, cmp_w2)


def _nsa_tile(t0, q_ref, kc_ref, vc_ref, ks_ref, vs_ref, kw_ref, vw_ref, g_ref, ovl_ref, exp_ref, o_ref):
    n_static = (t0 + Q_BLOCK + SLC_CHUNK - 1) // SLC_CHUNK
    nslot = kc_ref.shape[2]
    n_cmp = nslot - 1
    heads = range(NSA_REP)
    hcols = lambda a, r: a[:, r * Q_BLOCK:(r + 1) * Q_BLOCK]

    qrows = slice(t0, t0 + Q_BLOCK)
    q = q_ref[qrows, :]
    q4 = jnp.concatenate([q[:, r * NSA_HEAD_DIM:(r + 1) * NSA_HEAD_DIM] for r in heads], axis=0)
    tq = t0 + lax.broadcasted_iota(jnp.int32, (1, Q_BLOCK), 1)

    def softmax_cols(s_r):
        m = _col_reduce(jnp.maximum, s_r)
        e = jnp.exp2(s_r - m)
        return e, _col_reduce(jnp.add, e)

    s = _dot_nt(kc_ref[0, 0], q4)
    start = max(t0 - WINDOW, 0)
    wlen = t0 + Q_BLOCK - start
    sw = _dot_nt(kw_ref[start:start + wlen, :], q4)
    sc0 = _dot_nt(ks_ref[0:min(SLC_CHUNK, t0 + Q_BLOCK), :], q4)
    nrow = lax.broadcasted_iota(jnp.int32, (nslot, 1), 0)
    bias_c = jnp.where((nrow * CMP_STRIDE + (CMP_BLOCK - 1) <= tq) & (nrow < n_cmp), 0.0, NEG)
    row_ok = tq >= CMP_BLOCK - 1
    ps, psum = [], None
    for r in heads:
        e, l = softmax_cols(hcols(s, r) + bias_c)
        p = e * jnp.where(row_ok, 1.0 / l, 0.0)
        ps.append(p.astype(bf16))
        psum = p if psum is None else psum + p
    o_cmp = _dot_tn(vc_ref[0, 0], jnp.concatenate(ps, axis=1))

    imp_t = jnp.dot(ovl_ref[...], psum, precision=lax.Precision.HIGHEST,
                    preferred_element_type=f32)
    n_sb = imp_t.shape[0]
    jblk = lax.broadcasted_iota(jnp.int32, (n_sb, Q_BLOCK), 0)
    tl = t0 + lax.broadcasted_iota(jnp.int32, (n_sb, Q_BLOCK), 1)
    cur = tl >> 6
    forced = (jblk == 0) | (jblk == cur) | (jblk == cur - 1)
    valid = jblk * SEL_BLOCK <= tl
    v = jnp.where(forced, jnp.inf, jnp.where(valid, imp_t, -jnp.inf))
    n_live = min(n_sb, n_static * SLC_CHUNK // SEL_BLOCK)
    slabs = [v[k:k + SUBLANES] for k in range(0, n_live, SUBLANES)]
    jl = lax.broadcasted_iota(jnp.int32, (SUBLANES, Q_BLOCK), 0)
    cnts = [jnp.zeros((SUBLANES, Q_BLOCK), f32) for _ in slabs]
    for i in range(n_live):
        vi = jnp.broadcast_to(v[i:i + 1, :], (SUBLANES, Q_BLOCK))
        for k, vk in enumerate(slabs):
            if k * SUBLANES > i:
                ahead = vi >= vk
            elif (k + 1) * SUBLANES <= i:
                ahead = vi > vk
            else:
                ahead = (vi > vk) | ((vi == vk) & (jl > i - k * SUBLANES))
            cnts[k] = cnts[k] + jnp.where(ahead, 1.0, 0.0)
    selbias_pad = jnp.concatenate([jnp.where(c < float(N_SELECT), 0.0, NEG) for c in cnts]
                                  + [jnp.zeros((LANES - n_live, Q_BLOCK), f32)], axis=0)
    q4_aug = jnp.concatenate([q4, jnp.concatenate([selbias_pad.T.astype(bf16)] * NSA_REP, axis=0)], axis=1)

    def gate_rows(br):
        return jnp.concatenate([gs[br * NSA_REP + r:br * NSA_REP + r + 1, :] for r in heads], axis=1)

    kl = lax.broadcasted_iota(jnp.int32, (Q_BLOCK, Q_BLOCK), 0)
    ql = lax.broadcasted_iota(jnp.int32, (Q_BLOCK, Q_BLOCK), 1)
    head_bias = jnp.where(kl > ql, 0.0, NEG)
    tail_bias = jnp.where(kl <= ql, 0.0, NEG)

    def add_tail_bias(a):
        body = a.shape[0] - Q_BLOCK
        tail = a[body:] + tail_bias
        return tail if body == 0 else jnp.concatenate([a[:body], tail], axis=0)

    def add_bias_w(a):
        if t0 >= WINDOW:
            a = jnp.concatenate([a[:Q_BLOCK] + head_bias, a[Q_BLOCK:]], axis=0)
        return add_tail_bias(a)

    gs = g_ref[qrows, :].T
    es, ls = [], []
    for r in heads:
        e, l = softmax_cols(add_bias_w(hcols(sw, r)))
        es.append(e.astype(bf16))
        ls.append(l)
    o_win = _dot_tn(vw_ref[start:start + wlen, :], jnp.concatenate(es, axis=1))
    o_part = (gate_rows(0) * o_cmp
              + (gate_rows(2) * (1.0 / jnp.concatenate(ls, axis=1))) * o_win)

    def slc_branch(n):
        use_sel = t0 + Q_BLOCK > N_SELECT * SEL_BLOCK
        bounds = [(c * SLC_CHUNK, min((c + 1) * SLC_CHUNK, t0 + Q_BLOCK)) for c in range(n)]

        def slc_scores(c):
            lo, hi = bounds[c]
            ks = ks_ref[lo:hi, :]
            if not use_sel:
                return _dot_nt(ks, q4)
            return _dot_nt(jnp.concatenate([ks, exp_ref[lo:hi, :]], axis=1), q4_aug)

        m_i, l_i, acc = [None] * NSA_REP, [None] * NSA_REP, None
        sc = slc_scores(0) if use_sel else sc0
        for c in range(n):
            sc_next = slc_scores(c + 1) if c + 1 < n else None
            es, alphas = [], []
            for r in heads:
                s_r = add_tail_bias(hcols(sc, r)) if c == n - 1 else hcols(sc, r)
                m_c = _col_reduce(jnp.maximum, s_r)
                if c == 0:
                    e = jnp.exp2(s_r - m_c)
                    m_i[r], l_i[r] = m_c, _col_reduce(jnp.add, e)
                else:
                    m_new = jnp.maximum(m_i[r], m_c)
                    alpha = jnp.exp2(m_i[r] - m_new)
                    e = jnp.exp2(s_r - m_new)
                    m_i[r], l_i[r] = m_new, alpha * l_i[r] + _col_reduce(jnp.add, e)
                    alphas.append(alpha)
                es.append(e.astype(bf16))
            pv = _dot_tn(vs_ref[bounds[c][0]:bounds[c][1], :], jnp.concatenate(es, axis=1))
            acc = pv if c == 0 else jnp.concatenate(alphas, axis=1) * acc + pv
            sc = sc_next
        o = o_part + (gate_rows(1) * (1.0 / jnp.concatenate(l_i, axis=1))) * acc
        for r in heads:
            o_ref[qrows, r * NSA_HEAD_DIM:(r + 1) * NSA_HEAD_DIM] = hcols(o, r).T.astype(bf16)

    slc_branch(n_static)


def _nsa_kernel(*refs):
    seq = refs[3].shape[0]
    nq = seq // Q_BLOCK

    def pairs(step):
        for p in range(step * NSA_PAIRS, (step + 1) * NSA_PAIRS):
            _nsa_tile((nq - 1 - p) * Q_BLOCK, *refs)
            _nsa_tile(p * Q_BLOCK, *refs)

    for step in range(nq // (2 * NSA_PAIRS)):
        pl.when(pl.program_id(2) == step)(functools.partial(pairs, step))


def _nsa(q, cmp, kv, gsig, ovl_t, expand, batch, seq):
    nq = seq // Q_BLOCK
    gw = NSA_REP * NSA_HEAD_DIM
    nslot = cmp.shape[2]
    G = NSA_KV_GROUPS
    kvspec = lambda off: pl.BlockSpec((seq, NSA_HEAD_DIM), lambda b, g, i, off=off: (b, off + g))
    return pl.pallas_call(
        _nsa_kernel,
        grid=(batch, G, nq // (2 * NSA_PAIRS)),
        in_specs=[
            pl.BlockSpec((seq, gw), lambda b, g, i: (b, g)),
            pl.BlockSpec((1, 1, nslot, NSA_HEAD_DIM), lambda b, g, i: (b, g, 0, 0)),
            pl.BlockSpec((1, 1, nslot, NSA_HEAD_DIM), lambda b, g, i: (b, G + g, 0, 0)),
            kvspec(2 * G), kvspec(3 * G), kvspec(4 * G), kvspec(5 * G),
            pl.BlockSpec((seq, GATE_PAD), lambda b, g, i: (b, g)),
            pl.BlockSpec(ovl_t.shape, lambda b, g, i: (0, 0)),
            pl.BlockSpec(expand.shape, lambda b, g, i: (0, 0)),
        ],
        out_specs=pl.BlockSpec((seq, gw), lambda b, g, i: (b, g)),
        out_shape=jax.ShapeDtypeStruct((batch * seq, NSA_WIDTH), bf16),
        compiler_params=pltpu.CompilerParams(
            dimension_semantics=("arbitrary", "arbitrary", "arbitrary"), vmem_limit_bytes=VMEM_LIMIT),
        name="nsa",
    )(q, cmp, cmp, kv, kv, kv, kv, gsig, ovl_t, expand)


def _retention_kernel(q_ref, k_ref, v_ref, gn_ref, o_ref):
    seq = q_ref.shape[0]
    C = RET_CHUNK
    n = lax.broadcasted_iota(jnp.int32, (C, C), 0)
    mcol = lax.broadcasted_iota(jnp.int32, (C, C), 1)
    diff = (n - mcol).astype(f32)
    nvec = lax.broadcasted_iota(jnp.int32, (C, 1), 0).astype(f32)
    decay, xi, zeta, gamma_c = [], [], [], []
    for h in range(RET_HEADS):
        lg = math.log1p(-(2.0 ** (-5.0 - h)))
        decay.append(jnp.where(diff >= 0, jnp.exp(lg * jnp.maximum(diff, 0.0)), 0.0))
        xi.append(jnp.exp(lg * (nvec + 1.0)))
        zeta.append(jnp.exp(lg * (C - 1.0 - nvec)))
        gamma_c.append(math.exp(lg * C))
    gn = gn_ref[...]

    def body(c, states):
        base = pl.multiple_of(c * C, C)
        hs = range(RET_HEADS)
        qc = [q_ref[pl.ds(base, C), h * RET_QK_DIM:(h + 1) * RET_QK_DIM] for h in hs]
        kc = [k_ref[pl.ds(base, C), h * RET_QK_DIM:(h + 1) * RET_QK_DIM] for h in hs]
        vc = [v_ref[pl.ds(base, C), h * RET_V_DIM:(h + 1) * RET_V_DIM] for h in hs]
        qk = [_dot_nt(qc[h], kc[h]) for h in hs]
        cross = [_dot(qc[h], states[h].astype(bf16)) for h in hs]
        kv = [_dot_tn((kc[h].astype(f32) * zeta[h]).astype(bf16), vc[h]) for h in hs]
        intra = [_dot((qk[h] * decay[h]).astype(bf16), vc[h]) for h in hs]
        for h in hs:
            out = intra[h] + cross[h] * xi[h]
            mu = jnp.mean(out, axis=-1, keepdims=True)
            d = out - mu
            var = jnp.mean(d * d, axis=-1, keepdims=True)
            o = d * lax.rsqrt(var + EPS) * gn[:, h * RET_V_DIM:(h + 1) * RET_V_DIM]
            o_ref[pl.ds(base, C), h * RET_V_DIM:(h + 1) * RET_V_DIM] = o.astype(bf16)
        return tuple(states[h] * gamma_c[h] + kv[h] for h in hs)

    s0 = tuple(jnp.zeros((RET_QK_DIM, RET_V_DIM), f32) for _ in range(RET_HEADS))
    lax.fori_loop(0, seq // C, body, s0, unroll=2)


def _retention(rq, rk, rv, ret_gn, batch, seq):
    return pl.pallas_call(
        _retention_kernel,
        grid=(batch,),
        in_specs=[
            pl.BlockSpec((seq, RET_QK_WIDTH), lambda b: (b, 0)),
            pl.BlockSpec((seq, RET_QK_WIDTH), lambda b: (b, 0)),
            pl.BlockSpec((seq, RET_WIDTH), lambda b: (b, 0)),
            pl.BlockSpec((1, RET_WIDTH), lambda b: (0, 0)),
        ],
        out_specs=pl.BlockSpec((seq, RET_WIDTH), lambda b: (b, 0)),
        out_shape=jax.ShapeDtypeStruct((batch * seq, RET_WIDTH), bf16),
        compiler_params=pltpu.CompilerParams(
            dimension_semantics=("arbitrary",), vmem_limit_bytes=VMEM_LIMIT),
        name="retention",
    )(rq, rk, rv, ret_gn)


def _out_proj_kernel(*refs):
    for i in range(refs[0].shape[0] // OUT_ROW_TILE):
        _out_proj_rows(slice(i * OUT_ROW_TILE, (i + 1) * OUT_ROW_TILE), *refs)


def _out_proj_rows(rows, x_ref, on_ref, or_ref, mq_ref, kvm_ref, gs_ref, w_ref, g_ref, o_ref):
    off = NSA_WIDTH + RET_WIDTH
    heads = range(MEM_HEADS)
    hsl = lambda h: slice(h * MEM_HEAD_DIM, (h + 1) * MEM_HEAD_DIM)
    s = [_dot_nt(mq_ref[rows, hsl(h)], kvm_ref[:, hsl(h)]) for h in heads]
    mix_a = jnp.concatenate([on_ref[rows, :] * gs_ref[rows, 0:NSA_WIDTH],
                             or_ref[rows, :] * gs_ref[rows, NSA_WIDTH:off]], axis=1)
    y = _dot(mix_a, w_ref[0:off, :])
    mix_m = []
    for h in heads:
        m = jnp.max(s[h], axis=1, keepdims=True)
        e = jnp.exp2(s[h] - m)
        l = jnp.sum(e, axis=1, keepdims=True)
        om = _dot(e.astype(bf16), kvm_ref[:, MEM_WIDTH + h * MEM_HEAD_DIM:MEM_WIDTH + (h + 1) * MEM_HEAD_DIM])
        mix_m.append((om * (1.0 / l)).astype(bf16)
                     * gs_ref[rows, off + h * MEM_HEAD_DIM:off + (h + 1) * MEM_HEAD_DIM])
    y = y + _dot(jnp.concatenate(mix_m, axis=1), w_ref[off:, :])
    o_ref[rows, :] = x_ref[rows, :] + _rms(y, g_ref[...])


def _out_proj(x2, o_nsa, o_ret, mq, kvm, gsilu, w_out, norm_post, seq):
    bt = x2.shape[0]
    tm = math.gcd(seq, OUT_SUBTILES * OUT_ROW_TILE)
    per_b = seq // tm
    row = lambda i: (i, 0)
    const = lambda i: (0, 0)
    return pl.pallas_call(
        _out_proj_kernel,
        grid=(bt // tm,),
        in_specs=[
            pl.BlockSpec((tm, D_MODEL), row),
            pl.BlockSpec((tm, NSA_WIDTH), row),
            pl.BlockSpec((tm, RET_WIDTH), row),
            pl.BlockSpec((tm, MEM_WIDTH), row),
            pl.BlockSpec((MEM_LEN, 2 * MEM_WIDTH), lambda i: (i // per_b, 0)),
            pl.BlockSpec((tm, MIX_WIDTH), row),
            pl.BlockSpec((MIX_WIDTH, D_MODEL), const),
            pl.BlockSpec((1, D_MODEL), const),
        ],
        out_specs=pl.BlockSpec((tm, D_MODEL), row),
        out_shape=jax.ShapeDtypeStruct((bt, D_MODEL), f32),
        compiler_params=pltpu.CompilerParams(
            dimension_semantics=("arbitrary",), vmem_limit_bytes=VMEM_LIMIT),
        name="out_proj",
    )(x2, o_nsa, o_ret, mq, kvm, gsilu, w_out, norm_post)


def _split_w_in(w):
    n_a = NSA_WIDTH + NSA_KV_ALL
    n_g = 3 * NSA_HEADS
    gates = w[:, n_a:n_a + n_g]
    gates = gates.reshape(D_MODEL, 3, NSA_KV_GROUPS, NSA_REP).transpose(0, 2, 1, 3)
    gates = gates.reshape(D_MODEL, NSA_KV_GROUPS, 3 * NSA_REP)
    gates = jnp.pad(gates, ((0, 0), (0, 0), (0, GATE_PAD - 3 * NSA_REP)))
    gates = gates.reshape(D_MODEL, NSA_KV_GROUPS * GATE_PAD)
    return w[:, :n_a].astype(bf16), w[:, n_a + n_g:].astype(bf16), gates.astype(bf16)


def _overlap_t(seq):
    n_slot = seq // CMP_STRIDE
    n_sb = seq // SEL_BLOCK
    cmp_start = np.arange(n_slot) * CMP_STRIDE
    sel_start = np.arange(n_sb) * SEL_BLOCK
    ov = np.clip(np.minimum(cmp_start[None, :] + CMP_BLOCK, sel_start[:, None] + SEL_BLOCK)
                 - np.maximum(cmp_start[None, :], sel_start[:, None]), 0, None)
    return jnp.asarray(ov.astype(np.float32) / CMP_BLOCK)


def kernel(x, mem, positions, norm_pre, w_in, cmp_pos_k, cmp_w1_k, cmp_w2_k, cmp_pos_v, cmp_w1_v,
           cmp_w2_v, ret_gn, mem_norm, w_mem_kv, w_out, norm_post):
    depth = norm_pre.shape[0]
    batch, seq, _ = x.shape
    assert x.shape[2] == D_MODEL and mem.shape[1:] == (MEM_LEN, D_MODEL)
    assert seq % SLC_CHUNK == 0 and seq >= WINDOW + Q_BLOCK

    half = ROT_DIM // 2
    inv_n = ROPE_THETA ** (-jnp.arange(half, dtype=f32) / half)
    halfr = RET_QK_DIM // 2
    inv_r = RET_THETA ** (-jnp.arange(halfr, dtype=f32) / halfr)
    assert ROT_DIM <= RET_QK_DIM and 2 * RET_QK_DIM == LANES
    inv = jnp.concatenate([inv_n, inv_n, jnp.zeros((RET_QK_DIM - ROT_DIM,), f32), inv_r, inv_r])[None, :]
    ovl_t = _overlap_t(seq)
    expand = jnp.asarray(np.arange(seq)[:, None] // SEL_BLOCK == np.arange(LANES)[None, :], bf16)
    pos2 = positions.reshape(batch * seq, 1)
    mem2 = mem.reshape(batch * MEM_LEN, D_MODEL)

    x2 = x.reshape(batch * seq, D_MODEL)
    for layer in range(depth):
        q, kv, rq, rk, rv, mq, gsilu, gsig = _in_proj(x2, pos2, norm_pre[layer][None, :],
                                                      _split_w_in(w_in[layer]), inv)
        kvm = _mem_kv(mem2, mem_norm[layer][None, :], w_mem_kv[layer].astype(bf16))
        cmp = _compress(
            kv,
            jnp.stack([cmp_pos_k[layer], cmp_pos_v[layer]]),
            jnp.stack([cmp_w1_k[layer], cmp_w1_v[layer]]).astype(bf16),
            jnp.stack([cmp_w2_k[layer], cmp_w2_v[layer]]).astype(bf16),
            batch, seq)
        o_nsa = _nsa(q, cmp, kv, gsig, ovl_t, expand, batch, seq)
        o_ret = _retention(rq, rk, rv, ret_gn[layer][None, :], batch, seq)
        x2 = _out_proj(x2, o_nsa, o_ret, mq, kvm, gsilu, w_out[layer].astype(bf16),
                       norm_post[layer][None, :], seq)
    return x2.reshape(batch, seq, D_MODEL)
```

```python
import functools
import math

import jax
import jax.numpy as jnp
import numpy as np
from jax import lax
from jax.experimental import pallas as pl
from jax.experimental.pallas import tpu as pltpu

D_MODEL = 1024
MEM_LEN = 256
EPS = 1e-6

NSA_HEADS = 8
NSA_KV_GROUPS = 2
NSA_REP = NSA_HEADS // NSA_KV_GROUPS
NSA_HEAD_DIM = 128
CMP_BLOCK = 32
CMP_STRIDE = 16
CMP_HIDDEN = 256
SEL_BLOCK = 64
N_SELECT = 16
WINDOW = 512
Q_BLOCK = 256
ROPE_THETA = 500000.0
ROT_DIM = NSA_HEAD_DIM // 4

RET_HEADS = 4
RET_QK_DIM = 64
RET_V_DIM = 128
RET_CHUNK = 128
RET_THETA = 10000.0

MEM_HEADS = 4
MEM_HEAD_DIM = 128

NSA_WIDTH = NSA_HEADS * NSA_HEAD_DIM
NSA_KV_ALL = 3 * 2 * NSA_KV_GROUPS * NSA_HEAD_DIM
RET_QK_WIDTH = RET_HEADS * RET_QK_DIM
RET_WIDTH = RET_HEADS * RET_V_DIM
MEM_WIDTH = MEM_HEADS * MEM_HEAD_DIM
MIX_WIDTH = NSA_WIDTH + RET_WIDTH + MEM_WIDTH
GATE_PAD = 128

LANES = 128
REDUCE_CHAINS = 4
SUBLANES = 8
NEG = -1e30
LOG2E = math.log2(math.e)
NSA_QSCALE = NSA_HEAD_DIM ** -0.5 * LOG2E
MEM_QSCALE = MEM_HEAD_DIM ** -0.5 * LOG2E
VMEM_LIMIT = 56 * 1024 * 1024

IN_SUBTILES = 2
ROW_TILE = 256
CMP_PITCH = 24
CMP_BATCH = 4
OUT_SUBTILES = 2
OUT_ROW_TILE = 512
NSA_PAIRS = 4
SLC_CHUNK = 512

_NT = (((1,), (1,)), ((), ()))
_TN = (((0,), (0,)), ((), ()))

bf16 = jnp.bfloat16
f32 = jnp.float32


def _dot(a, b):
    return jnp.dot(a, b, preferred_element_type=f32)


def _dot_nt(a, b):
    return lax.dot_general(a, b, _NT, preferred_element_type=f32)


def _dot_tn(a, b):
    return lax.dot_general(a, b, _TN, preferred_element_type=f32)


def _col_reduce(op, x):
    slabs = [x[i:i + SUBLANES] for i in range(0, x.shape[0], SUBLANES)]
    accs = slabs[:REDUCE_CHAINS]
    for i in range(REDUCE_CHAINS, len(slabs)):
        accs[i % REDUCE_CHAINS] = op(accs[i % REDUCE_CHAINS], slabs[i])
    slabs = accs
    while len(slabs) > 1:
        nxt = [op(slabs[i], slabs[i + 1]) for i in range(0, len(slabs) - 1, 2)]
        if len(slabs) % 2:
            nxt.append(slabs[-1])
        slabs = nxt
    red = jnp.max if op is jnp.maximum else jnp.sum
    return red(slabs[0], axis=0, keepdims=True)


def _rms(x, g):
    return x * lax.rsqrt(jnp.mean(x * x, axis=-1, keepdims=True) + EPS) * g


def _rotate_heads(acc, c, s1, s2, shift):
    outs = []
    for j in range(acc.shape[1] // LANES):
        a = acc[:, j * LANES:(j + 1) * LANES]
        outs.append(a * c + pltpu.roll(a, shift, 1) * s1 + pltpu.roll(a, LANES - shift, 1) * s2)
    return outs


def _in_proj_kernel(*refs):
    for i in range(refs[0].shape[0] // ROW_TILE):
        _in_proj_rows(slice(i * ROW_TILE, (i + 1) * ROW_TILE), *refs)


def _in_proj_rows(rows, x_ref, pos_ref, g_ref, wa_ref, w_ref, wg_ref, inv_ref,
                  q_ref, kv_ref, rq_ref, rk_ref, rv_ref, mq_ref, gs_ref, gg_ref):
    x = x_ref[rows, :]
    hb = _rms(x, g_ref[...]).astype(bf16)
    c_kv = NSA_WIDTH
    c_rqk = 0
    c_rv = c_rqk + 2 * RET_QK_WIDTH
    c_mq = c_rv + RET_WIDTH
    c_gate = c_mq + MEM_WIDTH

    pos = pos_ref[rows, :].astype(f32)
    lane = lax.broadcasted_iota(jnp.int32, (x.shape[0], LANES), 1)
    ang = pos * inv_ref[...]
    cs, sn = jnp.cos(ang), jnp.sin(ang)
    half = ROT_DIM // 2
    cn = jnp.where(lane < ROT_DIM, cs, 1.0)
    s1n = jnp.where((lane >= half) & (lane < ROT_DIM), sn, 0.0)
    s2n = jnp.where(lane < half, -sn, 0.0)
    halfr = RET_QK_DIM // 2
    cr = jnp.where(lane >= RET_QK_DIM, cs, pltpu.roll(cs, RET_QK_DIM, 1))
    sr = jnp.where(lane >= RET_QK_DIM, sn, pltpu.roll(sn, RET_QK_DIM, 1))
    lr = lane & (RET_QK_DIM - 1)
    s1r = jnp.where(lr >= halfr, sr, 0.0)
    s2r = jnp.where(lr < halfr, -sr, 0.0)

    for j in range(MIX_WIDTH // 512):
        acc = _dot(hb, w_ref[:, c_gate + j * 512:c_gate + (j + 1) * 512])
        gs_ref[rows, j * 512:(j + 1) * 512] = (acc * (1.0 / (1.0 + jnp.exp(-acc)))).astype(bf16)
    mq_ref[rows, :] = (_dot(hb, w_ref[:, c_mq:c_mq + MEM_WIDTH]) * MEM_QSCALE).astype(bf16)
    acc = _dot(hb, wg_ref[...])
    gg_ref[rows, :] = 1.0 / (1.0 + jnp.exp(-acc))

    for j in range(NSA_WIDTH // 512):
        acc = _dot(hb, wa_ref[:, j * 512:(j + 1) * 512])
        for i, o in enumerate(_rotate_heads(acc, cn * NSA_QSCALE, s1n * NSA_QSCALE, s2n * NSA_QSCALE, half)):
            q_ref[rows, j * 512 + i * LANES: j * 512 + (i + 1) * LANES] = o.astype(bf16)
    for br in range(3):
        acc = _dot(hb, wa_ref[:, c_kv + br * 512:c_kv + (br + 1) * 512])
        for i, o in enumerate(_rotate_heads(acc[:, :256], cn, s1n, s2n, half)):
            kv_ref[rows, br * 512 + i * LANES: br * 512 + (i + 1) * LANES] = o.astype(bf16)
        kv_ref[rows, br * 512 + 256: br * 512 + 512] = acc[:, 256:].astype(bf16)
    acc = _dot(hb, w_ref[:, c_rqk:c_rqk + 2 * RET_QK_WIDTH])
    rot = _rotate_heads(acc, cr, s1r, s2r, halfr)
    for i in range(2):
        rq_ref[rows, i * LANES:(i + 1) * LANES] = rot[i].astype(bf16)
        rk_ref[rows, i * LANES:(i + 1) * LANES] = (rot[2 + i] * (RET_QK_DIM ** -0.5)).astype(bf16)

    rv_ref[rows, :] = _dot(hb, w_ref[:, c_rv:c_rv + RET_WIDTH]).astype(bf16)


def _in_proj(x2, pos2, norm_pre, w_parts, inv):
    bt = x2.shape[0]
    tm = IN_SUBTILES * ROW_TILE
    row = lambda i: (i, 0)
    const = lambda i: (0, 0)
    widths = (NSA_WIDTH, NSA_KV_ALL, RET_QK_WIDTH, RET_QK_WIDTH, RET_WIDTH, MEM_WIDTH, MIX_WIDTH)
    out_shape = [jax.ShapeDtypeStruct((bt, w), bf16) for w in widths]
    out_shape.append(jax.ShapeDtypeStruct((bt, 2 * GATE_PAD), f32))
    out_specs = [pl.BlockSpec((tm, w), row) for w in widths] + [pl.BlockSpec((tm, 2 * GATE_PAD), row)]
    return pl.pallas_call(
        _in_proj_kernel,
        grid=(bt // tm,),
        in_specs=[
            pl.BlockSpec((tm, D_MODEL), row),
            pl.BlockSpec((tm, 1), row),
            pl.BlockSpec((1, D_MODEL), const),
            *[pl.BlockSpec(w.shape, const) for w in w_parts],
            pl.BlockSpec((1, LANES), const),
        ],
        out_specs=out_specs,
        out_shape=out_shape,
        compiler_params=pltpu.CompilerParams(
            dimension_semantics=("arbitrary",), vmem_limit_bytes=VMEM_LIMIT),
        name="in_proj",
    )(x2, pos2, norm_pre, *w_parts, inv)


def _mem_kv_kernel(m_ref, g_ref, w_ref, o_ref):
    hb = _rms(m_ref[...], g_ref[...]).astype(bf16)
    o_ref[...] = _dot(hb, w_ref[...]).astype(bf16)


def _mem_kv(mem2, mem_norm, w_kv):
    n = mem2.shape[0]
    tm = math.gcd(n, 4 * MEM_LEN)
    return pl.pallas_call(
        _mem_kv_kernel,
        grid=(n // tm,),
        in_specs=[
            pl.BlockSpec((tm, D_MODEL), lambda i: (i, 0)),
            pl.BlockSpec((1, D_MODEL), lambda i: (0, 0)),
            pl.BlockSpec((D_MODEL, 2 * MEM_WIDTH), lambda i: (0, 0)),
        ],
        out_specs=pl.BlockSpec((tm, 2 * MEM_WIDTH), lambda i: (i, 0)),
        out_shape=jax.ShapeDtypeStruct((n, 2 * MEM_WIDTH), bf16),
        compiler_params=pltpu.CompilerParams(
            dimension_semantics=("arbitrary",), vmem_limit_bytes=VMEM_LIMIT),
        name="mem_kv",
    )(mem2, mem_norm, w_kv)


def _compress_kernel(kv_ref, pos_ref, w1_ref, w2_ref, o_ref, xf_ref):
    nb, _, per, _ = o_ref.shape
    nslot = nb * per
    for i in range(nslot):
        xf_ref[i * CMP_PITCH:i * CMP_PITCH + CMP_STRIDE, :] = (
            kv_ref[i * CMP_STRIDE:(i + 1) * CMP_STRIDE, :].astype(f32))
    pos = pos_ref[0]
    la, lb = [], []
    for p in range(CMP_STRIDE):
        a = xf_ref[pl.ds(p, nslot, stride=CMP_PITCH), :]
        la.append((a + pos[p:p + 1, :]).astype(bf16))
        lb.append((a + pos[CMP_STRIDE + p:CMP_STRIDE + p + 1, :]).astype(bf16))
    khalf = CMP_STRIDE * NSA_HEAD_DIM
    ha = _dot(jnp.concatenate(la, axis=1), w1_ref[0, 0:khalf, :])
    hb = _dot(jnp.concatenate(lb, axis=1), w1_ref[0, khalf:2 * khalf, :])
    h = ha + pltpu.roll(hb, nslot - 1, 0)
    h = jax.nn.gelu(h)
    out = _dot(h.astype(bf16), w2_ref[0])
    slot = lax.broadcasted_iota(jnp.int32, out.shape, 0) & (per - 1)
    out = jnp.where(slot < per - 1, out, 0.0).astype(bf16)
    for b in range(nb):
        o_ref[b, 0] = out[b * per:(b + 1) * per]


def _compress(kv, cmp_pos, cmp_w1, cmp_w2, batch, seq):
    nslot = seq // CMP_STRIDE
    assert nslot & (nslot - 1) == 0
    nb = math.gcd(batch, CMP_BATCH)
    kinds = 2 * NSA_KV_GROUPS
    return pl.pallas_call(
        _compress_kernel,
        grid=(kinds, batch // nb),
        in_specs=[
            pl.BlockSpec((nb * seq, NSA_HEAD_DIM), lambda j, b: (b, j)),
            pl.BlockSpec((1, CMP_BLOCK, NSA_HEAD_DIM), lambda j, b: (j // NSA_KV_GROUPS, 0, 0)),
            pl.BlockSpec((1, CMP_BLOCK * NSA_HEAD_DIM, CMP_HIDDEN), lambda j, b: (j // NSA_KV_GROUPS, 0, 0)),
            pl.BlockSpec((1, CMP_HIDDEN, NSA_HEAD_DIM), lambda j, b: (j // NSA_KV_GROUPS, 0, 0)),
        ],
        out_specs=pl.BlockSpec((nb, 1, nslot, NSA_HEAD_DIM), lambda j, b: (b, j, 0, 0)),
        out_shape=jax.ShapeDtypeStruct((batch, kinds, nslot, NSA_HEAD_DIM), bf16),
        scratch_shapes=[pltpu.VMEM((nb * nslot * CMP_PITCH, NSA_HEAD_DIM), f32)],
        compiler_params=pltpu.CompilerParams(
            dimension_semantics=("arbitrary", "arbitrary"), vmem_limit_bytes=VMEM_LIMIT),
        name="compress",
    )(kv, cmp_pos, cmp_w1, cmp_w2)


def _nsa_tile(t0, q_ref, kc_ref, vc_ref, ks_ref, vs_ref, kw_ref, vw_ref, g_ref, ovl_ref, exp_ref, o_ref):
    n_static = (t0 + Q_BLOCK + SLC_CHUNK - 1) // SLC_CHUNK
    nslot = kc_ref.shape[2]
    n_cmp = nslot - 1
    heads = range(NSA_REP)
    hcols = lambda a, r: a[:, r * Q_BLOCK:(r + 1) * Q_BLOCK]

    qrows = slice(t0, t0 + Q_BLOCK)
    q = q_ref[qrows, :]
    q4 = jnp.concatenate([q[:, r * NSA_HEAD_DIM:(r + 1) * NSA_HEAD_DIM] for r in heads], axis=0)
    tq = t0 + lax.broadcasted_iota(jnp.int32, (1, Q_BLOCK), 1)

    def softmax_cols(s_r):
        m = _col_reduce(jnp.maximum, s_r)
        e = jnp.exp2(s_r - m)
        return e, _col_reduce(jnp.add, e)

    s = _dot_nt(kc_ref[0, 0], q4)
    start = max(t0 - WINDOW, 0)
    wlen = t0 + Q_BLOCK - start
    sw = _dot_nt(kw_ref[start:start + wlen, :], q4)
    sc0 = _dot_nt(ks_ref[0:min(SLC_CHUNK, t0 + Q_BLOCK), :], q4)
    nrow = lax.broadcasted_iota(jnp.int32, (nslot, 1), 0)
    bias_c = jnp.where((nrow * CMP_STRIDE + (CMP_BLOCK - 1) <= tq) & (nrow < n_cmp), 0.0, NEG)
    row_ok = tq >= CMP_BLOCK - 1
    ps, psum = [], None
    for r in heads:
        e, l = softmax_cols(hcols(s, r) + bias_c)
        p = e * jnp.where(row_ok, 1.0 / l, 0.0)
        ps.append(p.astype(bf16))
        psum = p if psum is None else psum + p
    o_cmp = _dot_tn(vc_ref[0, 0], jnp.concatenate(ps, axis=1))

    imp_t = jnp.dot(ovl_ref[...], psum, precision=lax.Precision.HIGHEST,
                    preferred_element_type=f32)
    n_sb = imp_t.shape[0]
    jblk = lax.broadcasted_iota(jnp.int32, (n_sb, Q_BLOCK), 0)
    tl = t0 + lax.broadcasted_iota(jnp.int32, (n_sb, Q_BLOCK), 1)
    cur = tl >> 6
    forced = (jblk == 0) | (jblk == cur) | (jblk == cur - 1)
    valid = jblk * SEL_BLOCK <= tl
    v = jnp.where(forced, jnp.inf, jnp.where(valid, imp_t, -jnp.inf))
    n_live = min(n_sb, n_static * SLC_CHUNK // SEL_BLOCK)
    slabs = [v[k:k + SUBLANES] for k in range(0, n_live, SUBLANES)]
    jl = lax.broadcasted_iota(jnp.int32, (SUBLANES, Q_BLOCK), 0)
    cnts = [jnp.zeros((SUBLANES, Q_BLOCK), f32) for _ in slabs]
    for i in range(n_live):
        vi = jnp.broadcast_to(v[i:i + 1, :], (SUBLANES, Q_BLOCK))
        for k, vk in enumerate(slabs):
            if k * SUBLANES > i:
                ahead = vi >= vk
            elif (k + 1) * SUBLANES <= i:
                ahead = vi > vk
            else:
                ahead = (vi > vk) | ((vi == vk) & (jl > i - k * SUBLANES))
            cnts[k] = cnts[k] + jnp.where(ahead, 1.0, 0.0)
    selbias_pad = jnp.concatenate([jnp.where(c < float(N_SELECT), 0.0, NEG) for c in cnts]
                                  + [jnp.zeros((LANES - n_live, Q_BLOCK), f32)], axis=0)
    q4_aug = jnp.concatenate([q4, jnp.concatenate([selbias_pad.T.astype(bf16)] * NSA_REP, axis=0)], axis=1)

    def gate_rows(br):
        return jnp.concatenate([gs[br * NSA_REP + r:br * NSA_REP + r + 1, :] for r in heads], axis=1)

    kl = lax.broadcasted_iota(jnp.int32, (Q_BLOCK, Q_BLOCK), 0)
    ql = lax.broadcasted_iota(jnp.int32, (Q_BLOCK, Q_BLOCK), 1)
    head_bias = jnp.where(kl > ql, 0.0, NEG)
    tail_bias = jnp.where(kl <= ql, 0.0, NEG)

    def add_tail_bias(a):
        body = a.shape[0] - Q_BLOCK
        tail = a[body:] + tail_bias
        return tail if body == 0 else jnp.concatenate([a[:body], tail], axis=0)

    def add_bias_w(a):
        if t0 >= WINDOW:
            a = jnp.concatenate([a[:Q_BLOCK] + head_bias, a[Q_BLOCK:]], axis=0)
        return add_tail_bias(a)

    gs = g_ref[qrows, :].T
    es, ls = [], []
    for r in heads:
        e, l = softmax_cols(add_bias_w(hcols(sw, r)))
        es.append(e.astype(bf16))
        ls.append(l)
    o_win = _dot_tn(vw_ref[start:start + wlen, :], jnp.concatenate(es, axis=1))
    o_part = (gate_rows(0) * o_cmp
              + (gate_rows(2) * (1.0 / jnp.concatenate(ls, axis=1))) * o_win)

    def slc_branch(n):
        use_sel = t0 + Q_BLOCK > N_SELECT * SEL_BLOCK
        bounds = [(c * SLC_CHUNK, min((c + 1) * SLC_CHUNK, t0 + Q_BLOCK)) for c in range(n)]

        def slc_scores(c):
            lo, hi = bounds[c]
            ks = ks_ref[lo:hi, :]
            if not use_sel:
                return _dot_nt(ks, q4)
            return _dot_nt(jnp.concatenate([ks, exp_ref[lo:hi, :]], axis=1), q4_aug)

        m_i, l_i, acc = [None] * NSA_REP, [None] * NSA_REP, None
        sc = slc_scores(0) if use_sel else sc0
        for c in range(n):
            sc_next = slc_scores(c + 1) if c + 1 < n else None
            es, alphas = [], []
            for r in heads:
                s_r = add_tail_bias(hcols(sc, r)) if c == n - 1 else hcols(sc, r)
                m_c = _col_reduce(jnp.maximum, s_r)
                if c == 0:
                    e = jnp.exp2(s_r - m_c)
                    m_i[r], l_i[r] = m_c, _col_reduce(jnp.add, e)
                else:
                    m_new = jnp.maximum(m_i[r], m_c)
                    alpha = jnp.exp2(m_i[r] - m_new)
                    e = jnp.exp2(s_r - m_new)
                    m_i[r], l_i[r] = m_new, alpha * l_i[r] + _col_reduce(jnp.add, e)
                    alphas.append(alpha)
                es.append(e.astype(bf16))
            pv = _dot_tn(vs_ref[bounds[c][0]:bounds[c][1], :], jnp.concatenate(es, axis=1))
            acc = pv if c == 0 else jnp.concatenate(alphas, axis=1) * acc + pv
            sc = sc_next
        o = o_part + (gate_rows(1) * (1.0 / jnp.concatenate(l_i, axis=1))) * acc
        for r in heads:
            o_ref[qrows, r * NSA_HEAD_DIM:(r + 1) * NSA_HEAD_DIM] = hcols(o, r).T.astype(bf16)

    slc_branch(n_static)


def _nsa_kernel(*refs):
    seq = refs[3].shape[0]
    nq = seq // Q_BLOCK

    def pairs(step):
        for p in range(step * NSA_PAIRS, (step + 1) * NSA_PAIRS):
            _nsa_tile((nq - 1 - p) * Q_BLOCK, *refs)
            _nsa_tile(p * Q_BLOCK, *refs)

    for step in range(nq // (2 * NSA_PAIRS)):
        pl.when(pl.program_id(2) == step)(functools.partial(pairs, step))


def _nsa(q, cmp, kv, gsig, ovl_t, expand, batch, seq):
    nq = seq // Q_BLOCK
    gw = NSA_REP * NSA_HEAD_DIM
    nslot = cmp.shape[2]
    G = NSA_KV_GROUPS
    kvspec = lambda off: pl.BlockSpec((seq, NSA_HEAD_DIM), lambda b, g, i, off=off: (b, off + g))
    return pl.pallas_call(
        _nsa_kernel,
        grid=(batch, G, nq // (2 * NSA_PAIRS)),
        in_specs=[
            pl.BlockSpec((seq, gw), lambda b, g, i: (b, g)),
            pl.BlockSpec((1, 1, nslot, NSA_HEAD_DIM), lambda b, g, i: (b, g, 0, 0)),
            pl.BlockSpec((1, 1, nslot, NSA_HEAD_DIM), lambda b, g, i: (b, G + g, 0, 0)),
            kvspec(2 * G), kvspec(3 * G), kvspec(4 * G), kvspec(5 * G),
            pl.BlockSpec((seq, GATE_PAD), lambda b, g, i: (b, g)),
            pl.BlockSpec(ovl_t.shape, lambda b, g, i: (0, 0)),
            pl.BlockSpec(expand.shape, lambda b, g, i: (0, 0)),
        ],
        out_specs=pl.BlockSpec((seq, gw), lambda b, g, i: (b, g)),
        out_shape=jax.ShapeDtypeStruct((batch * seq, NSA_WIDTH), bf16),
        compiler_params=pltpu.CompilerParams(
            dimension_semantics=("arbitrary", "arbitrary", "arbitrary"), vmem_limit_bytes=VMEM_LIMIT),
        name="nsa",
    )(q, cmp, cmp, kv, kv, kv, kv, gsig, ovl_t, expand)


def _retention_kernel(q_ref, k_ref, v_ref, gn_ref, o_ref):
    seq = q_ref.shape[0]
    C = RET_CHUNK
    n = lax.broadcasted_iota(jnp.int32, (C, C), 0)
    mcol = lax.broadcasted_iota(jnp.int32, (C, C), 1)
    diff = (n - mcol).astype(f32)
    nvec = lax.broadcasted_iota(jnp.int32, (C, 1), 0).astype(f32)
    decay, xi, zeta, gamma_c = [], [], [], []
    for h in range(RET_HEADS):
        lg = math.log1p(-(2.0 ** (-5.0 - h)))
        decay.append(jnp.where(diff >= 0, jnp.exp(lg * jnp.maximum(diff, 0.0)), 0.0))
        xi.append(jnp.exp(lg * (nvec + 1.0)))
        zeta.append(jnp.exp(lg * (C - 1.0 - nvec)))
        gamma_c.append(math.exp(lg * C))
    gn = gn_ref[...]

    def body(c, states):
        base = pl.multiple_of(c * C, C)
        hs = range(RET_HEADS)
        qc = [q_ref[pl.ds(base, C), h * RET_QK_DIM:(h + 1) * RET_QK_DIM] for h in hs]
        kc = [k_ref[pl.ds(base, C), h * RET_QK_DIM:(h + 1) * RET_QK_DIM] for h in hs]
        vc = [v_ref[pl.ds(base, C), h * RET_V_DIM:(h + 1) * RET_V_DIM] for h in hs]
        qk = [_dot_nt(qc[h], kc[h]) for h in hs]
        cross = [_dot(qc[h], states[h].astype(bf16)) for h in hs]
        kv = [_dot_tn((kc[h].astype(f32) * zeta[h]).astype(bf16), vc[h]) for h in hs]
        intra = [_dot((qk[h] * decay[h]).astype(bf16), vc[h]) for h in hs]
        for h in hs:
            out = intra[h] + cross[h] * xi[h]
            mu = jnp.mean(out, axis=-1, keepdims=True)
            d = out - mu
            var = jnp.mean(d * d, axis=-1, keepdims=True)
            o = d * lax.rsqrt(var + EPS) * gn[:, h * RET_V_DIM:(h + 1) * RET_V_DIM]
            o_ref[pl.ds(base, C), h * RET_V_DIM:(h + 1) * RET_V_DIM] = o.astype(bf16)
        return tuple(states[h] * gamma_c[h] + kv[h] for h in hs)

    s0 = tuple(jnp.zeros((RET_QK_DIM, RET_V_DIM), f32) for _ in range(RET_HEADS))
    lax.fori_loop(0, seq // C, body, s0, unroll=2)


def _retention(rq, rk, rv, ret_gn, batch, seq):
    return pl.pallas_call(
        _retention_kernel,
        grid=(batch,),
        in_specs=[
            pl.BlockSpec((seq, RET_QK_WIDTH), lambda b: (b, 0)),
            pl.BlockSpec((seq, RET_QK_WIDTH), lambda b: (b, 0)),
            pl.BlockSpec((seq, RET_WIDTH), lambda b: (b, 0)),
            pl.BlockSpec((1, RET_WIDTH), lambda b: (0, 0)),
        ],
        out_specs=pl.BlockSpec((seq, RET_WIDTH), lambda b: (b, 0)),
        out_shape=jax.ShapeDtypeStruct((batch * seq, RET_WIDTH), bf16),
        compiler_params=pltpu.CompilerParams(
            dimension_semantics=("arbitrary",), vmem_limit_bytes=VMEM_LIMIT),
        name="retention",
    )(rq, rk, rv, ret_gn)


def _out_proj_kernel(*refs):
    for i in range(refs[0].shape[0] // OUT_ROW_TILE):
        _out_proj_rows(slice(i * OUT_ROW_TILE, (i + 1) * OUT_ROW_TILE), *refs)


def _out_proj_rows(rows, x_ref, on_ref, or_ref, mq_ref, kvm_ref, gs_ref, w_ref, g_ref, o_ref):
    off = NSA_WIDTH + RET_WIDTH
    heads = range(MEM_HEADS)
    hsl = lambda h: slice(h * MEM_HEAD_DIM, (h + 1) * MEM_HEAD_DIM)
    s = [_dot_nt(mq_ref[rows, hsl(h)], kvm_ref[:, hsl(h)]) for h in heads]
    mix_a = jnp.concatenate([on_ref[rows, :] * gs_ref[rows, 0:NSA_WIDTH],
                             or_ref[rows, :] * gs_ref[rows, NSA_WIDTH:off]], axis=1)
    y = _dot(mix_a, w_ref[0:off, :])
    mix_m = []
    for h in heads:
        m = jnp.max(s[h], axis=1, keepdims=True)
        e = jnp.exp2(s[h] - m)
        l = jnp.sum(e, axis=1, keepdims=True)
        om = _dot(e.astype(bf16), kvm_ref[:, MEM_WIDTH + h * MEM_HEAD_DIM:MEM_WIDTH + (h + 1) * MEM_HEAD_DIM])
        mix_m.append((om * (1.0 / l)).astype(bf16)
                     * gs_ref[rows, off + h * MEM_HEAD_DIM:off + (h + 1) * MEM_HEAD_DIM])
    y = y + _dot(jnp.concatenate(mix_m, axis=1), w_ref[off:, :])
    o_ref[rows, :] = x_ref[rows, :] + _rms(y, g_ref[...])


def _out_proj(x2, o_nsa, o_ret, mq, kvm, gsilu, w_out, norm_post, seq):
    bt = x2.shape[0]
    tm = math.gcd(seq, OUT_SUBTILES * OUT_ROW_TILE)
    per_b = seq // tm
    row = lambda i: (i, 0)
    const = lambda i: (0, 0)
    return pl.pallas_call(
        _out_proj_kernel,
        grid=(bt // tm,),
        in_specs=[
            pl.BlockSpec((tm, D_MODEL), row),
            pl.BlockSpec((tm, NSA_WIDTH), row),
            pl.BlockSpec((tm, RET_WIDTH), row),
            pl.BlockSpec((tm, MEM_WIDTH), row),
            pl.BlockSpec((MEM_LEN, 2 * MEM_WIDTH), lambda i: (i // per_b, 0)),
            pl.BlockSpec((tm, MIX_WIDTH), row),
            pl.BlockSpec((MIX_WIDTH, D_MODEL), const),
            pl.BlockSpec((1, D_MODEL), const),
        ],
        out_specs=pl.BlockSpec((tm, D_MODEL), row),
        out_shape=jax.ShapeDtypeStruct((bt, D_MODEL), f32),
        compiler_params=pltpu.CompilerParams(
            dimension_semantics=("arbitrary",), vmem_limit_bytes=VMEM_LIMIT),
        name="out_proj",
    )(x2, o_nsa, o_ret, mq, kvm, gsilu, w_out, norm_post)


def _split_w_in(w):
    n_a = NSA_WIDTH + NSA_KV_ALL
    n_g = 3 * NSA_HEADS
    gates = w[:, n_a:n_a + n_g]
    gates = gates.reshape(D_MODEL, 3, NSA_KV_GROUPS, NSA_REP).transpose(0, 2, 1, 3)
    gates = gates.reshape(D_MODEL, NSA_KV_GROUPS, 3 * NSA_REP)
    gates = jnp.pad(gates, ((0, 0), (0, 0), (0, GATE_PAD - 3 * NSA_REP)))
    gates = gates.reshape(D_MODEL, NSA_KV_GROUPS * GATE_PAD)
    return w[:, :n_a].astype(bf16), w[:, n_a + n_g:].astype(bf16), gates.astype(bf16)


def _overlap_t(seq):
    n_slot = seq // CMP_STRIDE
    n_sb = seq // SEL_BLOCK
    cmp_start = np.arange(n_slot) * CMP_STRIDE
    sel_start = np.arange(n_sb) * SEL_BLOCK
    ov = np.clip(np.minimum(cmp_start[None, :] + CMP_BLOCK, sel_start[:, None] + SEL_BLOCK)
                 - np.maximum(cmp_start[None, :], sel_start[:, None]), 0, None)
    return jnp.asarray(ov.astype(np.float32) / CMP_BLOCK)


def kernel(x, mem, positions, norm_pre, w_in, cmp_pos_k, cmp_w1_k, cmp_w2_k, cmp_pos_v, cmp_w1_v,
           cmp_w2_v, ret_gn, mem_norm, w_mem_kv, w_out, norm_post):
    depth = norm_pre.shape[0]
    batch, seq, _ = x.shape
    assert x.shape[2] == D_MODEL and mem.shape[1:] == (MEM_LEN, D_MODEL)
    assert seq % SLC_CHUNK == 0 and seq >= WINDOW + Q_BLOCK

    half = ROT_DIM // 2
    inv_n = ROPE_THETA ** (-jnp.arange(half, dtype=f32) / half)
    halfr = RET_QK_DIM // 2
    inv_r = RET_THETA ** (-jnp.arange(halfr, dtype=f32) / halfr)
    assert ROT_DIM <= RET_QK_DIM and 2 * RET_QK_DIM == LANES
    inv = jnp.concatenate([inv_n, inv_n, jnp.zeros((RET_QK_DIM - ROT_DIM,), f32), inv_r, inv_r])[None, :]
    ovl_t = _overlap_t(seq)
    expand = jnp.asarray(np.arange(seq)[:, None] // SEL_BLOCK == np.arange(LANES)[None, :], bf16)
    pos2 = positions.reshape(batch * seq, 1)
    mem2 = mem.reshape(batch * MEM_LEN, D_MODEL)

    x2 = x.reshape(batch * seq, D_MODEL)
    for layer in range(depth):
        q, kv, rq, rk, rv, mq, gsilu, gsig = _in_proj(x2, pos2, norm_pre[layer][None, :],
                                                      _split_w_in(w_in[layer]), inv)
        kvm = _mem_kv(mem2, mem_norm[layer][None, :], w_mem_kv[layer].astype(bf16))
        cmp = _compress(
            kv,
            jnp.stack([cmp_pos_k[layer], cmp_pos_v[layer]]),
            jnp.stack([cmp_w1_k[layer], cmp_w1_v[layer]]).astype(bf16),
            jnp.stack([cmp_w2_k[layer], cmp_w2_v[layer]]).astype(bf16),
            batch, seq)
        o_nsa = _nsa(q, cmp, kv, gsig, ovl_t, expand, batch, seq)
        o_ret = _retention(rq, rk, rv, ret_gn[layer][None, :], batch, seq)
        x2 = _out_proj(x2, o_nsa, o_ret, mq, kvm, gsilu, w_out[layer].astype(bf16),
                       norm_post[layer][None, :], seq)
    return x2.reshape(batch, seq, D_MODEL)
```

```python
import functools
import math

import jax
import jax.numpy as jnp
import numpy as np
from jax import lax
from jax.experimental import pallas as pl
from jax.experimental.pallas import tpu as pltpu

D_MODEL = 1024
MEM_LEN = 256
EPS = 1e-6

NSA_HEADS = 8
NSA_KV_GROUPS = 2
NSA_REP = NSA_HEADS // NSA_KV_GROUPS
NSA_HEAD_DIM = 128
CMP_BLOCK = 32
CMP_STRIDE = 16
CMP_HIDDEN = 256
SEL_BLOCK = 64
N_SELECT = 16
WINDOW = 512
Q_BLOCK = 256
ROPE_THETA = 500000.0
ROT_DIM = NSA_HEAD_DIM // 4

RET_HEADS = 4
RET_QK_DIM = 64
RET_V_DIM = 128
RET_CHUNK = 128
RET_THETA = 10000.0

MEM_HEADS = 4
MEM_HEAD_DIM = 128

NSA_WIDTH = NSA_HEADS * NSA_HEAD_DIM
NSA_KV_ALL = 3 * 2 * NSA_KV_GROUPS * NSA_HEAD_DIM
RET_QK_WIDTH = RET_HEADS * RET_QK_DIM
RET_WIDTH = RET_HEADS * RET_V_DIM
MEM_WIDTH = MEM_HEADS * MEM_HEAD_DIM
MIX_WIDTH = NSA_WIDTH + RET_WIDTH + MEM_WIDTH
GATE_PAD = 128

LANES = 128
REDUCE_CHAINS = 4
SUBLANES = 8
NEG = -1e30
LOG2E = math.log2(math.e)
NSA_QSCALE = NSA_HEAD_DIM ** -0.5 * LOG2E
MEM_QSCALE = MEM_HEAD_DIM ** -0.5 * LOG2E
VMEM_LIMIT = 56 * 1024 * 1024

IN_SUBTILES = 2
ROW_TILE = 256
CMP_PITCH = 24
CMP_BATCH = 4
OUT_SUBTILES = 2
OUT_ROW_TILE = 512
NSA_PAIRS = 4
SLC_CHUNK = 512

_NT = (((1,), (1,)), ((), ()))
_TN = (((0,), (0,)), ((), ()))

bf16 = jnp.bfloat16
f32 = jnp.float32


def _dot(a, b):
    return jnp.dot(a, b, preferred_element_type=f32)


def _dot_nt(a, b):
    return lax.dot_general(a, b, _NT, preferred_element_type=f32)


def _dot_tn(a, b):
    return lax.dot_general(a, b, _TN, preferred_element_type=f32)


def _col_reduce(op, x):
    slabs = [x[i:i + SUBLANES] for i in range(0, x.shape[0], SUBLANES)]
    accs = slabs[:REDUCE_CHAINS]
    for i in range(REDUCE_CHAINS, len(slabs)):
        accs[i % REDUCE_CHAINS] = op(accs[i % REDUCE_CHAINS], slabs[i])
    slabs = accs
    while len(slabs) > 1:
        nxt = [op(slabs[i], slabs[i + 1]) for i in range(0, len(slabs) - 1, 2)]
        if len(slabs) % 2:
            nxt.append(slabs[-1])
        slabs = nxt
    red = jnp.max if op is jnp.maximum else jnp.sum
    return red(slabs[0], axis=0, keepdims=True)


def _rms(x, g):
    return x * lax.rsqrt(jnp.mean(x * x, axis=-1, keepdims=True) + EPS) * g


def _rotate_heads(acc, c, s1, s2, shift):
    outs = []
    for j in range(acc.shape[1] // LANES):
        a = acc[:, j * LANES:(j + 1) * LANES]
        outs.append(a * c + pltpu.roll(a, shift, 1) * s1 + pltpu.roll(a, LANES - shift, 1) * s2)
    return outs


def _in_proj_kernel(*refs):
    for i in range(refs[0].shape[0] // ROW_TILE):
        _in_proj_rows(slice(i * ROW_TILE, (i + 1) * ROW_TILE), *refs)


def _in_proj_rows(rows, x_ref, pos_ref, g_ref, wa_ref, w_ref, wg_ref, inv_ref,
                  q_ref, kv_ref, rq_ref, rk_ref, rv_ref, mq_ref, gs_ref, gg_ref):
    x = x_ref[rows, :]
    hb = _rms(x, g_ref[...]).astype(bf16)
    c_kv = NSA_WIDTH
    c_rqk = 0
    c_rv = c_rqk + 2 * RET_QK_WIDTH
    c_mq = c_rv + RET_WIDTH
    c_gate = c_mq + MEM_WIDTH

    pos = pos_ref[rows, :].astype(f32)
    lane = lax.broadcasted_iota(jnp.int32, (x.shape[0], LANES), 1)
    ang = pos * inv_ref[...]
    cs, sn = jnp.cos(ang), jnp.sin(ang)
    half = ROT_DIM // 2
    cn = jnp.where(lane < ROT_DIM, cs, 1.0)
    s1n = jnp.where((lane >= half) & (lane < ROT_DIM), sn, 0.0)
    s2n = jnp.where(lane < half, -sn, 0.0)
    halfr = RET_QK_DIM // 2
    cr = jnp.where(lane >= RET_QK_DIM, cs, pltpu.roll(cs, RET_QK_DIM, 1))
    sr = jnp.where(lane >= RET_QK_DIM, sn, pltpu.roll(sn, RET_QK_DIM, 1))
    lr = lane & (RET_QK_DIM - 1)
    s1r = jnp.where(lr >= halfr, sr, 0.0)
    s2r = jnp.where(lr < halfr, -sr, 0.0)

    for j in range(MIX_WIDTH // 512):
        acc = _dot(hb, w_ref[:, c_gate + j * 512:c_gate + (j + 1) * 512])
        gs_ref[rows, j * 512:(j + 1) * 512] = (acc * (1.0 / (1.0 + jnp.exp(-acc)))).astype(bf16)
    mq_ref[rows, :] = (_dot(hb, w_ref[:, c_mq:c_mq + MEM_WIDTH]) * MEM_QSCALE).astype(bf16)
    acc = _dot(hb, wg_ref[...])
    gg_ref[rows, :] = 1.0 / (1.0 + jnp.exp(-acc))

    for j in range(NSA_WIDTH // 512):
        acc = _dot(hb, wa_ref[:, j * 512:(j + 1) * 512])
        for i, o in enumerate(_rotate_heads(acc, cn * NSA_QSCALE, s1n * NSA_QSCALE, s2n * NSA_QSCALE, half)):
            q_ref[rows, j * 512 + i * LANES: j * 512 + (i + 1) * LANES] = o.astype(bf16)
    for br in range(3):
        acc = _dot(hb, wa_ref[:, c_kv + br * 512:c_kv + (br + 1) * 512])
        for i, o in enumerate(_rotate_heads(acc[:, :256], cn, s1n, s2n, half)):
            kv_ref[rows, br * 512 + i * LANES: br * 512 + (i + 1) * LANES] = o.astype(bf16)
        kv_ref[rows, br * 512 + 256: br * 512 + 512] = acc[:, 256:].astype(bf16)
    acc = _dot(hb, w_ref[:, c_rqk:c_rqk + 2 * RET_QK_WIDTH])
    rot = _rotate_heads(acc, cr, s1r, s2r, halfr)
    for i in range(2):
        rq_ref[rows, i * LANES:(i + 1) * LANES] = rot[i].astype(bf16)
        rk_ref[rows, i * LANES:(i + 1) * LANES] = (rot[2 + i] * (RET_QK_DIM ** -0.5)).astype(bf16)

    rv_ref[rows, :] = _dot(hb, w_ref[:, c_rv:c_rv + RET_WIDTH]).astype(bf16)


def _in_proj(x2, pos2, norm_pre, w_parts, inv):
    bt = x2.shape[0]
    tm = IN_SUBTILES * ROW_TILE
    row = lambda i: (i, 0)
    const = lambda i: (0, 0)
    widths = (NSA_WIDTH, NSA_KV_ALL, RET_QK_WIDTH, RET_QK_WIDTH, RET_WIDTH, MEM_WIDTH, MIX_WIDTH)
    out_shape = [jax.ShapeDtypeStruct((bt, w), bf16) for w in widths]
    out_shape.append(jax.ShapeDtypeStruct((bt, 2 * GATE_PAD), f32))
    out_specs = [pl.BlockSpec((tm, w), row) for w in widths] + [pl.BlockSpec((tm, 2 * GATE_PAD), row)]
    return pl.pallas_call(
        _in_proj_kernel,
        grid=(bt // tm,),
        in_specs=[
            pl.BlockSpec((tm, D_MODEL), row),
            pl.BlockSpec((tm, 1), row),
            pl.BlockSpec((1, D_MODEL), const),
            *[pl.BlockSpec(w.shape, const) for w in w_parts],
            pl.BlockSpec((1, LANES), const),
        ],
        out_specs=out_specs,
        out_shape=out_shape,
        compiler_params=pltpu.CompilerParams(
            dimension_semantics=("arbitrary",), vmem_limit_bytes=VMEM_LIMIT),
        name="in_proj",
    )(x2, pos2, norm_pre, *w_parts, inv)


def _mem_kv_kernel(m_ref, g_ref, w_ref, o_ref):
    hb = _rms(m_ref[...], g_ref[...]).astype(bf16)
    o_ref[...] = _dot(hb, w_ref[...]).astype(bf16)


def _mem_kv(mem2, mem_norm, w_kv):
    n = mem2.shape[0]
    tm = math.gcd(n, 4 * MEM_LEN)
    return pl.pallas_call(
        _mem_kv_kernel,
        grid=(n // tm,),
        in_specs=[
            pl.BlockSpec((tm, D_MODEL), lambda i: (i, 0)),
            pl.BlockSpec((1, D_MODEL), lambda i: (0, 0)),
            pl.BlockSpec((D_MODEL, 2 * MEM_WIDTH), lambda i: (0, 0)),
        ],
        out_specs=pl.BlockSpec((tm, 2 * MEM_WIDTH), lambda i: (i, 0)),
        out_shape=jax.ShapeDtypeStruct((n, 2 * MEM_WIDTH), bf16),
        compiler_params=pltpu.CompilerParams(
            dimension_semantics=("arbitrary",), vmem_limit_bytes=VMEM_LIMIT),
        name="mem_kv",
    )(mem2, mem_norm, w_kv)


def _compress_kernel(kv_ref, pos_ref, w1_ref, w2_ref, o_ref, xf_ref):
    nb, _, per, _ = o_ref.shape
    nslot = nb * per
    for i in range(nslot):
        xf_ref[i * CMP_PITCH:i * CMP_PITCH + CMP_STRIDE, :] = (
            kv_ref[i * CMP_STRIDE:(i + 1) * CMP_STRIDE, :].astype(f32))
    pos = pos_ref[0]
    la, lb = [], []
    for p in range(CMP_STRIDE):
        a = xf_ref[pl.ds(p, nslot, stride=CMP_PITCH), :]
        la.append((a + pos[p:p + 1, :]).astype(bf16))
        lb.append((a + pos[CMP_STRIDE + p:CMP_STRIDE + p + 1, :]).astype(bf16))
    khalf = CMP_STRIDE * NSA_HEAD_DIM
    ha = _dot(jnp.concatenate(la, axis=1), w1_ref[0, 0:khalf, :])
    hb = _dot(jnp.concatenate(lb, axis=1), w1_ref[0, khalf:2 * khalf, :])
    h = ha + pltpu.roll(hb, nslot - 1, 0)
    h = jax.nn.gelu(h)
    out = _dot(h.astype(bf16), w2_ref[0])
    slot = lax.broadcasted_iota(jnp.int32, out.shape, 0) & (per - 1)
    out = jnp.where(slot < per - 1, out, 0.0).astype(bf16)
    for b in range(nb):
        o_ref[b, 0] = out[b * per:(b + 1) * per]


def _compress(kv, cmp_pos, cmp_w1, cmp_w2, batch, seq):
    nslot = seq // CMP_STRIDE
    assert nslot & (nslot - 1) == 0
    nb = math.gcd(batch, CMP_BATCH)
    kinds = 2 * NSA_KV_GROUPS
    return pl.pallas_call(
        _compress_kernel,
        grid=(kinds, batch // nb),
        in_specs=[
            pl.BlockSpec((nb * seq, NSA_HEAD_DIM), lambda j, b: (b, j)),
            pl.BlockSpec((1, CMP_BLOCK, NSA_HEAD_DIM), lambda j, b: (j // NSA_KV_GROUPS, 0, 0)),
            pl.BlockSpec((1, CMP_BLOCK * NSA_HEAD_DIM, CMP_HIDDEN), lambda j, b: (j // NSA_KV_GROUPS, 0, 0)),
            pl.BlockSpec((1, CMP_HIDDEN, NSA_HEAD_DIM), lambda j, b: (j // NSA_KV_GROUPS, 0, 0)),
        ],
        out_specs=pl.BlockSpec((nb, 1, nslot, NSA_HEAD_DIM), lambda j, b: (b, j, 0, 0)),
        out_shape=jax.ShapeDtypeStruct((batch, kinds, nslot, NSA_HEAD_DIM), bf16),
        scratch_shapes=[pltpu.VMEM((nb * nslot * CMP_PITCH, NSA_HEAD_DIM), f32)],
        compiler_params=pltpu.CompilerParams(
            dimension_semantics=("arbitrary", "arbitrary"), vmem_limit_bytes=VMEM_LIMIT),
        name="compress",
    )(kv, cmp_pos, cmp_w1, cmp_w2)


def _nsa_tile(t0, q_ref, kc_ref, vc_ref, ks_ref, vs_ref, kw_ref, vw_ref, g_ref, ovl_ref, exp_ref, o_ref):
    n_static = (t0 + Q_BLOCK + SLC_CHUNK - 1) // SLC_CHUNK
    nslot = kc_ref.shape[2]
    n_cmp = nslot - 1
    heads = range(NSA_REP)
    hcols = lambda a, r: a[:, r * Q_BLOCK:(r + 1) * Q_BLOCK]

    qrows = slice(t0, t0 + Q_BLOCK)
    q = q_ref[qrows, :]
    q4 = jnp.concatenate([q[:, r * NSA_HEAD_DIM:(r + 1) * NSA_HEAD_DIM] for r in heads], axis=0)
    tq = t0 + lax.broadcasted_iota(jnp.int32, (1, Q_BLOCK), 1)

    def softmax_cols(s_r):
        m = _col_reduce(jnp.maximum, s_r)
        e = jnp.exp2(s_r - m)
        return e, _col_reduce(jnp.add, e)

    s = _dot_nt(kc_ref[0, 0], q4)
    start = max(t0 - WINDOW, 0)
    wlen = t0 + Q_BLOCK - start
    sw = _dot_nt(kw_ref[start:start + wlen, :], q4)
    sc0 = _dot_nt(ks_ref[0:min(SLC_CHUNK, t0 + Q_BLOCK), :], q4)
    nrow = lax.broadcasted_iota(jnp.int32, (nslot, 1), 0)
    bias_c = jnp.where((nrow * CMP_STRIDE + (CMP_BLOCK - 1) <= tq) & (nrow < n_cmp), 0.0, NEG)
    row_ok = tq >= CMP_BLOCK - 1
    ps, psum = [], None
    for r in heads:
        e, l = softmax_cols(hcols(s, r) + bias_c)
        p = e * jnp.where(row_ok, 1.0 / l, 0.0)
        ps.append(p.astype(bf16))
        psum = p if psum is None else psum + p
    o_cmp = _dot_tn(vc_ref[0, 0], jnp.concatenate(ps, axis=1))

    imp_t = jnp.dot(ovl_ref[...], psum, precision=lax.Precision.HIGHEST,
                    preferred_element_type=f32)
    n_sb = imp_t.shape[0]
    jblk = lax.broadcasted_iota(jnp.int32, (n_sb, Q_BLOCK), 0)
    tl = t0 + lax.broadcasted_iota(jnp.int32, (n_sb, Q_BLOCK), 1)
    cur = tl >> 6
    forced = (jblk == 0) | (jblk == cur) | (jblk == cur - 1)
    valid = jblk * SEL_BLOCK <= tl
    v = jnp.where(forced, jnp.inf, jnp.where(valid, imp_t, -jnp.inf))
    n_live = min(n_sb, n_static * SLC_CHUNK // SEL_BLOCK)
    slabs = [v[k:k + SUBLANES] for k in range(0, n_live, SUBLANES)]
    jl = lax.broadcasted_iota(jnp.int32, (SUBLANES, Q_BLOCK), 0)
    cnts = [jnp.zeros((SUBLANES, Q_BLOCK), f32) for _ in slabs]
    for i in range(n_live):
        vi = jnp.broadcast_to(v[i:i + 1, :], (SUBLANES, Q_BLOCK))
        for k, vk in enumerate(slabs):
            if k * SUBLANES > i:
                ahead = vi >= vk
            elif (k + 1) * SUBLANES <= i:
                ahead = vi > vk
            else:
                ahead = (vi > vk) | ((vi == vk) & (jl > i - k * SUBLANES))
            cnts[k] = cnts[k] + jnp.where(ahead, 1.0, 0.0)
    selbias_pad = jnp.concatenate([jnp.where(c < float(N_SELECT), 0.0, NEG) for c in cnts]
                                  + [jnp.zeros((LANES - n_live, Q_BLOCK), f32)], axis=0)
    q4_aug = jnp.concatenate([q4, jnp.concatenate([selbias_pad.T.astype(bf16)] * NSA_REP, axis=0)], axis=1)

    def gate_rows(br):
        return jnp.concatenate([gs[br * NSA_REP + r:br * NSA_REP + r + 1, :] for r in heads], axis=1)

    kl = lax.broadcasted_iota(jnp.int32, (Q_BLOCK, Q_BLOCK), 0)
    ql = lax.broadcasted_iota(jnp.int32, (Q_BLOCK, Q_BLOCK), 1)
    head_bias = jnp.where(kl > ql, 0.0, NEG)
    tail_bias = jnp.where(kl <= ql, 0.0, NEG)

    def add_tail_bias(a):
        body = a.shape[0] - Q_BLOCK
        tail = a[body:] + tail_bias
        return tail if body == 0 else jnp.concatenate([a[:body], tail], axis=0)

    def add_bias_w(a):
        if t0 >= WINDOW:
            a = jnp.concatenate([a[:Q_BLOCK] + head_bias, a[Q_BLOCK:]], axis=0)
        return add_tail_bias(a)

    gs = g_ref[qrows, :].T
    es, ls = [], []
    for r in heads:
        e, l = softmax_cols(add_bias_w(hcols(sw, r)))
        es.append(e.astype(bf16))
        ls.append(l)
    o_win = _dot_tn(vw_ref[start:start + wlen, :], jnp.concatenate(es, axis=1))
    o_part = (gate_rows(0) * o_cmp
              + (gate_rows(2) * (1.0 / jnp.concatenate(ls, axis=1))) * o_win)

    def slc_branch(n):
        use_sel = t0 + Q_BLOCK > N_SELECT * SEL_BLOCK
        bounds = [(c * SLC_CHUNK, min((c + 1) * SLC_CHUNK, t0 + Q_BLOCK)) for c in range(n)]

        def slc_scores(c):
            lo, hi = bounds[c]
            ks = ks_ref[lo:hi, :]
            if not use_sel:
                return _dot_nt(ks, q4)
            return _dot_nt(jnp.concatenate([ks, exp_ref[lo:hi, :]], axis=1), q4_aug)

        m_i, l_i, acc = [None] * NSA_REP, [None] * NSA_REP, None
        sc = slc_scores(0) if use_sel else sc0
        for c in range(n):
            sc_next = slc_scores(c + 1) if c + 1 < n else None
            es, alphas = [], []
            for r in heads:
                s_r = add_tail_bias(hcols(sc, r)) if c == n - 1 else hcols(sc, r)
                m_c = _col_reduce(jnp.maximum, s_r)
                if c == 0:
                    e = jnp.exp2(s_r - m_c)
                    m_i[r], l_i[r] = m_c, _col_reduce(jnp.add, e)
                else:
                    m_new = jnp.maximum(m_i[r], m_c)
                    alpha = jnp.exp2(m_i[r] - m_new)
                    e = jnp.exp2(s_r - m_new)
                    m_i[r], l_i[r] = m_new, alpha * l_i[r] + _col_reduce(jnp.add, e)
                    alphas.append(alpha)
                es.append(e.astype(bf16))
            pv = _dot_tn(vs_ref[bounds[c][0]:bounds[c][1], :], jnp.concatenate(es, axis=1))
            acc = pv if c == 0 else jnp.concatenate(alphas, axis=1) * acc + pv
            sc = sc_next
        o = o_part + (gate_rows(1) * (1.0 / jnp.concatenate(l_i, axis=1))) * acc
        for r in heads:
            o_ref[qrows, r * NSA_HEAD_DIM:(r + 1) * NSA_HEAD_DIM] = hcols(o, r).T.astype(bf16)

    slc_branch(n_static)


def _nsa_kernel(*refs):
    seq = refs[3].shape[0]
    nq = seq // Q_BLOCK

    def pairs(step):
        for p in range(step * NSA_PAIRS, (step + 1) * NSA_PAIRS):
            _nsa_tile((nq - 1 - p) * Q_BLOCK, *refs)
            _nsa_tile(p * Q_BLOCK, *refs)

    for step in range(nq // (2 * NSA_PAIRS)):
        pl.when(pl.program_id(2) == step)(functools.partial(pairs, step))


def _nsa(q, cmp, kv, gsig, ovl_t, expand, batch, seq):
    nq = seq // Q_BLOCK
    gw = NSA_REP * NSA_HEAD_DIM
    nslot = cmp.shape[2]
    G = NSA_KV_GROUPS
    kvspec = lambda off: pl.BlockSpec((seq, NSA_HEAD_DIM), lambda b, g, i, off=off: (b, off + g))
    return pl.pallas_call(
        _nsa_kernel,
        grid=(batch, G, nq // (2 * NSA_PAIRS)),
        in_specs=[
            pl.BlockSpec((seq, gw), lambda b, g, i: (b, g)),
            pl.BlockSpec((1, 1, nslot, NSA_HEAD_DIM), lambda b, g, i: (b, g, 0, 0)),
            pl.BlockSpec((1, 1, nslot, NSA_HEAD_DIM), lambda b, g, i: (b, G + g, 0, 0)),
            kvspec(2 * G), kvspec(3 * G), kvspec(4 * G), kvspec(5 * G),
            pl.BlockSpec((seq, GATE_PAD), lambda b, g, i: (b, g)),
            pl.BlockSpec(ovl_t.shape, lambda b, g, i: (0, 0)),
            pl.BlockSpec(expand.shape, lambda b, g, i: (0, 0)),
        ],
        out_specs=pl.BlockSpec((seq, gw), lambda b, g, i: (b, g)),
        out_shape=jax.ShapeDtypeStruct((batch * seq, NSA_WIDTH), bf16),
        compiler_params=pltpu.CompilerParams(
            dimension_semantics=("arbitrary", "arbitrary", "arbitrary"), vmem_limit_bytes=VMEM_LIMIT),
        name="nsa",
    )(q, cmp, cmp, kv, kv, kv, kv, gsig, ovl_t, expand)


def _retention_kernel(q_ref, k_ref, v_ref, gn_ref, o_ref):
    seq = q_ref.shape[0]
    C = RET_CHUNK
    n = lax.broadcasted_iota(jnp.int32, (C, C), 0)
    mcol = lax.broadcasted_iota(jnp.int32, (C, C), 1)
    diff = (n - mcol).astype(f32)
    nvec = lax.broadcasted_iota(jnp.int32, (C, 1), 0).astype(f32)
    decay, xi, zeta, gamma_c = [], [], [], []
    for h in range(RET_HEADS):
        lg = math.log1p(-(2.0 ** (-5.0 - h)))
        decay.append(jnp.where(diff >= 0, jnp.exp(lg * jnp.maximum(diff, 0.0)), 0.0))
        xi.append(jnp.exp(lg * (nvec + 1.0)))
        zeta.append(jnp.exp(lg * (C - 1.0 - nvec)))
        gamma_c.append(math.exp(lg * C))
    gn = gn_ref[...]

    def body(c, states):
        base = pl.multiple_of(c * C, C)
        hs = range(RET_HEADS)
        qc = [q_ref[pl.ds(base, C), h * RET_QK_DIM:(h + 1) * RET_QK_DIM] for h in hs]
        kc = [k_ref[pl.ds(base, C), h * RET_QK_DIM:(h + 1) * RET_QK_DIM] for h in hs]
        vc = [v_ref[pl.ds(base, C), h * RET_V_DIM:(h + 1) * RET_V_DIM] for h in hs]
        qk = [_dot_nt(qc[h], kc[h]) for h in hs]
        cross = [_dot(qc[h], states[h].astype(bf16)) for h in hs]
        kv = [_dot_tn((kc[h].astype(f32) * zeta[h]).astype(bf16), vc[h]) for h in hs]
        intra = [_dot((qk[h] * decay[h]).astype(bf16), vc[h]) for h in hs]
        for h in hs:
            out = intra[h] + cross[h] * xi[h]
            mu = jnp.mean(out, axis=-1, keepdims=True)
            d = out - mu
            var = jnp.mean(d * d, axis=-1, keepdims=True)
            o = d * lax.rsqrt(var + EPS) * gn[:, h * RET_V_DIM:(h + 1) * RET_V_DIM]
            o_ref[pl.ds(base, C), h * RET_V_DIM:(h + 1) * RET_V_DIM] = o.astype(bf16)
        return tuple(states[h] * gamma_c[h] + kv[h] for h in hs)

    s0 = tuple(jnp.zeros((RET_QK_DIM, RET_V_DIM), f32) for _ in range(RET_HEADS))
    lax.fori_loop(0, seq // C, body, s0, unroll=2)


def _retention(rq, rk, rv, ret_gn, batch, seq):
    return pl.pallas_call(
        _retention_kernel,
        grid=(batch,),
        in_specs=[
            pl.BlockSpec((seq, RET_QK_WIDTH), lambda b: (b, 0)),
            pl.BlockSpec((seq, RET_QK_WIDTH), lambda b: (b, 0)),
            pl.BlockSpec((seq, RET_WIDTH), lambda b: (b, 0)),
            pl.BlockSpec((1, RET_WIDTH), lambda b: (0, 0)),
        ],
        out_specs=pl.BlockSpec((seq, RET_WIDTH), lambda b: (b, 0)),
        out_shape=jax.ShapeDtypeStruct((batch * seq, RET_WIDTH), bf16),
        compiler_params=pltpu.CompilerParams(
            dimension_semantics=("arbitrary",), vmem_limit_bytes=VMEM_LIMIT),
        name="retention",
    )(rq, rk, rv, ret_gn)


def _out_proj_kernel(*refs):
    for i in range(refs[0].shape[0] // OUT_ROW_TILE):
        _out_proj_rows(slice(i * OUT_ROW_TILE, (i + 1) * OUT_ROW_TILE), *refs)


def _out_proj_rows(rows, x_ref, on_ref, or_ref, mq_ref, kvm_ref, gs_ref, w_ref, g_ref, o_ref):
    off = NSA_WIDTH + RET_WIDTH
    heads = range(MEM_HEADS)
    hsl = lambda h: slice(h * MEM_HEAD_DIM, (h + 1) * MEM_HEAD_DIM)
    s = [_dot_nt(mq_ref[rows, hsl(h)], kvm_ref[:, hsl(h)]) for h in heads]
    mix_a = jnp.concatenate([on_ref[rows, :] * gs_ref[rows, 0:NSA_WIDTH],
                             or_ref[rows, :] * gs_ref[rows, NSA_WIDTH:off]], axis=1)
    y = _dot(mix_a, w_ref[0:off, :])
    mix_m = []
    for h in heads:
        m = jnp.max(s[h], axis=1, keepdims=True)
        e = jnp.exp2(s[h] - m)
        l = jnp.sum(e, axis=1, keepdims=True)
        om = _dot(e.astype(bf16), kvm_ref[:, MEM_WIDTH + h * MEM_HEAD_DIM:MEM_WIDTH + (h + 1) * MEM_HEAD_DIM])
        mix_m.append((om * (1.0 / l)).astype(bf16)
                     * gs_ref[rows, off + h * MEM_HEAD_DIM:off + (h + 1) * MEM_HEAD_DIM])
    y = y + _dot(jnp.concatenate(mix_m, axis=1), w_ref[off:, :])
    o_ref[rows, :] = x_ref[rows, :] + _rms(y, g_ref[...])


def _out_proj(x2, o_nsa, o_ret, mq, kvm, gsilu, w_out, norm_post, seq):
    bt = x2.shape[0]
    tm = math.gcd(seq, OUT_SUBTILES * OUT_ROW_TILE)
    per_b = seq // tm
    row = lambda i: (i, 0)
    const = lambda i: (0, 0)
    return pl.pallas_call(
        _out_proj_kernel,
        grid=(bt // tm,),
        in_specs=[
            pl.BlockSpec((tm, D_MODEL), row),
            pl.BlockSpec((tm, NSA_WIDTH), row),
            pl.BlockSpec((tm, RET_WIDTH), row),
            pl.BlockSpec((tm, MEM_WIDTH), row),
            pl.BlockSpec((MEM_LEN, 2 * MEM_WIDTH), lambda i: (i // per_b, 0)),
            pl.BlockSpec((tm, MIX_WIDTH), row),
            pl.BlockSpec((MIX_WIDTH, D_MODEL), const),
            pl.BlockSpec((1, D_MODEL), const),
        ],
        out_specs=pl.BlockSpec((tm, D_MODEL), row),
        out_shape=jax.ShapeDtypeStruct((bt, D_MODEL), f32),
        compiler_params=pltpu.CompilerParams(
            dimension_semantics=("arbitrary",), vmem_limit_bytes=VMEM_LIMIT),
        name="out_proj",
    )(x2, o_nsa, o_ret, mq, kvm, gsilu, w_out, norm_post)


def _split_w_in(w):
    n_a = NSA_WIDTH + NSA_KV_ALL
    n_g = 3 * NSA_HEADS
    gates = w[:, n_a:n_a + n_g]
    gates = gates.reshape(D_MODEL, 3, NSA_KV_GROUPS, NSA_REP).transpose(0, 2, 1, 3)
    gates = gates.reshape(D_MODEL, NSA_KV_GROUPS, 3 * NSA_REP)
    gates = jnp.pad(gates, ((0, 0), (0, 0), (0, GATE_PAD - 3 * NSA_REP)))
    gates = gates.reshape(D_MODEL, NSA_KV_GROUPS * GATE_PAD)
    return w[:, :n_a].astype(bf16), w[:, n_a + n_g:].astype(bf16), gates.astype(bf16)


def _overlap_t(seq):
    n_slot = seq // CMP_STRIDE
    n_sb = seq // SEL_BLOCK
    cmp_start = np.arange(n_slot) * CMP_STRIDE
    sel_start = np.arange(n_sb) * SEL_BLOCK
    ov = np.clip(np.minimum(cmp_start[None, :] + CMP_BLOCK, sel_start[:, None] + SEL_BLOCK)
                 - np.maximum(cmp_start[None, :], sel_start[:, None]), 0, None)
    return jnp.asarray(ov.astype(np.float32) / CMP_BLOCK)


def kernel(x, mem, positions, norm_pre, w_in, cmp_pos_k, cmp_w1_k, cmp_w2_k, cmp_pos_v, cmp_w1_v,
           cmp_w2_v, ret_gn, mem_norm, w_mem_kv, w_out, norm_post):
    depth = norm_pre.shape[0]
    batch, seq, _ = x.shape
    assert x.shape[2] == D_MODEL and mem.shape[1:] == (MEM_LEN, D_MODEL)
    assert seq % SLC_CHUNK == 0 and seq >= WINDOW + Q_BLOCK
    assert Q_BLOCK % LANES == 0 and WINDOW % Q_BLOCK == 0 and SLC_CHUNK % Q_BLOCK == 0
    assert seq % (2 * NSA_PAIRS * Q_BLOCK) == 0 and seq // SEL_BLOCK <= LANES
    assert seq % RET_CHUNK == 0 and seq % CMP_STRIDE == 0 and Q_BLOCK % SEL_BLOCK == 0
    assert (batch * seq) % (IN_SUBTILES * ROW_TILE) == 0

    half = ROT_DIM // 2
    inv_n = ROPE_THETA ** (-jnp.arange(half, dtype=f32) / half)
    halfr = RET_QK_DIM // 2
    inv_r = RET_THETA ** (-jnp.arange(halfr, dtype=f32) / halfr)
    assert ROT_DIM <= RET_QK_DIM and 2 * RET_QK_DIM == LANES
    inv = jnp.concatenate([inv_n, inv_n, jnp.zeros((RET_QK_DIM - ROT_DIM,), f32), inv_r, inv_r])[None, :]
    ovl_t = _overlap_t(seq)
    expand = jnp.asarray(np.arange(seq)[:, None] // SEL_BLOCK == np.arange(LANES)[None, :], bf16)
    pos2 = positions.reshape(batch * seq, 1)
    mem2 = mem.reshape(batch * MEM_LEN, D_MODEL)

    x2 = x.reshape(batch * seq, D_MODEL)
    for layer in range(depth):
        q, kv, rq, rk, rv, mq, gsilu, gsig = _in_proj(x2, pos2, norm_pre[layer][None, :],
                                                      _split_w_in(w_in[layer]), inv)
        kvm = _mem_kv(mem2, mem_norm[layer][None, :], w_mem_kv[layer].astype(bf16))
        cmp = _compress(
            kv,
            jnp.stack([cmp_pos_k[layer], cmp_pos_v[layer]]),
            jnp.stack([cmp_w1_k[layer], cmp_w1_v[layer]]).astype(bf16),
            jnp.stack([cmp_w2_k[layer], cmp_w2_v[layer]]).astype(bf16),
            batch, seq)
        o_nsa = _nsa(q, cmp, kv, gsig, ovl_t, expand, batch, seq)
        o_ret = _retention(rq, rk, rv, ret_gn[layer][None, :], batch, seq)
        x2 = _out_proj(x2, o_nsa, o_ret, mq, kvm, gsilu, w_out[layer].astype(bf16),
                       norm_post[layer][None, :], seq)
    return x2.reshape(batch, seq, D_MODEL)
```
